```python
import math
import jax, jax.numpy as jnp
from jax import lax
import numpy as np

D_MODEL = 2048
BATCH = 2
SEQ = 4096
DEPTH = 4
DEC_BATCH = 8
DEC_SEQ = 1
PAST_LEN = 16384
PAGE_SIZE = 128

N_MIXERS = 2
N_GMLP_LAYERS = (DEPTH + 1) // 2
N_NSA_LAYERS = DEPTH // 2
N_ADA = 9
D_FF = 5632
GMLP_DIM = 3 * D_MODEL
GMLP_GROUPS = 16
GMLP_GW = GMLP_DIM // GMLP_GROUPS
CHUNK = 128
N_HEADS = 16
HEAD_DIM = D_MODEL // N_HEADS
N_KV = 4
HPG = N_HEADS // N_KV
KV_COLS = N_KV * HEAD_DIM
NSA_IN_COLS = N_HEADS * HEAD_DIM + 6 * KV_COLS + 3 * N_HEADS
CMP_LEN = 32
CMP_STRIDE = 16
CMP_HID = 2 * HEAD_DIM
SLC_LEN = 64
N_SEL = 16
WINDOW = 512
Q_BLOCK = 128
REL_BUCKETS = 32
REL_MAX_DIST = 128
LN_EPS = 1e-5
DN_ALPHA = (2 * DEPTH) ** 0.25
DN_BETA = (8 * DEPTH) ** -0.25
FORCE_BONUS = 1e4

kernel_name = "hybrid_gmlp_nsa_macaron_deepnorm_step"


def _layer_norm(x, g, b):
    xf = x.astype(jnp.float32)
    mu = jnp.mean(xf, -1, keepdims=True)
    var = jnp.mean(jnp.square(xf - mu), -1, keepdims=True)
    return ((xf - mu) * lax.rsqrt(var + LN_EPS) * g + b).astype(x.dtype)


def _modulate(x, shift, scale):
    return x * (1 + scale) + shift


def _ada(c, w, b):
    return (jax.nn.silu(c) @ w + b).reshape(c.shape[0], N_ADA, 1, D_MODEL)


def _residual(x, y, gate, g, b):
    return _layer_norm(DN_ALPHA * x + (1 + gate) * y, g, b)


def _swiglu(h, w_in, w_out):
    gate, up = jnp.split(h @ w_in, 2, axis=-1)
    return (jax.nn.silu(gate) * up) @ w_out


def _ffn_sublayer(x, mod, k, w_in, w_out, g, b):
    h = _modulate(x, mod[:, k], mod[:, k + 1])
    return _residual(x, 0.5 * _swiglu(h, w_in, w_out), mod[:, k + 2], g, b)


def _rel_bucket(dist):
    dist = jnp.maximum(dist, 0)
    exact = REL_BUCKETS // 2
    logv = jnp.log(jnp.maximum(dist, 1).astype(jnp.float32) / exact) / math.log(REL_MAX_DIST / exact)
    large = exact + (logv * (REL_BUCKETS - exact)).astype(jnp.int32)
    return jnp.where(dist < exact, dist, jnp.minimum(large, REL_BUCKETS - 1))


def _masked_softmax(s, mask):
    s = jnp.where(mask, s, -jnp.inf)
    m = jnp.max(s, axis=-1, keepdims=True)
    m = jnp.where(jnp.isfinite(m), m, 0.0)
    e = jnp.exp(s - m)
    d = jnp.sum(e, axis=-1, keepdims=True)
    return e / jnp.where(d > 0, d, 1.0)


def _blocked(fn, xs):
    t = xs[0].shape[0]
    if t <= Q_BLOCK:
        return fn(xs)
    nb = t // Q_BLOCK
    outs = lax.map(fn, tuple(a.reshape((nb, Q_BLOCK) + a.shape[1:]) for a in xs))
    return tuple(o.reshape((t,) + o.shape[2:]) for o in outs)


def _gmlp_mixer(h, w_in, ln_g, ln_b, w_s, b_s, w_out):
    bsz, t, _ = h.shape
    u, v = jnp.split(jax.nn.gelu(h @ w_in), 2, axis=-1)
    v = _layer_norm(v, ln_g, ln_b)
    c = min(CHUNK, t)
    tri = jnp.tril(jnp.ones((c, c), dtype=bool))
    ws = jnp.where(tri, w_s[:, :c, :c], 0)
    vc = v.reshape(bsz, t // c, c, GMLP_GROUPS, GMLP_GW)
    s = jnp.einsum('gts,bnsgw->bntgw', ws, vc) + b_s[:, :c].T[None, None, :, :, None]
    return (u * s.reshape(bsz, t, GMLP_DIM)) @ w_out, v


def _nsa_project(h, w_in):
    bsz, t, _ = h.shape
    proj = h @ w_in
    qd = N_HEADS * HEAD_DIM
    q = proj[..., :qd].reshape(bsz, t, N_KV, HPG, HEAD_DIM).transpose(1, 0, 2, 3, 4)
    kv = proj[..., qd:qd + 6 * KV_COLS].reshape(bsz, t, 3, 2, N_KV, HEAD_DIM)
    gates = jax.nn.sigmoid(proj[..., qd + 6 * KV_COLS:].reshape(bsz, t, 3, N_HEADS))
    return q, kv[:, :, 0], kv[:, :, 1], kv[:, :, 2], gates


def _compress(rows, pe, w1, w2):
    bsz, length = rows.shape[:2]
    n_cmp = (length - CMP_LEN) // CMP_STRIDE + 1
    r = CMP_LEN // CMP_STRIDE
    seg = rows[:, :(n_cmp + r - 1) * CMP_STRIDE].reshape(bsz, n_cmp + r - 1, CMP_STRIDE, N_KV, HEAD_DIM)
    seg = seg.transpose(0, 1, 3, 2, 4).reshape(bsz, n_cmp + r - 1, N_KV, CMP_STRIDE * HEAD_DIM)
    w1r = w1.reshape(r, CMP_STRIDE * HEAD_DIM, CMP_HID)
    hid = pe.reshape(-1) @ w1
    for j in range(r):
        hid = hid + jnp.einsum('bngf,fh->bngh', seg[:, j:j + n_cmp], w1r[j])
    return jax.nn.gelu(hid) @ w2


def _cmp_to_slc(n_cmp, n_slc):
    i = jnp.arange(n_cmp)[:, None] * CMP_STRIDE
    j = jnp.arange(n_slc)[None, :] * SLC_LEN
    ov = jnp.minimum(i + CMP_LEN, j + SLC_LEN) - jnp.maximum(i, j)
    return (jnp.maximum(ov, 0) / CMP_STRIDE).astype(jnp.float32)


def _nsa_long_branches(q, pos, kv_cmp_rows, kv_slc_rows, pe, w1, w2, rel_table):
    bsz, length = kv_cmp_rows.shape[:2]
    k_c = _compress(kv_cmp_rows[:, :, 0], pe[0], w1[0], w2[0])
    v_c = _compress(kv_cmp_rows[:, :, 1], pe[1], w1[1], w2[1])
    n_cmp = k_c.shape[1]
    cmp_end = jnp.arange(n_cmp, dtype=jnp.int32) * CMP_STRIDE + CMP_LEN - 1
    n_slc = -(-length // SLC_LEN)
    n_sel = min(N_SEL, n_slc)
    imp_map = _cmp_to_slc(n_cmp, n_slc)
    sb = jnp.pad(kv_slc_rows, ((0, 0), (0, n_slc * SLC_LEN - length), (0, 0), (0, 0), (0, 0)))
    sb = sb.reshape(bsz, n_slc, SLC_LEN, 2, N_KV, HEAD_DIM).transpose(3, 0, 4, 1, 2, 5)
    k_sb, v_sb = sb[0], sb[1]
    tbl = rel_table.reshape(REL_BUCKETS, N_KV, HPG)
    scale = HEAD_DIM ** -0.5
    blk_ids = jnp.arange(n_slc, dtype=jnp.int32)
    b_ids = jnp.arange(bsz)[:, None, None, None]
    g_ids = jnp.arange(N_KV)[None, :, None, None]

    def block(args):
        qb, pb = args
        nq = qb.shape[0]
        dist = pb[:, None] - cmp_end[None, :]
        bias = tbl[_rel_bucket(dist)].transpose(2, 3, 0, 1)
        s = jnp.einsum('qbghd,bngd->bghqn', qb, k_c).astype(jnp.float32) * scale + bias
        p = _masked_softmax(s, dist >= 0)
        o_c = jnp.einsum('bghqn,bngd->qbghd', p, v_c)
        imp = jnp.einsum('bghqn,nj->bgqj', p, imp_map)
        cur = pb[:, None] // SLC_LEN
        valid = blk_ids[None, :] * SLC_LEN <= pb[:, None]
        forced = (blk_ids[None, :] == 0) | (blk_ids[None, :] == cur) | (blk_ids[None, :] == cur - 1)
        score = jnp.where(valid, imp + jnp.where(forced, FORCE_BONUS, 0.0), -jnp.inf)
        top_s, top_i = lax.top_k(score, n_sel)
        nk = n_sel * SLC_LEN
        k_sel = k_sb[b_ids, g_ids, top_i].reshape(bsz, N_KV, nq, nk, HEAD_DIM)
        v_sel = v_sb[b_ids, g_ids, top_i].reshape(bsz, N_KV, nq, nk, HEAD_DIM)
        key_pos = (top_i[..., None] * SLC_LEN + jnp.arange(SLC_LEN, dtype=jnp.int32)).reshape(bsz, N_KV, nq, nk)
        dist_s = pb[None, None, :, None] - key_pos
        ok = jnp.repeat(jnp.isfinite(top_s), SLC_LEN, axis=-1) & (dist_s >= 0)
        bias_s = jnp.moveaxis(tbl[_rel_bucket(dist_s), g_ids], -1, 2)
        s2 = jnp.einsum('qbghd,bgqkd->bghqk', qb, k_sel).astype(jnp.float32) * scale + bias_s
        p2 = _masked_softmax(s2, ok[:, :, None])
        o_s = jnp.einsum('bghqk,bgqkd->qbghd', p2, v_sel)
        return (o_c.astype(qb.dtype), o_s.astype(qb.dtype))

    return _blocked(block, (q, pos))


def _window_attend(q, pos, k, v, k_pos, rel_table):
    dist = pos[:, None] - k_pos[None, :]
    mask = (dist >= 0) & (dist < WINDOW) & (k_pos[None, :] >= 0)
    bias = rel_table.reshape(REL_BUCKETS, N_KV, HPG)[_rel_bucket(dist)].transpose(2, 3, 0, 1)
    s = jnp.einsum('qbghd,bkgd->bghqk', q, k).astype(jnp.float32) * HEAD_DIM ** -0.5 + bias
    p = _masked_softmax(s, mask)
    return jnp.einsum('bghqk,bkgd->qbghd', p, v).astype(q.dtype)


def _window_prompt(q, pos, kv_rows, rel_table):
    kp = jnp.pad(kv_rows, ((0, 0), (WINDOW, 0), (0, 0), (0, 0), (0, 0)))

    def block(args):
        qb, pb = args
        span = WINDOW + qb.shape[0]
        slab = lax.dynamic_slice_in_dim(kp, pb[0], span, axis=1)
        k_pos = pb[0] - WINDOW + jnp.arange(span, dtype=jnp.int32)
        return (_window_attend(qb, pb, slab[:, :, 0], slab[:, :, 1], k_pos, rel_table),)

    return _blocked(block, (q, pos))[0]


def _nsa_merge(o_c, o_s, o_w, gates, w_out):
    t, bsz = o_c.shape[:2]
    g = gates.reshape(bsz, t, 3, N_KV, HPG, 1)
    o = (g[:, :, 0] * o_c.transpose(1, 0, 2, 3, 4) + g[:, :, 1] * o_s.transpose(1, 0, 2, 3, 4)
         + g[:, :, 2] * o_w.transpose(1, 0, 2, 3, 4))
    return o.reshape(bsz, t, N_HEADS * HEAD_DIM) @ w_out


def _nsa_prompt(h, w_in, pe, w1, w2, w_out, rel_table):
    t = h.shape[1]
    q, kv_c, kv_s, kv_w, gates = _nsa_project(h, w_in)
    pos = jnp.arange(t, dtype=jnp.int32)
    o_c, o_s = _nsa_long_branches(q, pos, kv_c, kv_s, pe, w1, w2, rel_table)
    o_w = _window_prompt(q, pos, kv_w, rel_table)
    return _nsa_merge(o_c, o_s, o_w, gates, w_out), kv_c, kv_s, kv_w[:, -min(WINDOW, t):]


def _nsa_sample(h, past_cmp, past_slc, win_buf, w_in, pe, w1, w2, w_out, rel_table):
    bsz, t = h.shape[:2]
    past = past_cmp.shape[1] * PAGE_SIZE
    q, kv_c, kv_s, kv_w, gates = _nsa_project(h, w_in)
    pos = past + jnp.arange(t, dtype=jnp.int32)
    rows_c = jnp.concatenate([past_cmp.reshape((bsz, past) + past_cmp.shape[3:]), kv_c], axis=1)
    rows_s = jnp.concatenate([past_slc.reshape((bsz, past) + past_slc.shape[3:]), kv_s], axis=1)
    o_c, o_s = _nsa_long_branches(q, pos, rows_c, rows_s, pe, w1, w2, rel_table)
    buf_len = win_buf.shape[1]
    rows_w = jnp.concatenate([win_buf, kv_w], axis=1)
    k_pos = past - buf_len + jnp.arange(buf_len + t, dtype=jnp.int32)
    o_w = _window_attend(q, pos, rows_w[:, :, 0], rows_w[:, :, 1], k_pos, rel_table)
    new_win = rows_w[:, -min(WINDOW, buf_len + t):]
    return _nsa_merge(o_c, o_s, o_w, gates, w_out), kv_c, kv_s, new_win


def setup_inputs(seed: int = 0) -> dict:
    key = jax.random.key(seed)
    ks = iter(jax.random.split(key, 40))

    def nrm(shape, s):
        return jax.random.normal(next(ks), shape, jnp.float32) * s

    n_pages = PAST_LEN // PAGE_SIZE
    n_used = DEC_BATCH * n_pages
    n_phys = n_used + max(1, n_used // 4)
    win_len = min(WINDOW, PAST_LEN)
    sd = D_MODEL ** -0.5
    x_prompt = nrm((BATCH, SEQ, D_MODEL), 1.0)
    x_sample = nrm((DEC_BATCH, DEC_SEQ, D_MODEL), 1.0)
    cache_cmp_kv = nrm((N_NSA_LAYERS, n_phys, PAGE_SIZE, 2, N_KV, HEAD_DIM), 1.0)
    cache_slc_kv = nrm((N_NSA_LAYERS, n_phys, PAGE_SIZE, 2, N_KV, HEAD_DIM), 1.0)
    state_win_kv = nrm((N_NSA_LAYERS, DEC_BATCH, win_len, 2, N_KV, HEAD_DIM), 1.0)
    page_table = jax.random.permutation(next(ks), n_phys)[:n_used].reshape(DEC_BATCH, n_pages).astype(jnp.int32)
    c_prompt = nrm((BATCH, D_MODEL), 1.0)
    c_sample = nrm((DEC_BATCH, D_MODEL), 1.0)
    ada_w = nrm((DEPTH, D_MODEL, N_ADA * D_MODEL), 0.2 * sd)
    ada_b = nrm((DEPTH, N_ADA * D_MODEL), 0.02)
    ln_g = 1.0 + nrm((DEPTH, 3, D_MODEL), 0.02)
    ln_b = nrm((DEPTH, 3, D_MODEL), 0.02)
    ffn_pre_w_in = nrm((DEPTH, D_MODEL, 2 * D_FF), sd)
    ffn_pre_w_out = nrm((DEPTH, D_FF, D_MODEL), DN_BETA * D_FF ** -0.5)
    ffn_post_w_in = nrm((DEPTH, D_MODEL, 2 * D_FF), sd)
    ffn_post_w_out = nrm((DEPTH, D_FF, D_MODEL), DN_BETA * D_FF ** -0.5)
    gmlp_w_in = nrm((N_GMLP_LAYERS, D_MODEL, 2 * GMLP_DIM), sd)
    gmlp_ln_g = 1.0 + nrm((N_GMLP_LAYERS, GMLP_DIM), 0.02)
    gmlp_ln_b = nrm((N_GMLP_LAYERS, GMLP_DIM), 0.02)
    gmlp_w_s = nrm((N_GMLP_LAYERS, GMLP_GROUPS, CHUNK, CHUNK), CHUNK ** -0.5)
    gmlp_b_s = 1.0 + nrm((N_GMLP_LAYERS, GMLP_GROUPS, CHUNK), 0.02)
    gmlp_w_out = nrm((N_GMLP_LAYERS, GMLP_DIM, D_MODEL), DN_BETA * GMLP_DIM ** -0.5)
    w_q = nrm((N_NSA_LAYERS, D_MODEL, N_HEADS * HEAD_DIM), sd)
    w_kv = nrm((N_NSA_LAYERS, D_MODEL, 3, 2, KV_COLS), sd) * jnp.array([1.0, DN_BETA], jnp.float32)[:, None]
    w_g = nrm((N_NSA_LAYERS, D_MODEL, 3 * N_HEADS), sd)
    nsa_w_in = jnp.concatenate([w_q, w_kv.reshape(N_NSA_LAYERS, D_MODEL, 6 * KV_COLS), w_g], axis=-1)
    nsa_cmp_pe = nrm((N_NSA_LAYERS, 2, CMP_LEN, HEAD_DIM), 0.02)
    nsa_cmp_w1 = nrm((N_NSA_LAYERS, 2, CMP_LEN * HEAD_DIM, CMP_HID), (CMP_LEN * HEAD_DIM) ** -0.5)
    nsa_cmp_w2 = nrm((N_NSA_LAYERS, 2, CMP_HID, HEAD_DIM), CMP_HID ** -0.5)
    nsa_w_out = nrm((N_NSA_LAYERS, N_HEADS * HEAD_DIM, D_MODEL), DN_BETA * (N_HEADS * HEAD_DIM) ** -0.5)
    rel_table = nrm((REL_BUCKETS, N_HEADS), 0.5)
    return {"x_prompt": x_prompt, "x_sample": x_sample, "cache_cmp_kv": cache_cmp_kv,
            "cache_slc_kv": cache_slc_kv, "state_win_kv": state_win_kv, "page_table": page_table,
            "c_prompt": c_prompt, "c_sample": c_sample, "ada_w": ada_w, "ada_b": ada_b,
            "ln_g": ln_g, "ln_b": ln_b, "ffn_pre_w_in": ffn_pre_w_in, "ffn_pre_w_out": ffn_pre_w_out,
            "ffn_post_w_in": ffn_post_w_in, "ffn_post_w_out": ffn_post_w_out,
            "gmlp_w_in": gmlp_w_in, "gmlp_ln_g": gmlp_ln_g, "gmlp_ln_b": gmlp_ln_b,
            "gmlp_w_s": gmlp_w_s, "gmlp_b_s": gmlp_b_s, "gmlp_w_out": gmlp_w_out,
            "nsa_w_in": nsa_w_in, "nsa_cmp_pe": nsa_cmp_pe, "nsa_cmp_w1": nsa_cmp_w1,
            "nsa_cmp_w2": nsa_cmp_w2, "nsa_w_out": nsa_w_out, "rel_table": rel_table}


def reference(x_prompt, x_sample, cache_cmp_kv, cache_slc_kv, state_win_kv, page_table, c_prompt, c_sample,
              ada_w, ada_b, ln_g, ln_b, ffn_pre_w_in, ffn_pre_w_out, ffn_post_w_in, ffn_post_w_out,
              gmlp_w_in, gmlp_ln_g, gmlp_ln_b, gmlp_w_s, gmlp_b_s, gmlp_w_out,
              nsa_w_in, nsa_cmp_pe, nsa_cmp_w1, nsa_cmp_w2, nsa_w_out, rel_table):
    xp, xs = x_prompt, x_sample
    cmp_p, cmp_s, slc_p, slc_s, win_p, win_s, gv_s = [], [], [], [], [], [], []
    for i in range(DEPTH):
        mp = _ada(c_prompt, ada_w[i], ada_b[i])
        ms = _ada(c_sample, ada_w[i], ada_b[i])
        xp = _ffn_sublayer(xp, mp, 0, ffn_pre_w_in[i], ffn_pre_w_out[i], ln_g[i, 0], ln_b[i, 0])
        xs = _ffn_sublayer(xs, ms, 0, ffn_pre_w_in[i], ffn_pre_w_out[i], ln_g[i, 0], ln_b[i, 0])
        hp = _modulate(xp, mp[:, 3], mp[:, 4])
        hs = _modulate(xs, ms[:, 3], ms[:, 4])
        if i % N_MIXERS == 0:
            a = i // N_MIXERS
            yp, _ = _gmlp_mixer(hp, gmlp_w_in[a], gmlp_ln_g[a], gmlp_ln_b[a], gmlp_w_s[a], gmlp_b_s[a], gmlp_w_out[a])
            ys, v_new = _gmlp_mixer(hs, gmlp_w_in[a], gmlp_ln_g[a], gmlp_ln_b[a], gmlp_w_s[a], gmlp_b_s[a], gmlp_w_out[a])
            gv_s.append(v_new)
        else:
            b = i // N_MIXERS
            yp, kc_p, ks_p, kw_p = _nsa_prompt(hp, nsa_w_in[b], nsa_cmp_pe[b], nsa_cmp_w1[b], nsa_cmp_w2[b],
                                               nsa_w_out[b], rel_table)
            ys, kc_s, ks_s, kw_s = _nsa_sample(hs, cache_cmp_kv[b, page_table], cache_slc_kv[b, page_table],
                                               state_win_kv[b], nsa_w_in[b], nsa_cmp_pe[b], nsa_cmp_w1[b],
                                               nsa_cmp_w2[b], nsa_w_out[b], rel_table)
            cmp_p.append(kc_p); cmp_s.append(kc_s); slc_p.append(ks_p); slc_s.append(ks_s)
            win_p.append(kw_p); win_s.append(kw_s)
        xp = _residual(xp, yp, mp[:, 5], ln_g[i, 1], ln_b[i, 1])
        xs = _residual(xs, ys, ms[:, 5], ln_g[i, 1], ln_b[i, 1])
        xp = _ffn_sublayer(xp, mp, 6, ffn_post_w_in[i], ffn_post_w_out[i], ln_g[i, 2], ln_b[i, 2])
        xs = _ffn_sublayer(xs, ms, 6, ffn_post_w_in[i], ffn_post_w_out[i], ln_g[i, 2], ln_b[i, 2])
    return (xp, xs, jnp.stack(cmp_p), jnp.stack(cmp_s), jnp.stack(slc_p), jnp.stack(slc_s),
            jnp.stack(win_p), jnp.stack(win_s), jnp.stack(gv_s))
```

```python
import functools
import math

import numpy as np
import jax
import jax.numpy as jnp
from jax import lax
from jax.experimental import pallas as pl
from jax.experimental.pallas import tpu as pltpu

F32 = jnp.float32
BF16 = jnp.bfloat16

DEPTH = 4
N_ADA = 9
N_HEADS = 16
HEAD_DIM = 128
N_KV = 4
HPG = N_HEADS // N_KV
GMLP_GROUPS = 16
CHUNK = 128
PAGE_SIZE = 128
CMP_LEN = 32
CMP_STRIDE = 16
SLC_LEN = 64
N_SEL = 16
WINDOW = 512
Q_BLOCK = 128
REL_BUCKETS = 32
REL_MAX_DIST = 128
LN_EPS = 1e-5
DN_ALPHA = (2 * DEPTH) ** 0.25
FORCE_BONUS = 1e4
ATT_SCALE = HEAD_DIM ** -0.5

SAMPLE_ROWS = 16
NEG_MASK = -30000.0
NEG_BIAS = -1e30
VMEM_LIMIT = 56 * 1024 * 1024


def _cparams(sem):
    return pltpu.CompilerParams(dimension_semantics=sem, vmem_limit_bytes=VMEM_LIMIT)


def _dot(a, b):
    return jnp.dot(a, b, preferred_element_type=F32)


def _dot_nt(a, b):
    return lax.dot_general(a, b, (((1,), (1,)), ((), ())), preferred_element_type=F32)


def _ln_rows(z, g, b):
    mu = jnp.mean(z, axis=-1, keepdims=True)
    zc = z - mu
    var = jnp.mean(zc * zc, axis=-1, keepdims=True)
    return zc * lax.rsqrt(var + LN_EPS) * g + b


def _silu(x):
    return x * jax.nn.sigmoid(x)


def _ada_kernel(c_ref, w_ref, b_ref, o_ref):
    h = _silu(c_ref[...]).astype(BF16)
    o_ref[0] = _dot(h, w_ref[0].astype(BF16)) + b_ref[0]


def _ada(c_all, ada_w, ada_b, tn=1024):
    depth, d, n = ada_w.shape
    r = c_all.shape[0]
    return pl.pallas_call(
        _ada_kernel,
        out_shape=jax.ShapeDtypeStruct((depth, r, n), F32),
        grid=(depth, n // tn),
        in_specs=[pl.BlockSpec((r, d), lambda l, j: (0, 0)),
                  pl.BlockSpec((1, d, tn), lambda l, j: (l, 0, j)),
                  pl.BlockSpec((1, 1, tn), lambda l, j: (l, 0, j))],
        out_specs=pl.BlockSpec((1, r, tn), lambda l, j: (l, 0, j)),
        compiler_params=_cparams(("parallel", "parallel")),
        name="ada",
    )(c_all, ada_w, ada_b.reshape(depth, 1, n))


def _ffn_kernel(x_ref, sh_ref, sc_ref, gt_ref, wg_ref, wu_ref, wo_ref, lg_ref, lb_ref,
                o_ref, h_ref, acc_ref):
    j = pl.program_id(1)

    @pl.when(j == 0)
    def _():
        h_ref[...] = (x_ref[...] * (1.0 + sc_ref[0]) + sh_ref[0]).astype(BF16)
        acc_ref[...] = jnp.zeros_like(acc_ref)

    h = h_ref[...]
    g = _dot(h, wg_ref[...])
    u = _dot(h, wu_ref[...])
    a = (_silu(g) * u).astype(BF16)
    acc_ref[...] += _dot(a, wo_ref[...])

    @pl.when(j == pl.num_programs(1) - 1)
    def _():
        z = DN_ALPHA * x_ref[...] + (1.0 + gt_ref[0]) * (0.5 * acc_ref[...])
        o_ref[...] = _ln_rows(z, lg_ref[...], lb_ref[...])


def _ffn(x, shift, scale, gate, w_in, w_out, ln_g, ln_b, *, rows_per_batch, tm, tf=512):
    m, d = x.shape
    dff = w_out.shape[0]
    nf = dff // tf
    r = shift.shape[1]
    tpb = rows_per_batch // tm
    mod_spec = pl.BlockSpec((1, r, d), lambda i, j: (i // tpb, 0, 0))
    vec_spec = pl.BlockSpec((1, d), lambda i, j: (0, 0))
    return pl.pallas_call(
        _ffn_kernel,
        out_shape=jax.ShapeDtypeStruct((m, d), F32),
        grid=(m // tm, nf),
        in_specs=[pl.BlockSpec((tm, d), lambda i, j: (i, 0)),
                  mod_spec, mod_spec, mod_spec,
                  pl.BlockSpec((d, tf), lambda i, j: (0, j)),
                  pl.BlockSpec((d, tf), lambda i, j: (0, j + nf)),
                  pl.BlockSpec((tf, d), lambda i, j: (j, 0)),
                  vec_spec, vec_spec],
        out_specs=pl.BlockSpec((tm, d), lambda i, j: (i, 0)),
        scratch_shapes=[pltpu.VMEM((tm, d), BF16), pltpu.VMEM((tm, d), F32)],
        compiler_params=_cparams(("parallel", "arbitrary")),
        name="ffn",
    )(x, shift, scale, gate, w_in, w_in, w_out, ln_g.reshape(1, d), ln_b.reshape(1, d))


def _mm_kernel(x_ref, sh_ref, sc_ref, w_ref, o_ref, h_ref, *, act):
    @pl.when(pl.program_id(1) == 0)
    def _():
        h_ref[...] = (x_ref[...] * (1.0 + sc_ref[0]) + sh_ref[0]).astype(BF16)

    y = _dot(h_ref[...], w_ref[...])
    if act == "gelu":
        y = jax.nn.gelu(y)
    o_ref[...] = y.astype(o_ref.dtype)


def _mod_mm(x, shift, scale, w, *, rows_per_batch, tm, tn, act=None):
    m, d = x.shape
    n = w.shape[1]
    r = shift.shape[1]
    tpb = rows_per_batch // tm
    mod_spec = pl.BlockSpec((1, r, d), lambda i, j: (i // tpb, 0, 0))
    return pl.pallas_call(
        functools.partial(_mm_kernel, act=act),
        out_shape=jax.ShapeDtypeStruct((m, n), F32),
        grid=(m // tm, n // tn),
        in_specs=[pl.BlockSpec((tm, d), lambda i, j: (i, 0)),
                  mod_spec, mod_spec,
                  pl.BlockSpec((d, tn), lambda i, j: (0, j))],
        out_specs=pl.BlockSpec((tm, tn), lambda i, j: (i, j)),
        scratch_shapes=[pltpu.VMEM((tm, d), BF16)],
        compiler_params=_cparams(("parallel", "arbitrary")),
        name="mod_mm",
    )(x, shift, scale, w)


GMLP_GROUPS_PER_STEP = 2


def _gmlp_gate_kernel(x_ref, gt_ref, u_ref, vfull_ref, vg_ref, vlg_ref, vlb_ref, ws_ref, bs_ref,
                      wo_ref, lg_ref, lb_ref, o_ref, vn_ref, acc_ref, mu_ref, rs_ref,
                      *, chunk, gw):
    k = pl.program_id(1)
    tm = x_ref.shape[0]

    @pl.when(k == 0)
    def _():
        v = vfull_ref[...]
        mu = jnp.mean(v, axis=-1, keepdims=True)
        vc = v - mu
        var = jnp.mean(vc * vc, axis=-1, keepdims=True)
        mu_ref[...] = mu
        rs_ref[...] = lax.rsqrt(var + LN_EPS)
        acc_ref[...] = jnp.zeros_like(acc_ref)

    vn = (vg_ref[...] - mu_ref[...]) * rs_ref[...] * vlg_ref[...] + vlb_ref[...]
    vn_ref[...] = vn
    parts = []
    for gi in range(GMLP_GROUPS_PER_STEP):
        vn_g = vn[:, gi * gw:(gi + 1) * gw]
        if chunk == 1:
            s_g = ws_ref[gi, 0:1, 0:1] * vn_g + bs_ref[gi, 0:1, 0:1]
        else:
            row = lax.broadcasted_iota(jnp.int32, (chunk, chunk), 0)
            col = lax.broadcasted_iota(jnp.int32, (chunk, chunk), 1)
            w_tri = jnp.where(col <= row, ws_ref[gi], 0.0).astype(BF16)
            vb = vn_g.astype(BF16)
            s_g = jnp.concatenate(
                [_dot(w_tri, vb[c * chunk:(c + 1) * chunk]) + bs_ref[gi]
                 for c in range(tm // chunk)], axis=0)
        parts.append(s_g)
    s = jnp.concatenate(parts, axis=1)
    a = (u_ref[...] * s).astype(BF16)
    acc_ref[...] += _dot(a, wo_ref[...])

    @pl.when(k == pl.num_programs(1) - 1)
    def _():
        z = DN_ALPHA * x_ref[...] + (1.0 + gt_ref[0]) * acc_ref[...]
        o_ref[...] = _ln_rows(z, lg_ref[...], lb_ref[...])


def _gmlp_gate(x, gate, uv, v_ln_g, v_ln_b, w_s, b_s, w_out, ln_g, ln_b, *, rows_per_batch, tm, chunk):
    m, d = x.shape
    gdim = uv.shape[1] // 2
    gw = gdim // GMLP_GROUPS
    gw2 = gw * GMLP_GROUPS_PER_STEP
    ns = GMLP_GROUPS // GMLP_GROUPS_PER_STEP
    r = gate.shape[1]
    tpb = rows_per_batch // tm
    vec_spec = pl.BlockSpec((1, d), lambda i, k: (0, 0))
    out, vn = pl.pallas_call(
        functools.partial(_gmlp_gate_kernel, chunk=chunk, gw=gw),
        out_shape=(jax.ShapeDtypeStruct((m, d), F32), jax.ShapeDtypeStruct((m, gdim), F32)),
        grid=(m // tm, ns),
        in_specs=[pl.BlockSpec((tm, d), lambda i, k: (i, 0)),
                  pl.BlockSpec((1, r, d), lambda i, k: (i // tpb, 0, 0)),
                  pl.BlockSpec((tm, gw2), lambda i, k: (i, k)),
                  pl.BlockSpec((tm, gdim), lambda i, k: (i, 1)),
                  pl.BlockSpec((tm, gw2), lambda i, k: (i, k + ns)),
                  pl.BlockSpec((1, gw2), lambda i, k: (0, k)),
                  pl.BlockSpec((1, gw2), lambda i, k: (0, k)),
                  pl.BlockSpec((GMLP_GROUPS_PER_STEP, CHUNK, CHUNK), lambda i, k: (k, 0, 0)),
                  pl.BlockSpec((GMLP_GROUPS_PER_STEP, CHUNK, 1), lambda i, k: (k, 0, 0)),
                  pl.BlockSpec((gw2, d), lambda i, k: (k, 0)),
                  vec_spec, vec_spec],
        out_specs=(pl.BlockSpec((tm, d), lambda i, k: (i, 0)),
                   pl.BlockSpec((tm, gw2), lambda i, k: (i, k))),
        scratch_shapes=[pltpu.VMEM((tm, d), F32), pltpu.VMEM((tm, 1), F32), pltpu.VMEM((tm, 1), F32)],
        compiler_params=_cparams(("parallel", "arbitrary")),
        name="gmlp_gate",
    )(x, gate, uv, uv, uv, v_ln_g.reshape(1, gdim), v_ln_b.reshape(1, gdim), w_s,
      b_s.reshape(GMLP_GROUPS, CHUNK, 1), w_out, ln_g.reshape(1, d), ln_b.reshape(1, d))
    return out, vn


def _compress_kernel(r_ref, pe_ref, w1_ref, w2_ref, o_ref, *, nseg):
    half = CMP_STRIDE * HEAD_DIM
    a = jnp.concatenate(
        [r_ref[pl.ds(r, nseg, stride=CMP_STRIDE), :].astype(BF16) for r in range(CMP_STRIDE)], axis=1)
    p0 = _dot(a, w1_ref[0, 0:half, :])
    p1 = _dot(a, w1_ref[0, half:2 * half, :])
    pe = jnp.broadcast_to(pe_ref[0], (8, 2 * half)).astype(BF16)
    peh = _dot(pe, w1_ref[0])[0:1]
    hid = peh + p0 + pltpu.roll(p1, nseg - 1, 0)
    o_ref[0, 0, 0] = _dot(jax.nn.gelu(hid).astype(BF16), w2_ref[0])


def _compress(rows2d, n_batch, t, col0, pe, w1, w2):
    nseg = t // CMP_STRIDE
    return pl.pallas_call(
        functools.partial(_compress_kernel, nseg=nseg),
        out_shape=jax.ShapeDtypeStruct((n_batch, 2, N_KV, nseg, HEAD_DIM), F32),
        grid=(n_batch, 2, N_KV),
        in_specs=[pl.BlockSpec((t, HEAD_DIM), lambda b, kv, g: (b, col0 + kv * N_KV + g)),
                  pl.BlockSpec((1, 1, CMP_LEN * HEAD_DIM), lambda b, kv, g: (kv, 0, 0)),
                  pl.BlockSpec((1, CMP_LEN * HEAD_DIM, w1.shape[2]), lambda b, kv, g: (kv, 0, 0)),
                  pl.BlockSpec((1, w2.shape[1], HEAD_DIM), lambda b, kv, g: (kv, 0, 0))],
        out_specs=pl.BlockSpec((1, 1, 1, nseg, HEAD_DIM), lambda b, kv, g: (b, kv, g, 0, 0)),
        compiler_params=_cparams(("parallel", "parallel", "parallel")),
        name="compress",
    )(rows2d, pe.reshape(2, 1, CMP_LEN * HEAD_DIM), w1, w2)


def _cmp_select_kernel(q_ref, kc_ref, vc_ref, bias_ref, imap_ref, oc_ref, sel_ref, *, nslc, nsel):
    qb = pl.program_id(2)
    tq = q_ref.shape[0]
    kc = kc_ref[0, 0, 0].astype(BF16)
    vc = vc_ref[0, 0, 0].astype(BF16)
    psum = None
    outs = []
    for h in range(HPG):
        qh = (q_ref[:, h * HEAD_DIM:(h + 1) * HEAD_DIM] * ATT_SCALE).astype(BF16)
        bias = bias_ref[h]
        valid = bias > 0.5 * NEG_BIAS
        s = _dot_nt(qh, kc) + bias
        m = jnp.max(s, axis=-1, keepdims=True)
        e = jnp.where(valid, jnp.exp(s - m), 0.0)
        den = jnp.sum(e, axis=-1, keepdims=True)
        p = e / jnp.where(den > 0.0, den, 1.0)
        outs.append(_dot(p.astype(BF16), vc))
        psum = p if psum is None else psum + p
    oc_ref[...] = jnp.concatenate(outs, axis=1)

    p_hi = psum.astype(BF16)
    p_lo = (psum - p_hi.astype(F32)).astype(BF16)
    imap = imap_ref[...]
    imp = _dot_nt(imap, p_hi) + _dot_nt(imap, p_lo)
    jidx = lax.broadcasted_iota(jnp.int32, (nslc, tq), 0)
    qpos = qb * tq + lax.broadcasted_iota(jnp.int32, (nslc, tq), 1)
    cur = qpos // SLC_LEN
    valid_blk = jidx * SLC_LEN <= qpos
    forced = (jidx == 0) | (jidx == cur) | (jidx == cur - 1)
    score = jnp.where(valid_blk, imp + jnp.where(forced, FORCE_BONUS, 0.0), -jnp.inf)
    cnt = jnp.zeros((nslc, tq), jnp.int32)
    for jp in range(nslc):
        row = score[jp:jp + 1, :]
        ahead = (row > score) | ((row == score) & (jidx > jp))
        cnt = cnt + jnp.where(ahead, 1, 0)
    sel = jnp.where((cnt < nsel) & valid_blk, 1.0, 0.0)
    sel_ref[0, 0] = sel.T


def _cmp_select(proj, kvc, bias_cmp, imap_t, n_batch, t):
    nqb = t // Q_BLOCK
    nseg = t // CMP_STRIDE
    nslc = t // SLC_LEN
    nsel = min(N_SEL, nslc)
    d_q = N_HEADS * HEAD_DIM
    gcols = HPG * HEAD_DIM
    return pl.pallas_call(
        functools.partial(_cmp_select_kernel, nslc=nslc, nsel=nsel),
        out_shape=(jax.ShapeDtypeStruct((n_batch * t, d_q), F32),
                   jax.ShapeDtypeStruct((n_batch, N_KV, t, nslc), F32)),
        grid=(n_batch, N_KV, nqb),
        in_specs=[pl.BlockSpec((Q_BLOCK, gcols), lambda b, g, i: (b * nqb + i, g)),
                  pl.BlockSpec((1, 1, 1, nseg, HEAD_DIM), lambda b, g, i: (b, 0, g, 0, 0)),
                  pl.BlockSpec((1, 1, 1, nseg, HEAD_DIM), lambda b, g, i: (b, 1, g, 0, 0)),
                  pl.BlockSpec((HPG, Q_BLOCK, nseg), lambda b, g, i: (g, i, 0)),
                  pl.BlockSpec((nslc, nseg), lambda b, g, i: (0, 0))],
        out_specs=(pl.BlockSpec((Q_BLOCK, gcols), lambda b, g, i: (b * nqb + i, g)),
                   pl.BlockSpec((1, 1, Q_BLOCK, nslc), lambda b, g, i: (b, g, i, 0))),
        compiler_params=_cparams(("parallel", "parallel", "parallel")),
        name="cmp_select",
    )(proj, kvc, kvc, bias_cmp, imap_t)


def _softmax_update(state, s, v):
    m, l, acc = state
    m_new = jnp.maximum(m, jnp.max(s, axis=-1, keepdims=True))
    alpha = jnp.exp(m - m_new)
    p = jnp.exp(s - m_new)
    l = alpha * l + jnp.sum(p, axis=-1, keepdims=True)
    acc = alpha * acc + _dot(p.astype(BF16), v)
    return m_new, l, acc


def _sel_win_kernel(q_ref, ks_ref, vs_ref, kw_ref, vw_ref, sel_ref, bias_ref, oc_ref, gate_ref,
                    o_ref, ksb, vsb, kwb, vwb, *, nslc):
    qb = pl.program_id(2)
    tq = Q_BLOCK
    rows = HPG * tq

    @pl.when(qb == 0)
    def _():
        ksb[...] = ks_ref[...].astype(BF16)
        vsb[...] = vs_ref[...].astype(BF16)
        kwb[...] = kw_ref[...].astype(BF16)
        vwb[...] = vw_ref[...].astype(BF16)

    q_all = jnp.concatenate(
        [(q_ref[:, h * HEAD_DIM:(h + 1) * HEAD_DIM] * ATT_SCALE).astype(BF16) for h in range(HPG)], axis=0)
    sel = sel_ref[0, 0].astype(BF16)
    blk = lax.broadcasted_iota(jnp.int32, (nslc, tq), 0)
    key_blk = lax.broadcasted_iota(jnp.int32, (nslc, tq), 1) // SLC_LEN
    bias_prev = bias_ref[:, :, 0:tq].reshape(rows, tq)
    bias_diag = bias_ref[:, :, tq:2 * tq].reshape(rows, tq)

    def tile(ref, kt):
        return ref[pl.ds(pl.multiple_of(kt * tq, tq), tq), :]

    def sel_mask(kt):
        expand = jnp.where(blk == kt * (tq // SLC_LEN) + key_blk, 1.0, 0.0).astype(BF16)
        chosen = _dot(sel, expand)
        add = (chosen - 1.0) * (-NEG_MASK)
        return jnp.concatenate([add] * HPG, axis=0)

    init = (jnp.full((rows, 1), NEG_MASK, F32), jnp.zeros((rows, 1), F32), jnp.zeros((rows, HEAD_DIM), F32))

    def far_body(kt, state):
        s = _dot_nt(q_all, tile(ksb, kt)) + sel_mask(kt)
        return _softmax_update(state, s, tile(vsb, kt))

    st = lax.fori_loop(0, jnp.maximum(qb - 1, 0), far_body, init)
    kp = jnp.maximum(qb - 1, 0)
    prev_ok = jnp.where(qb >= 1, 0.0, NEG_MASK)
    s = _dot_nt(q_all, tile(ksb, kp)) + sel_mask(kp) + bias_prev + prev_ok
    st = _softmax_update(st, s, tile(vsb, kp))
    s = _dot_nt(q_all, tile(ksb, qb)) + sel_mask(qb) + bias_diag
    m_s, l_s, acc_s = _softmax_update(st, s, tile(vsb, qb))
    o_s = acc_s / l_s

    ql = lax.broadcasted_iota(jnp.int32, (tq, tq), 0)
    kl = lax.broadcasted_iota(jnp.int32, (tq, tq), 1)
    edge = jnp.concatenate([jnp.where(kl > ql, 0.0, NEG_MASK)] * HPG, axis=0)
    st = init
    n_win = WINDOW // tq
    for step in range(n_win + 1):
        kt = qb - n_win + step
        ktc = jnp.maximum(kt, 0)
        s = _dot_nt(q_all, tile(kwb, ktc))
        if step == 0:
            s = s + edge
        if step == n_win - 1:
            s = s + bias_prev
        if step == n_win:
            s = s + bias_diag
        else:
            s = s + jnp.where(kt >= 0, 0.0, NEG_MASK)
        st = _softmax_update(st, s, tile(vwb, ktc))
    m_w, l_w, acc_w = st
    o_w = acc_w / l_w

    gates = jax.nn.sigmoid(gate_ref[...])
    outs = []
    for h in range(HPG):
        o_c = oc_ref[:, h * HEAD_DIM:(h + 1) * HEAD_DIM]
        g_c = gates[:, h:h + 1]
        g_s = gates[:, HPG + h:HPG + h + 1]
        g_w = gates[:, 2 * HPG + h:2 * HPG + h + 1]
        outs.append(g_c * o_c + g_s * o_s[h * tq:(h + 1) * tq] + g_w * o_w[h * tq:(h + 1) * tq])
    o_ref[...] = jnp.concatenate(outs, axis=1).astype(o_ref.dtype)


def _sel_win(proj, sel, bias_near, o_c, n_batch, t):
    nqb = t // Q_BLOCK
    nslc = t // SLC_LEN
    d_q = N_HEADS * HEAD_DIM
    gcols = HPG * HEAD_DIM
    kv0 = d_q // HEAD_DIM
    gate0 = kv0 + 6 * N_KV

    def kv_spec(branch, kv):
        return pl.BlockSpec((t, HEAD_DIM), lambda b, g, i: (b, kv0 + (branch * 2 + kv) * N_KV + g))

    return pl.pallas_call(
        functools.partial(_sel_win_kernel, nslc=nslc),
        out_shape=jax.ShapeDtypeStruct((n_batch * t, d_q), BF16),
        grid=(n_batch, N_KV, nqb),
        in_specs=[pl.BlockSpec((Q_BLOCK, gcols), lambda b, g, i: (b * nqb + i, g)),
                  kv_spec(1, 0), kv_spec(1, 1), kv_spec(2, 0), kv_spec(2, 1),
                  pl.BlockSpec((1, 1, Q_BLOCK, nslc), lambda b, g, i: (b, g, i, 0)),
                  pl.BlockSpec((HPG, Q_BLOCK, 2 * Q_BLOCK), lambda b, g, i: (g, 0, 0)),
                  pl.BlockSpec((Q_BLOCK, gcols), lambda b, g, i: (b * nqb + i, g)),
                  pl.BlockSpec((Q_BLOCK, HEAD_DIM), lambda b, g, i: (b * nqb + i, gate0 + g))],
        out_specs=pl.BlockSpec((Q_BLOCK, gcols), lambda b, g, i: (b * nqb + i, g)),
        scratch_shapes=[pltpu.VMEM((t, HEAD_DIM), BF16)] * 4,
        compiler_params=_cparams(("parallel", "parallel", "arbitrary")),
        name="sel_win",
    )(proj, proj, proj, proj, proj, sel, bias_near, o_c, proj)


def _proj_res_kernel(a_ref, w_ref, x_ref, gt_ref, lg_ref, lb_ref, o_ref):
    y = _dot(a_ref[...], w_ref[...])
    z = DN_ALPHA * x_ref[...] + (1.0 + gt_ref[0]) * y
    o_ref[...] = _ln_rows(z, lg_ref[...], lb_ref[...])


def _proj_res(a, w, x, gate, ln_g, ln_b, *, rows_per_batch, tm):
    m, d = x.shape
    k = a.shape[1]
    r = gate.shape[1]
    tpb = rows_per_batch // tm
    vec_spec = pl.BlockSpec((1, d), lambda i: (0, 0))
    return pl.pallas_call(
        _proj_res_kernel,
        out_shape=jax.ShapeDtypeStruct((m, d), F32),
        grid=(m // tm,),
        in_specs=[pl.BlockSpec((tm, k), lambda i: (i, 0)),
                  pl.BlockSpec((k, d), lambda i: (0, 0)),
                  pl.BlockSpec((tm, d), lambda i: (i, 0)),
                  pl.BlockSpec((1, r, d), lambda i: (i // tpb, 0, 0)),
                  vec_spec, vec_spec],
        out_specs=pl.BlockSpec((tm, d), lambda i: (i, 0)),
        compiler_params=_cparams(("parallel",)),
        name="proj_res",
    )(a, w, x, gate, ln_g.reshape(1, d), ln_b.reshape(1, d))


def _rel_bucket_np(dist):
    dist = np.maximum(dist, 0)
    exact = REL_BUCKETS // 2
    logv = (np.log(np.maximum(dist, 1).astype(np.float32) / np.float32(exact))
            / np.float32(math.log(REL_MAX_DIST / exact))).astype(np.float32)
    large = exact + (logv * np.float32(REL_BUCKETS - exact)).astype(np.int32)
    return np.where(dist < exact, dist, np.minimum(large, REL_BUCKETS - 1)).astype(np.int32)


def _cmp_to_slc_np(nseg, nslc):
    i = np.arange(nseg)[:, None] * CMP_STRIDE
    j = np.arange(nslc)[None, :] * SLC_LEN
    ov = np.minimum(i + CMP_LEN, j + SLC_LEN) - np.maximum(i, j)
    return (np.maximum(ov, 0) / CMP_STRIDE).astype(np.float32)


def _bias_tables(rel_table, t):
    tbl_t = rel_table.T
    nseg = t // CMP_STRIDE
    dist_c = np.arange(t)[:, None] - (np.arange(nseg)[None, :] * CMP_STRIDE + CMP_LEN - 1)
    bias_cmp = jnp.where(dist_c >= 0, jnp.take(tbl_t, _rel_bucket_np(dist_c), axis=1), NEG_BIAS)
    dist_n = np.arange(Q_BLOCK)[:, None] + Q_BLOCK - np.arange(2 * Q_BLOCK)[None, :]
    near = jnp.take(tbl_t, _rel_bucket_np(dist_n), axis=1) - tbl_t[:, REL_BUCKETS - 1][:, None, None]
    bias_near = jnp.where(dist_n >= 0, near, NEG_MASK)
    return bias_cmp, bias_near


def _masked_softmax(s, mask):
    s = jnp.where(mask, s, -jnp.inf)
    m = jnp.max(s, axis=-1, keepdims=True)
    m = jnp.where(jnp.isfinite(m), m, 0.0)
    e = jnp.exp(s - m)
    d = jnp.sum(e, axis=-1, keepdims=True)
    return e / jnp.where(d > 0, d, 1.0)


def _sample_attention(q, kv_new, gates, kvc, cache_slc, win_buf, page_table, rel_table):
    n_seq, n_pages = page_table.shape
    past = n_pages * PAGE_SIZE
    pos = past
    nseg = kvc.shape[3]
    tbl = rel_table.reshape(REL_BUCKETS, N_KV, HPG)
    dist = pos - (np.arange(nseg) * CMP_STRIDE + CMP_LEN - 1)
    bias = tbl[_rel_bucket_np(dist)].transpose(1, 2, 0)
    s = jnp.einsum('sghd,sgnd->sghn', q, kvc[:, 0]) * ATT_SCALE + bias
    p = _masked_softmax(s, jnp.asarray(dist >= 0))
    o_c = jnp.einsum('sghn,sgnd->sghd', p, kvc[:, 1])
    nslc = -(-(past + 1) // SLC_LEN)
    nsel = min(N_SEL, nslc)
    imp = jnp.einsum('sghn,nj->sgj', p, jnp.asarray(_cmp_to_slc_np(nseg, nslc)))
    blk = np.arange(nslc)
    cur = pos // SLC_LEN
    forced = (blk == 0) | (blk == cur) | (blk == cur - 1)
    score = jnp.where(jnp.asarray(blk * SLC_LEN <= pos), imp + jnp.where(jnp.asarray(forced), FORCE_BONUS, 0.0),
                      -jnp.inf)
    top_s, top_i = lax.top_k(score, nsel)
    per_page = PAGE_SIZE // SLC_LEN
    past_i = jnp.minimum(top_i, past // SLC_LEN - 1)
    page = page_table[jnp.arange(n_seq)[:, None, None], past_i // per_page]
    row = (past_i % per_page)[..., None] * SLC_LEN + jnp.arange(SLC_LEN)
    gid = jnp.arange(N_KV)[None, :, None, None]
    blocks = cache_slc[page[..., None], row, :, gid, :]
    new_blk = jnp.where((jnp.arange(SLC_LEN) == 0)[None, None, :, None, None],
                        kv_new[:, 1].transpose(0, 2, 1, 3)[:, :, None], 0.0)
    is_new = (top_i >= past // SLC_LEN)[..., None, None, None]
    blocks = jnp.where(is_new, new_blk[:, :, None], blocks)
    nk = nsel * SLC_LEN
    k_sel = blocks[..., 0, :].reshape(n_seq, N_KV, nk, HEAD_DIM)
    v_sel = blocks[..., 1, :].reshape(n_seq, N_KV, nk, HEAD_DIM)
    key_pos = (top_i[..., None] * SLC_LEN + jnp.arange(SLC_LEN)).reshape(n_seq, N_KV, nk)
    dist_s = pos - key_pos
    ok = jnp.repeat(jnp.isfinite(top_s), SLC_LEN, axis=-1) & (dist_s >= 0)
    d_cl = jnp.maximum(dist_s, 0)
    logv = jnp.log(jnp.maximum(d_cl, 1).astype(F32) / (REL_BUCKETS // 2)) / math.log(REL_MAX_DIST / (REL_BUCKETS // 2))
    large = REL_BUCKETS // 2 + (logv * (REL_BUCKETS - REL_BUCKETS // 2)).astype(jnp.int32)
    bucket = jnp.where(d_cl < REL_BUCKETS // 2, d_cl, jnp.minimum(large, REL_BUCKETS - 1))
    bias_s = jnp.moveaxis(tbl[bucket, jnp.arange(N_KV)[None, :, None]], -1, 2)
    s2 = jnp.einsum('sghd,sgkd->sghk', q, k_sel) * ATT_SCALE + bias_s
    p2 = _masked_softmax(s2, ok[:, :, None])
    o_s = jnp.einsum('sghk,sgkd->sghd', p2, v_sel)
    buf_len = win_buf.shape[1]
    rows_w = jnp.concatenate([win_buf, kv_new[:, 2][:, None]], axis=1)
    dist_w = pos - (past - buf_len + np.arange(buf_len + 1))
    mask_w = jnp.asarray((dist_w >= 0) & (dist_w < WINDOW))
    bias_w = tbl[_rel_bucket_np(dist_w)].transpose(1, 2, 0)
    s3 = jnp.einsum('sghd,skgd->sghk', q, rows_w[:, :, 0]) * ATT_SCALE + bias_w
    p3 = _masked_softmax(s3, mask_w)
    o_w = jnp.einsum('sghk,skgd->sghd', p3, rows_w[:, :, 1])
    new_win = rows_w[:, -min(WINDOW, buf_len + 1):]
    o = gates[:, :, 0, :, None] * o_c + gates[:, :, 1, :, None] * o_s + gates[:, :, 2, :, None] * o_w
    return o.reshape(n_seq, N_HEADS * HEAD_DIM), new_win


def _pad_rows(a, rows):
    return jnp.pad(a, ((0, rows - a.shape[0]),) + ((0, 0),) * (a.ndim - 1))


def _nsa_w_in_layout(w):
    d = w.shape[0]
    n_main = N_HEADS * HEAD_DIM + 6 * N_KV * HEAD_DIM
    wg = w[:, n_main:].reshape(d, 3, N_KV, HPG).transpose(0, 2, 1, 3).reshape(d, N_KV, 3 * HPG)
    wg = jnp.pad(wg, ((0, 0), (0, 0), (0, HEAD_DIM - 3 * HPG))).reshape(d, N_KV * HEAD_DIM)
    return jnp.concatenate([w[:, :n_main], wg], axis=1).astype(BF16)


def kernel(x_prompt, x_sample, cache_cmp_kv, cache_slc_kv, state_win_kv, page_table, c_prompt, c_sample,
           ada_w, ada_b, ln_g, ln_b, ffn_pre_w_in, ffn_pre_w_out, ffn_post_w_in, ffn_post_w_out,
           gmlp_w_in, gmlp_ln_g, gmlp_ln_b, gmlp_w_s, gmlp_b_s, gmlp_w_out,
           nsa_w_in, nsa_cmp_pe, nsa_cmp_w1, nsa_cmp_w2, nsa_w_out, rel_table):
    n_b, t, d = x_prompt.shape
    n_s = x_sample.shape[0]
    n_pages = page_table.shape[1]
    past = n_pages * PAGE_SIZE
    d_q = N_HEADS * HEAD_DIM
    kvw = 2 * N_KV * HEAD_DIM
    tm_p = min(512, t)
    tm_mm = min(1024, t)
    tm_g = min(256, t)

    xp = x_prompt.reshape(n_b * t, d)
    xs = _pad_rows(x_sample.reshape(n_s, d), SAMPLE_ROWS)
    c_all = _pad_rows(jnp.concatenate([c_prompt, c_sample], axis=0), SAMPLE_ROWS)
    mod = _ada(c_all, ada_w, ada_b)
    bias_cmp, bias_near = _bias_tables(rel_table, t)
    imap_t = jnp.asarray(_cmp_to_slc_np(t // CMP_STRIDE, t // SLC_LEN).T).astype(BF16)

    cmp_p, cmp_s, slc_p, slc_s, win_p, win_s, gv_s = [], [], [], [], [], [], []
    for i in range(DEPTH):
        mi = mod[i].reshape(SAMPLE_ROWS, N_ADA, d)
        mp = [mi[:n_b, k][:, None, :] for k in range(N_ADA)]
        ms = [_pad_rows(mi[n_b:n_b + n_s, k], SAMPLE_ROWS)[None] for k in range(N_ADA)]

        w_in = ffn_pre_w_in[i].astype(BF16)
        w_out = ffn_pre_w_out[i].astype(BF16)
        xp = _ffn(xp, mp[0], mp[1], mp[2], w_in, w_out, ln_g[i, 0], ln_b[i, 0], rows_per_batch=t, tm=tm_p)
        xs = _ffn(xs, ms[0], ms[1], ms[2], w_in, w_out, ln_g[i, 0], ln_b[i, 0],
                  rows_per_batch=SAMPLE_ROWS, tm=SAMPLE_ROWS)

        if i % 2 == 0:
            a = i // 2
            gw_in = gmlp_w_in[a].astype(BF16)
            gw_out = gmlp_w_out[a].astype(BF16)
            uv_p = _mod_mm(xp, mp[3], mp[4], gw_in, rows_per_batch=t, tm=tm_mm, tn=1024, act="gelu")
            xp, _ = _gmlp_gate(xp, mp[5], uv_p, gmlp_ln_g[a], gmlp_ln_b[a], gmlp_w_s[a], gmlp_b_s[a], gw_out,
                               ln_g[i, 1], ln_b[i, 1], rows_per_batch=t, tm=tm_g, chunk=min(CHUNK, t))
            uv_s = _mod_mm(xs, ms[3], ms[4], gw_in, rows_per_batch=SAMPLE_ROWS, tm=SAMPLE_ROWS, tn=1024,
                           act="gelu")
            xs, vn_s = _gmlp_gate(xs, ms[5], uv_s, gmlp_ln_g[a], gmlp_ln_b[a], gmlp_w_s[a], gmlp_b_s[a], gw_out,
                                  ln_g[i, 1], ln_b[i, 1], rows_per_batch=SAMPLE_ROWS, tm=SAMPLE_ROWS, chunk=1)
            gv_s.append(vn_s[:n_s].reshape(n_s, 1, -1))
        else:
            a = i // 2
            nw_in = _nsa_w_in_layout(nsa_w_in[a])
            nw_out = nsa_w_out[a].astype(BF16)
            w1 = nsa_cmp_w1[a].astype(BF16)
            w2 = nsa_cmp_w2[a].astype(BF16)
            proj = _mod_mm(xp, mp[3], mp[4], nw_in, rows_per_batch=t, tm=tm_mm, tn=512)
            kvc = _compress(proj, n_b, t, d_q // HEAD_DIM, nsa_cmp_pe[a], w1, w2)
            o_c, sel = _cmp_select(proj, kvc, bias_cmp, imap_t, n_b, t)
            o = _sel_win(proj, sel, bias_near, o_c, n_b, t)
            xp = _proj_res(o, nw_out, xp, mp[5], ln_g[i, 1], ln_b[i, 1], rows_per_batch=t, tm=tm_p)
            kv_all = proj[:, d_q:d_q + 3 * kvw].reshape(n_b, t, 3, 2, N_KV, HEAD_DIM)
            cmp_p.append(kv_all[:, :, 0])
            slc_p.append(kv_all[:, :, 1])
            win_p.append(kv_all[:, -min(WINDOW, t):, 2])
            proj_s = _mod_mm(xs, ms[3], ms[4], nw_in, rows_per_batch=SAMPLE_ROWS, tm=SAMPLE_ROWS, tn=512)[:n_s]
            past_cmp = cache_cmp_kv[a][page_table].reshape(n_s * past, kvw)
            kvc_s = _compress(past_cmp, n_s, past, 0, nsa_cmp_pe[a], w1, w2)
            q_s = proj_s[:, :d_q].reshape(n_s, N_KV, HPG, HEAD_DIM)
            kv_new = proj_s[:, d_q:d_q + 3 * kvw].reshape(n_s, 3, 2, N_KV, HEAD_DIM)
            gates_s = jax.nn.sigmoid(
                proj_s[:, d_q + 3 * kvw:].reshape(n_s, N_KV, HEAD_DIM)[:, :, :3 * HPG].reshape(n_s, N_KV, 3, HPG))
            o_samp, new_win = _sample_attention(q_s, kv_new, gates_s, kvc_s, cache_slc_kv[a], state_win_kv[a],
                                                page_table, rel_table)
            xs = _proj_res(_pad_rows(o_samp, SAMPLE_ROWS).astype(BF16), nw_out, xs, ms[5], ln_g[i, 1], ln_b[i, 1],
                           rows_per_batch=SAMPLE_ROWS, tm=SAMPLE_ROWS)
            cmp_s.append(kv_new[:, 0][:, None])
            slc_s.append(kv_new[:, 1][:, None])
            win_s.append(new_win)

        w_in = ffn_post_w_in[i].astype(BF16)
        w_out = ffn_post_w_out[i].astype(BF16)
        xp = _ffn(xp, mp[6], mp[7], mp[8], w_in, w_out, ln_g[i, 2], ln_b[i, 2], rows_per_batch=t, tm=tm_p)
        xs = _ffn(xs, ms[6], ms[7], ms[8], w_in, w_out, ln_g[i, 2], ln_b[i, 2],
                  rows_per_batch=SAMPLE_ROWS, tm=SAMPLE_ROWS)

    return (xp.reshape(n_b, t, d), xs[:n_s].reshape(n_s, 1, d),
            jnp.stack(cmp_p), jnp.stack(cmp_s), jnp.stack(slc_p), jnp.stack(slc_s),
            jnp.stack(win_p), jnp.stack(win_s), jnp.stack(gv_s))
```

```python
import functools
import math

import numpy as np
import jax
import jax.numpy as jnp
from jax import lax
from jax.experimental import pallas as pl
from jax.experimental.pallas import tpu as pltpu

F32 = jnp.float32
BF16 = jnp.bfloat16

DEPTH = 4
N_ADA = 9
N_HEADS = 16
HEAD_DIM = 128
N_KV = 4
HPG = N_HEADS // N_KV
GMLP_GROUPS = 16
CHUNK = 128
PAGE_SIZE = 128
CMP_LEN = 32
CMP_STRIDE = 16
SLC_LEN = 64
N_SEL = 16
WINDOW = 512
Q_BLOCK = 128
REL_BUCKETS = 32
REL_MAX_DIST = 128
LN_EPS = 1e-5
DN_ALPHA = (2 * DEPTH) ** 0.25
FORCE_BONUS = 1e4
ATT_SCALE = HEAD_DIM ** -0.5

SAMPLE_ROWS = 16
NEG_MASK = -30000.0
NEG_BIAS = -1e30
SEL_MASK = 32768.0
KEY_PAD = WINDOW
SEL_CHUNK = 512
VMEM_LIMIT = 56 * 1024 * 1024


def _cparams(sem):
    return pltpu.CompilerParams(dimension_semantics=sem, vmem_limit_bytes=VMEM_LIMIT)


def _dot(a, b):
    return jnp.dot(a, b, preferred_element_type=F32)


def _ln_rows(z, g, b):
    mu = jnp.mean(z, axis=-1, keepdims=True)
    zc = z - mu
    var = jnp.mean(zc * zc, axis=-1, keepdims=True)
    return zc * lax.rsqrt(var + LN_EPS) * g + b


def _silu(x):
    return x * jax.nn.sigmoid(x)


def _ada_kernel(c_ref, w_ref, b_ref, o_ref):
    h = _silu(c_ref[...]).astype(BF16)
    o_ref[0] = _dot(h, w_ref[0].astype(BF16)) + b_ref[0]


def _ada(c_all, ada_w, ada_b, tn=1024):
    depth, d, n = ada_w.shape
    r = c_all.shape[0]
    return pl.pallas_call(
        _ada_kernel,
        out_shape=jax.ShapeDtypeStruct((depth, r, n), F32),
        grid=(depth, n // tn),
        in_specs=[pl.BlockSpec((r, d), lambda l, j: (0, 0)),
                  pl.BlockSpec((1, d, tn), lambda l, j: (l, 0, j)),
                  pl.BlockSpec((1, 1, tn), lambda l, j: (l, 0, j))],
        out_specs=pl.BlockSpec((1, r, tn), lambda l, j: (l, 0, j)),
        compiler_params=_cparams(("parallel", "parallel")),
        name="ada",
    )(c_all, ada_w, ada_b.reshape(depth, 1, n))


def _ffn_kernel(x_ref, sh_ref, sc_ref, gt_ref, wg_ref, wu_ref, wo_ref, lg_ref, lb_ref,
                o_ref, h_ref, acc_ref):
    j = pl.program_id(1)

    @pl.when(j == 0)
    def _():
        h_ref[...] = (x_ref[...] * (1.0 + sc_ref[0]) + sh_ref[0]).astype(BF16)
        acc_ref[...] = jnp.zeros_like(acc_ref)

    h = h_ref[...]
    g = _dot(h, wg_ref[...])
    u = _dot(h, wu_ref[...])
    a = (_silu(g) * u).astype(BF16)
    acc_ref[...] += _dot(a, wo_ref[...])

    @pl.when(j == pl.num_programs(1) - 1)
    def _():
        z = DN_ALPHA * x_ref[...] + (1.0 + gt_ref[0]) * (0.5 * acc_ref[...])
        o_ref[...] = _ln_rows(z, lg_ref[...], lb_ref[...])


def _ffn(x, shift, scale, gate, w_in, w_out, ln_g, ln_b, *, rows_per_batch, tm, tf=512):
    m, d = x.shape
    dff = w_out.shape[0]
    nf = dff // tf
    r = shift.shape[1]
    tpb = rows_per_batch // tm
    mod_spec = pl.BlockSpec((1, r, d), lambda i, j: (i // tpb, 0, 0))
    vec_spec = pl.BlockSpec((1, d), lambda i, j: (0, 0))
    return pl.pallas_call(
        _ffn_kernel,
        out_shape=jax.ShapeDtypeStruct((m, d), F32),
        grid=(m // tm, nf),
        in_specs=[pl.BlockSpec((tm, d), lambda i, j: (i, 0)),
                  mod_spec, mod_spec, mod_spec,
                  pl.BlockSpec((d, tf), lambda i, j: (0, j)),
                  pl.BlockSpec((d, tf), lambda i, j: (0, j + nf)),
                  pl.BlockSpec((tf, d), lambda i, j: (j, 0)),
                  vec_spec, vec_spec],
        out_specs=pl.BlockSpec((tm, d), lambda i, j: (i, 0)),
        scratch_shapes=[pltpu.VMEM((tm, d), BF16), pltpu.VMEM((tm, d), F32)],
        compiler_params=_cparams(("parallel", "arbitrary")),
        name="ffn",
    )(x, shift, scale, gate, w_in, w_in, w_out, ln_g.reshape(1, d), ln_b.reshape(1, d))


def _mm_kernel(x_ref, sh_ref, sc_ref, w_ref, o_ref, h_ref, *, act):
    @pl.when(pl.program_id(1) == 0)
    def _():
        h_ref[...] = (x_ref[...] * (1.0 + sc_ref[0]) + sh_ref[0]).astype(BF16)

    y = _dot(h_ref[...], w_ref[...])
    if act == "gelu":
        y = jax.nn.gelu(y)
    o_ref[...] = y.astype(o_ref.dtype)


def _mod_mm(x, shift, scale, w, *, rows_per_batch, tm, tn, act=None):
    m, d = x.shape
    n = w.shape[1]
    r = shift.shape[1]
    tpb = rows_per_batch // tm
    mod_spec = pl.BlockSpec((1, r, d), lambda i, j: (i // tpb, 0, 0))
    return pl.pallas_call(
        functools.partial(_mm_kernel, act=act),
        out_shape=jax.ShapeDtypeStruct((m, n), F32),
        grid=(m // tm, n // tn),
        in_specs=[pl.BlockSpec((tm, d), lambda i, j: (i, 0)),
                  mod_spec, mod_spec,
                  pl.BlockSpec((d, tn), lambda i, j: (0, j))],
        out_specs=pl.BlockSpec((tm, tn), lambda i, j: (i, j)),
        scratch_shapes=[pltpu.VMEM((tm, d), BF16)],
        compiler_params=_cparams(("parallel", "arbitrary")),
        name="mod_mm",
    )(x, shift, scale, w)


GMLP_GROUPS_PER_STEP = 2


def _gmlp_gate_kernel(x_ref, gt_ref, u_ref, vfull_ref, vg_ref, vlg_ref, vlb_ref, ws_ref, bs_ref,
                      wo_ref, lg_ref, lb_ref, o_ref, *rest, chunk, gw, emit_vn):
    if emit_vn:
        vn_ref, acc_ref, mu_ref, rs_ref = rest
    else:
        acc_ref, mu_ref, rs_ref = rest
    k = pl.program_id(1)
    tm = x_ref.shape[0]

    @pl.when(k == 0)
    def _():
        v = vfull_ref[...]
        mu = jnp.mean(v, axis=-1, keepdims=True)
        vc = v - mu
        var = jnp.mean(vc * vc, axis=-1, keepdims=True)
        mu_ref[...] = mu
        rs_ref[...] = lax.rsqrt(var + LN_EPS)
        acc_ref[...] = jnp.zeros_like(acc_ref)

    vn = (vg_ref[...] - mu_ref[...]) * rs_ref[...] * vlg_ref[...] + vlb_ref[...]
    if emit_vn:
        vn_ref[...] = vn
    parts = []
    for gi in range(GMLP_GROUPS_PER_STEP):
        vn_g = vn[:, gi * gw:(gi + 1) * gw]
        if chunk == 1:
            s_g = ws_ref[gi, 0:1, 0:1] * vn_g + bs_ref[gi, 0:1, 0:1]
        else:
            row = lax.broadcasted_iota(jnp.int32, (chunk, chunk), 0)
            col = lax.broadcasted_iota(jnp.int32, (chunk, chunk), 1)
            w_tri = jnp.where(col <= row, ws_ref[gi], 0.0).astype(BF16)
            vb = vn_g.astype(BF16)
            s_g = jnp.concatenate(
                [_dot(w_tri, vb[c * chunk:(c + 1) * chunk]) + bs_ref[gi]
                 for c in range(tm // chunk)], axis=0)
        parts.append(s_g)
    s = jnp.concatenate(parts, axis=1)
    a = (u_ref[...] * s).astype(BF16)
    acc_ref[...] += _dot(a, wo_ref[...])

    @pl.when(k == pl.num_programs(1) - 1)
    def _():
        z = DN_ALPHA * x_ref[...] + (1.0 + gt_ref[0]) * acc_ref[...]
        o_ref[...] = _ln_rows(z, lg_ref[...], lb_ref[...])


def _gmlp_gate(x, gate, uv, v_ln_g, v_ln_b, w_s, b_s, w_out, ln_g, ln_b, *, rows_per_batch, tm, chunk,
               emit_vn):
    m, d = x.shape
    gdim = uv.shape[1] // 2
    gw = gdim // GMLP_GROUPS
    gw2 = gw * GMLP_GROUPS_PER_STEP
    ns = GMLP_GROUPS // GMLP_GROUPS_PER_STEP
    r = gate.shape[1]
    tpb = rows_per_batch // tm
    vec_spec = pl.BlockSpec((1, d), lambda i, k: (0, 0))
    out_shape = [jax.ShapeDtypeStruct((m, d), F32)]
    out_specs = [pl.BlockSpec((tm, d), lambda i, k: (i, 0))]
    if emit_vn:
        out_shape.append(jax.ShapeDtypeStruct((m, gdim), F32))
        out_specs.append(pl.BlockSpec((tm, gw2), lambda i, k: (i, k)))
    return pl.pallas_call(
        functools.partial(_gmlp_gate_kernel, chunk=chunk, gw=gw, emit_vn=emit_vn),
        out_shape=tuple(out_shape),
        grid=(m // tm, ns),
        in_specs=[pl.BlockSpec((tm, d), lambda i, k: (i, 0)),
                  pl.BlockSpec((1, r, d), lambda i, k: (i // tpb, 0, 0)),
                  pl.BlockSpec((tm, gw2), lambda i, k: (i, k)),
                  pl.BlockSpec((tm, gdim), lambda i, k: (i, 1)),
                  pl.BlockSpec((tm, gw2), lambda i, k: (i, k + ns)),
                  pl.BlockSpec((1, gw2), lambda i, k: (0, k)),
                  pl.BlockSpec((1, gw2), lambda i, k: (0, k)),
                  pl.BlockSpec((GMLP_GROUPS_PER_STEP, CHUNK, CHUNK), lambda i, k: (k, 0, 0)),
                  pl.BlockSpec((GMLP_GROUPS_PER_STEP, CHUNK, 1), lambda i, k: (k, 0, 0)),
                  pl.BlockSpec((gw2, d), lambda i, k: (k, 0)),
                  vec_spec, vec_spec],
        out_specs=tuple(out_specs),
        scratch_shapes=[pltpu.VMEM((tm, d), F32), pltpu.VMEM((tm, 1), F32), pltpu.VMEM((tm, 1), F32)],
        compiler_params=_cparams(("parallel", "arbitrary")),
        name="gmlp_gate",
    )(x, gate, uv, uv, uv, v_ln_g.reshape(1, gdim), v_ln_b.reshape(1, gdim), w_s,
      b_s.reshape(GMLP_GROUPS, CHUNK, 1), w_out, ln_g.reshape(1, d), ln_b.reshape(1, d))


def _compress_kernel(r_ref, pe_ref, w1_ref, w2_ref, o_ref, *, nseg):
    half = CMP_STRIDE * HEAD_DIM
    a = jnp.concatenate(
        [r_ref[pl.ds(r, nseg, stride=CMP_STRIDE), :].astype(BF16) for r in range(CMP_STRIDE)], axis=1)
    p0 = _dot(a, w1_ref[0, 0:half, :])
    p1 = _dot(a, w1_ref[0, half:2 * half, :])
    pe = jnp.broadcast_to(pe_ref[0], (8, 2 * half)).astype(BF16)
    peh = _dot(pe, w1_ref[0])[0:1]
    hid = peh + p0 + pltpu.roll(p1, nseg - 1, 0)
    o_ref[0, 0, 0] = _dot(jax.nn.gelu(hid).astype(BF16), w2_ref[0])


def _compress(rows2d, n_batch, t, col0, pe, w1, w2):
    nseg = t // CMP_STRIDE
    return pl.pallas_call(
        functools.partial(_compress_kernel, nseg=nseg),
        out_shape=jax.ShapeDtypeStruct((n_batch, 2, N_KV, nseg, HEAD_DIM), F32),
        grid=(n_batch, 2, N_KV),
        in_specs=[pl.BlockSpec((t, HEAD_DIM), lambda b, kv, g: (b, col0 + kv * N_KV + g)),
                  pl.BlockSpec((1, 1, CMP_LEN * HEAD_DIM), lambda b, kv, g: (kv, 0, 0)),
                  pl.BlockSpec((1, CMP_LEN * HEAD_DIM, w1.shape[2]), lambda b, kv, g: (kv, 0, 0)),
                  pl.BlockSpec((1, w2.shape[1], HEAD_DIM), lambda b, kv, g: (kv, 0, 0))],
        out_specs=pl.BlockSpec((1, 1, 1, nseg, HEAD_DIM), lambda b, kv, g: (b, kv, g, 0, 0)),
        compiler_params=_cparams(("parallel", "parallel", "parallel")),
        name="compress",
    )(rows2d, pe.reshape(2, 1, CMP_LEN * HEAD_DIM), w1, w2)


def _softmax_update_t(state, s, vt):
    m, l, acc = state
    m_new = jnp.maximum(m, jnp.max(s, axis=0, keepdims=True))
    alpha = jnp.exp(m - m_new)
    p = jnp.exp(s - m_new)
    l = alpha * l + jnp.sum(p, axis=0, keepdims=True)
    acc = alpha * acc + _dot(vt, p.astype(BF16))
    return m_new, l, acc


def _nsa_attn_kernel(q_ref, kc_ref, vc_ref, band_ref, imap_ref, ks_ref, vs_ref, kw_ref, vw_ref,
                     bsel_ref, bwin_ref, gate_ref, o_ref, ksa, vst, kwa, vwt, *, nslc, nsel):
    qb = pl.program_id(2)
    tq = Q_BLOCK
    cols = HPG * tq
    t = ks_ref.shape[0]
    nseg = kc_ref.shape[3]

    @pl.when(qb == 0)
    def _():
        lane_p = lax.broadcasted_iota(jnp.int32, (KEY_PAD, HEAD_DIM), 1)
        pad_mark = jnp.where(lane_p == HEAD_DIM - 1, 1.0, 0.0).astype(BF16)
        key_blk = lax.broadcasted_iota(jnp.int32, (t, HEAD_DIM), 0) // SLC_LEN
        lane = lax.broadcasted_iota(jnp.int32, (t, HEAD_DIM), 1)
        for ref in (ksa, kwa):
            ref[0:KEY_PAD, 0:HEAD_DIM] = jnp.zeros((KEY_PAD, HEAD_DIM), BF16)
            ref[0:KEY_PAD, HEAD_DIM:2 * HEAD_DIM] = pad_mark
        ksa[KEY_PAD:KEY_PAD + t, 0:HEAD_DIM] = ks_ref[...].astype(BF16)
        ksa[KEY_PAD:KEY_PAD + t, HEAD_DIM:2 * HEAD_DIM] = jnp.where(lane == key_blk, 1.0, 0.0).astype(BF16)
        kwa[KEY_PAD:KEY_PAD + t, 0:HEAD_DIM] = kw_ref[...].astype(BF16)
        kwa[KEY_PAD:KEY_PAD + t, HEAD_DIM:2 * HEAD_DIM] = jnp.zeros((t, HEAD_DIM), BF16)
        vst[:, 0:KEY_PAD] = jnp.zeros((HEAD_DIM, KEY_PAD), BF16)
        vwt[:, 0:KEY_PAD] = jnp.zeros((HEAD_DIM, KEY_PAD), BF16)
        for c in range(t // tq):
            vst[:, KEY_PAD + c * tq:KEY_PAD + (c + 1) * tq] = vs_ref[c * tq:(c + 1) * tq, :].T.astype(BF16)
            vwt[:, KEY_PAD + c * tq:KEY_PAD + (c + 1) * tq] = vw_ref[c * tq:(c + 1) * tq, :].T.astype(BF16)

    q_t = jnp.concatenate(
        [(q_ref[:, h * HEAD_DIM:(h + 1) * HEAD_DIM] * ATT_SCALE).T for h in range(HPG)], axis=1).astype(BF16)

    off = pl.multiple_of(nseg - (tq // CMP_STRIDE) * qb, 8)
    bias = jnp.concatenate([band_ref[h, pl.ds(off, nseg), :] for h in range(HPG)], axis=1)
    valid = bias > 0.5 * NEG_BIAS
    s = _dot(kc_ref[0, 0, 0].astype(BF16), q_t) + bias
    m = jnp.max(s, axis=0, keepdims=True)
    e = jnp.where(valid, jnp.exp(s - m), 0.0)
    den = jnp.sum(e, axis=0, keepdims=True)
    p = e / jnp.where(den > 0.0, den, 1.0)
    o_c = _dot(vc_ref[0, 0, 0].T.astype(BF16), p.astype(BF16))

    psum = p[:, 0:tq]
    for h in range(1, HPG):
        psum = psum + p[:, h * tq:(h + 1) * tq]
    p_hi = psum.astype(BF16)
    p_lo = (psum - p_hi.astype(F32)).astype(BF16)
    imap = imap_ref[...]
    imp = _dot(imap, p_hi) + _dot(imap, p_lo)
    jidx = lax.broadcasted_iota(jnp.int32, (nslc, tq), 0)
    qpos = qb * tq + lax.broadcasted_iota(jnp.int32, (nslc, tq), 1)
    cur = qpos // SLC_LEN
    valid_blk = jidx * SLC_LEN <= qpos
    forced = (jidx == 0) | (jidx == cur) | (jidx == cur - 1)
    score = jnp.where(valid_blk, imp + jnp.where(forced, FORCE_BONUS, 0.0), -jnp.inf)
    cnt = jnp.zeros((nslc, tq), jnp.int32)
    for jp in range(nslc):
        row = score[jp:jp + 1, :]
        ahead = (row > score) | ((row == score) & (jidx > jp))
        cnt = cnt + jnp.where(ahead, 1, 0)
    unsel = jnp.where((cnt < nsel) & valid_blk, 0.0, -SEL_MASK)

    marker_row = lax.broadcasted_iota(jnp.int32, (8, cols), 0) == 7
    aug = jnp.concatenate([jnp.concatenate([unsel] * HPG, axis=1),
                           jnp.zeros((HEAD_DIM - nslc - 8, cols), F32),
                           jnp.where(marker_row, -SEL_MASK, 0.0)], axis=0)
    rhs = jnp.concatenate([q_t, aug.astype(BF16)], axis=0)

    last = pl.multiple_of((qb + 1) * tq, tq)
    s = _dot(ksa[pl.ds(last, SEL_CHUNK), :], rhs) + bsel_ref[0]
    m = jnp.max(s, axis=0, keepdims=True)
    p = jnp.exp(s - m)
    st = (m, jnp.sum(p, axis=0, keepdims=True), _dot(vst[:, pl.ds(last, SEL_CHUNK)], p.astype(BF16)))

    def far_body(c, state):
        start = pl.multiple_of(last - (c + 1) * SEL_CHUNK, tq)
        return _softmax_update_t(state, _dot(ksa[pl.ds(start, SEL_CHUNK), :], rhs), vst[:, pl.ds(start, SEL_CHUNK)])

    _, l_s, acc_s = lax.fori_loop(0, qb // (SEL_CHUNK // tq), far_body, st)
    o_s = acc_s / l_s

    first = pl.multiple_of(qb * tq, tq)
    s = _dot(kwa[pl.ds(first, WINDOW + tq), :], rhs) + bwin_ref[0]
    p = jnp.exp(s - jnp.max(s, axis=0, keepdims=True))
    o_w = _dot(vwt[:, pl.ds(first, WINDOW + tq)], p.astype(BF16)) / jnp.sum(p, axis=0, keepdims=True)

    gates_t = jax.nn.sigmoid(gate_ref[...]).T
    outs = []
    for h in range(HPG):
        hs = slice(h * tq, (h + 1) * tq)
        merged = (gates_t[h:h + 1, :] * o_c[:, hs] + gates_t[HPG + h:HPG + h + 1, :] * o_s[:, hs]
                  + gates_t[2 * HPG + h:2 * HPG + h + 1, :] * o_w[:, hs])
        outs.append(merged.T)
    o_ref[...] = jnp.concatenate(outs, axis=1).astype(o_ref.dtype)


def _nsa_attn(proj, kvc, band_t, imap_t, bias_sel, bias_win, n_batch, t):
    nqb = t // Q_BLOCK
    nseg = t // CMP_STRIDE
    nslc = t // SLC_LEN
    nsel = min(N_SEL, nslc)
    d_q = N_HEADS * HEAD_DIM
    gcols = HPG * HEAD_DIM
    kv0 = d_q // HEAD_DIM
    gate0 = kv0 + 6 * N_KV

    def kv_spec(branch, kv):
        return pl.BlockSpec((t, HEAD_DIM), lambda b, g, i: (b, kv0 + (branch * 2 + kv) * N_KV + g))

    return pl.pallas_call(
        functools.partial(_nsa_attn_kernel, nslc=nslc, nsel=nsel),
        out_shape=jax.ShapeDtypeStruct((n_batch * t, d_q), BF16),
        grid=(n_batch, N_KV, nqb),
        in_specs=[pl.BlockSpec((Q_BLOCK, gcols), lambda b, g, i: (b * nqb + i, g)),
                  pl.BlockSpec((1, 1, 1, nseg, HEAD_DIM), lambda b, g, i: (b, 0, g, 0, 0)),
                  pl.BlockSpec((1, 1, 1, nseg, HEAD_DIM), lambda b, g, i: (b, 1, g, 0, 0)),
                  pl.BlockSpec((HPG, 2 * nseg, Q_BLOCK), lambda b, g, i: (g, 0, 0)),
                  pl.BlockSpec((nslc, nseg), lambda b, g, i: (0, 0)),
                  kv_spec(1, 0), kv_spec(1, 1), kv_spec(2, 0), kv_spec(2, 1),
                  pl.BlockSpec((1, SEL_CHUNK, HPG * Q_BLOCK), lambda b, g, i: (g, 0, 0)),
                  pl.BlockSpec((1, WINDOW + Q_BLOCK, HPG * Q_BLOCK), lambda b, g, i: (g, 0, 0)),
                  pl.BlockSpec((Q_BLOCK, HEAD_DIM), lambda b, g, i: (b * nqb + i, gate0 + g))],
        out_specs=pl.BlockSpec((Q_BLOCK, gcols), lambda b, g, i: (b * nqb + i, g)),
        scratch_shapes=[pltpu.VMEM((KEY_PAD + t, 2 * HEAD_DIM), BF16), pltpu.VMEM((HEAD_DIM, KEY_PAD + t), BF16),
                        pltpu.VMEM((KEY_PAD + t, 2 * HEAD_DIM), BF16), pltpu.VMEM((HEAD_DIM, KEY_PAD + t), BF16)],
        compiler_params=_cparams(("parallel", "parallel", "arbitrary")),
        name="nsa_attn",
    )(proj, kvc, kvc, band_t, imap_t, proj, proj, proj, proj, bias_sel, bias_win, proj)


def _proj_res_kernel(a_ref, w_ref, x_ref, gt_ref, lg_ref, lb_ref, o_ref):
    y = _dot(a_ref[...], w_ref[...])
    z = DN_ALPHA * x_ref[...] + (1.0 + gt_ref[0]) * y
    o_ref[...] = _ln_rows(z, lg_ref[...], lb_ref[...])


def _proj_res(a, w, x, gate, ln_g, ln_b, *, rows_per_batch, tm):
    m, d = x.shape
    k = a.shape[1]
    r = gate.shape[1]
    tpb = rows_per_batch // tm
    vec_spec = pl.BlockSpec((1, d), lambda i: (0, 0))
    return pl.pallas_call(
        _proj_res_kernel,
        out_shape=jax.ShapeDtypeStruct((m, d), F32),
        grid=(m // tm,),
        in_specs=[pl.BlockSpec((tm, k), lambda i: (i, 0)),
                  pl.BlockSpec((k, d), lambda i: (0, 0)),
                  pl.BlockSpec((tm, d), lambda i: (i, 0)),
                  pl.BlockSpec((1, r, d), lambda i: (i // tpb, 0, 0)),
                  vec_spec, vec_spec],
        out_specs=pl.BlockSpec((tm, d), lambda i: (i, 0)),
        compiler_params=_cparams(("parallel",)),
        name="proj_res",
    )(a, w, x, gate, ln_g.reshape(1, d), ln_b.reshape(1, d))


def _rel_bucket_np(dist):
    dist = np.maximum(dist, 0)
    exact = REL_BUCKETS // 2
    logv = (np.log(np.maximum(dist, 1).astype(np.float32) / np.float32(exact))
            / np.float32(math.log(REL_MAX_DIST / exact))).astype(np.float32)
    large = exact + (logv * np.float32(REL_BUCKETS - exact)).astype(np.int32)
    return np.where(dist < exact, dist, np.minimum(large, REL_BUCKETS - 1)).astype(np.int32)


def _cmp_to_slc_np(nseg, nslc):
    i = np.arange(nseg)[:, None] * CMP_STRIDE
    j = np.arange(nslc)[None, :] * SLC_LEN
    ov = np.minimum(i + CMP_LEN, j + SLC_LEN) - np.maximum(i, j)
    return (np.maximum(ov, 0) / CMP_STRIDE).astype(np.float32)


def _bias_tables(rel_table, t):
    tbl_c = rel_table.T - rel_table[REL_BUCKETS - 1][:, None]
    nseg = t // CMP_STRIDE
    ql = np.arange(Q_BLOCK)[None, :]
    dist_c = ql - ((np.arange(2 * nseg)[:, None] - nseg) * CMP_STRIDE + CMP_LEN - 1)
    band_t = jnp.where(dist_c >= 0, jnp.take(tbl_c, _rel_bucket_np(dist_c), axis=1), NEG_BIAS)

    def per_group(dist, ok):
        tab = jnp.where(ok, jnp.take(tbl_c, _rel_bucket_np(dist), axis=1), NEG_MASK)
        n_keys = dist.shape[0]
        return tab.reshape(N_KV, HPG, n_keys, Q_BLOCK).transpose(0, 2, 1, 3).reshape(N_KV, n_keys, HPG * Q_BLOCK)

    dist_s = ql + (SEL_CHUNK - Q_BLOCK) - np.arange(SEL_CHUNK)[:, None]
    dist_w = ql + WINDOW - np.arange(WINDOW + Q_BLOCK)[:, None]
    return band_t, per_group(dist_s, dist_s >= 0), per_group(dist_w, (dist_w >= 0) & (dist_w < WINDOW))


def _masked_softmax(s, mask):
    s = jnp.where(mask, s, -jnp.inf)
    m = jnp.max(s, axis=-1, keepdims=True)
    m = jnp.where(jnp.isfinite(m), m, 0.0)
    e = jnp.exp(s - m)
    d = jnp.sum(e, axis=-1, keepdims=True)
    return e / jnp.where(d > 0, d, 1.0)


def _sample_attention(q, kv_new, gates, kvc, slc_rows, page0, win_buf, page_table, rel_table):
    n_seq, n_pages = page_table.shape
    past = n_pages * PAGE_SIZE
    pos = past
    nseg = kvc.shape[3]
    tbl = rel_table.reshape(REL_BUCKETS, N_KV, HPG)
    dist = pos - (np.arange(nseg) * CMP_STRIDE + CMP_LEN - 1)
    bias = tbl[_rel_bucket_np(dist)].transpose(1, 2, 0)
    s = jnp.einsum('sghd,sgnd->sghn', q, kvc[:, 0]) * ATT_SCALE + bias
    p = _masked_softmax(s, jnp.asarray(dist >= 0))
    o_c = jnp.einsum('sghn,sgnd->sghd', p, kvc[:, 1])
    nslc = -(-(past + 1) // SLC_LEN)
    nsel = min(N_SEL, nslc)
    imp = jnp.einsum('sghn,nj->sgj', p, jnp.asarray(_cmp_to_slc_np(nseg, nslc)))
    blk = np.arange(nslc)
    cur = pos // SLC_LEN
    forced = (blk == 0) | (blk == cur) | (blk == cur - 1)
    score = jnp.where(jnp.asarray(blk * SLC_LEN <= pos), imp + jnp.where(jnp.asarray(forced), FORCE_BONUS, 0.0),
                      -jnp.inf)
    top_s, top_i = lax.top_k(score, nsel)
    per_page = PAGE_SIZE // SLC_LEN
    past_i = jnp.minimum(top_i, past // SLC_LEN - 1)
    page = page_table[jnp.arange(n_seq)[:, None, None], past_i // per_page]
    row = (past_i % per_page)[..., None] * SLC_LEN + jnp.arange(SLC_LEN)
    rows_all = slc_rows[(page0 + page)[..., None] * PAGE_SIZE + row]
    blocks = jnp.stack([rows_all[:, g, :, :, :, g, :] for g in range(N_KV)], axis=1)
    new_blk = jnp.where((jnp.arange(SLC_LEN) == 0)[None, None, :, None, None],
                        kv_new[:, 1].transpose(0, 2, 1, 3)[:, :, None], 0.0)
    is_new = (top_i >= past // SLC_LEN)[..., None, None, None]
    blocks = jnp.where(is_new, new_blk[:, :, None], blocks)
    nk = nsel * SLC_LEN
    k_sel = blocks[..., 0, :].reshape(n_seq, N_KV, nk, HEAD_DIM)
    v_sel = blocks[..., 1, :].reshape(n_seq, N_KV, nk, HEAD_DIM)
    key_pos = (top_i[..., None] * SLC_LEN + jnp.arange(SLC_LEN)).reshape(n_seq, N_KV, nk)
    dist_s = pos - key_pos
    ok = jnp.repeat(jnp.isfinite(top_s), SLC_LEN, axis=-1) & (dist_s >= 0)
    d_cl = jnp.maximum(dist_s, 0)
    logv = jnp.log(jnp.maximum(d_cl, 1).astype(F32) / (REL_BUCKETS // 2)) / math.log(REL_MAX_DIST / (REL_BUCKETS // 2))
    large = REL_BUCKETS // 2 + (logv * (REL_BUCKETS - REL_BUCKETS // 2)).astype(jnp.int32)
    bucket = jnp.where(d_cl < REL_BUCKETS // 2, d_cl, jnp.minimum(large, REL_BUCKETS - 1))
    bias_s = jnp.moveaxis(tbl[bucket, jnp.arange(N_KV)[None, :, None]], -1, 2)
    s2 = jnp.einsum('sghd,sgkd->sghk', q, k_sel) * ATT_SCALE + bias_s
    p2 = _masked_softmax(s2, ok[:, :, None])
    o_s = jnp.einsum('sghk,sgkd->sghd', p2, v_sel)
    buf_len = win_buf.shape[1]
    rows_w = jnp.concatenate([win_buf, kv_new[:, 2][:, None]], axis=1)
    dist_w = pos - (past - buf_len + np.arange(buf_len + 1))
    mask_w = jnp.asarray((dist_w >= 0) & (dist_w < WINDOW))
    bias_w = tbl[_rel_bucket_np(dist_w)].transpose(1, 2, 0)
    s3 = jnp.einsum('sghd,skgd->sghk', q, rows_w[:, :, 0]) * ATT_SCALE + bias_w
    p3 = _masked_softmax(s3, mask_w)
    o_w = jnp.einsum('sghk,skgd->sghd', p3, rows_w[:, :, 1])
    new_win = rows_w[:, -min(WINDOW, buf_len + 1):]
    o = gates[:, :, 0, :, None] * o_c + gates[:, :, 1, :, None] * o_s + gates[:, :, 2, :, None] * o_w
    return o.reshape(n_seq, N_HEADS * HEAD_DIM), new_win


def _pad_rows(a, rows):
    return jnp.pad(a, ((0, rows - a.shape[0]),) + ((0, 0),) * (a.ndim - 1))


def _nsa_w_in_layout(w):
    d = w.shape[0]
    n_main = N_HEADS * HEAD_DIM + 6 * N_KV * HEAD_DIM
    wg = w[:, n_main:].reshape(d, 3, N_KV, HPG).transpose(0, 2, 1, 3).reshape(d, N_KV, 3 * HPG)
    wg = jnp.pad(wg, ((0, 0), (0, 0), (0, HEAD_DIM - 3 * HPG))).reshape(d, N_KV * HEAD_DIM)
    return jnp.concatenate([w[:, :n_main], wg], axis=1).astype(BF16)


def kernel(x_prompt, x_sample, cache_cmp_kv, cache_slc_kv, state_win_kv, page_table, c_prompt, c_sample,
           ada_w, ada_b, ln_g, ln_b, ffn_pre_w_in, ffn_pre_w_out, ffn_post_w_in, ffn_post_w_out,
           gmlp_w_in, gmlp_ln_g, gmlp_ln_b, gmlp_w_s, gmlp_b_s, gmlp_w_out,
           nsa_w_in, nsa_cmp_pe, nsa_cmp_w1, nsa_cmp_w2, nsa_w_out, rel_table):
    n_b, t, d = x_prompt.shape
    n_s = x_sample.shape[0]
    n_pages = page_table.shape[1]
    n_phys = cache_cmp_kv.shape[1]
    past = n_pages * PAGE_SIZE
    d_q = N_HEADS * HEAD_DIM
    kvw = 2 * N_KV * HEAD_DIM
    tm_p = min(512, t)
    tm_mm = min(1024, t)
    tm_g = min(256, t)

    xp = x_prompt.reshape(n_b * t, d)
    xs = _pad_rows(x_sample.reshape(n_s, d), SAMPLE_ROWS)
    c_all = _pad_rows(jnp.concatenate([c_prompt, c_sample], axis=0), SAMPLE_ROWS)
    mod = _ada(c_all, ada_w, ada_b)
    band_t, bias_sel, bias_win = _bias_tables(rel_table, t)
    imap_t = jnp.asarray(_cmp_to_slc_np(t // CMP_STRIDE, t // SLC_LEN).T).astype(BF16)
    cmp_pages = cache_cmp_kv.reshape((-1,) + cache_cmp_kv.shape[2:])
    slc_rows = cache_slc_kv.reshape((-1,) + cache_slc_kv.shape[3:])

    cmp_p, cmp_s, slc_p, slc_s, win_p, win_s, gv_s = [], [], [], [], [], [], []
    for i in range(DEPTH):
        mi = mod[i].reshape(SAMPLE_ROWS, N_ADA, d)
        mp = [mi[:n_b, k][:, None, :] for k in range(N_ADA)]
        ms = [_pad_rows(mi[n_b:n_b + n_s, k], SAMPLE_ROWS)[None] for k in range(N_ADA)]

        w_in = ffn_pre_w_in[i].astype(BF16)
        w_out = ffn_pre_w_out[i].astype(BF16)
        xp = _ffn(xp, mp[0], mp[1], mp[2], w_in, w_out, ln_g[i, 0], ln_b[i, 0], rows_per_batch=t, tm=tm_p)
        xs = _ffn(xs, ms[0], ms[1], ms[2], w_in, w_out, ln_g[i, 0], ln_b[i, 0],
                  rows_per_batch=SAMPLE_ROWS, tm=SAMPLE_ROWS)

        if i % 2 == 0:
            a = i // 2
            gw_in = gmlp_w_in[a].astype(BF16)
            gw_out = gmlp_w_out[a].astype(BF16)
            uv_p = _mod_mm(xp, mp[3], mp[4], gw_in, rows_per_batch=t, tm=tm_mm, tn=1024, act="gelu")
            xp, = _gmlp_gate(xp, mp[5], uv_p, gmlp_ln_g[a], gmlp_ln_b[a], gmlp_w_s[a], gmlp_b_s[a], gw_out,
                             ln_g[i, 1], ln_b[i, 1], rows_per_batch=t, tm=tm_g, chunk=min(CHUNK, t),
                             emit_vn=False)
            uv_s = _mod_mm(xs, ms[3], ms[4], gw_in, rows_per_batch=SAMPLE_ROWS, tm=SAMPLE_ROWS, tn=1024,
                           act="gelu")
            xs, vn_s = _gmlp_gate(xs, ms[5], uv_s, gmlp_ln_g[a], gmlp_ln_b[a], gmlp_w_s[a], gmlp_b_s[a], gw_out,
                                  ln_g[i, 1], ln_b[i, 1], rows_per_batch=SAMPLE_ROWS, tm=SAMPLE_ROWS, chunk=1,
                                  emit_vn=True)
            gv_s.append(vn_s[:n_s].reshape(n_s, 1, -1))
        else:
            a = i // 2
            nw_in = _nsa_w_in_layout(nsa_w_in[a])
            nw_out = nsa_w_out[a].astype(BF16)
            w1 = nsa_cmp_w1[a].astype(BF16)
            w2 = nsa_cmp_w2[a].astype(BF16)
            proj = _mod_mm(xp, mp[3], mp[4], nw_in, rows_per_batch=t, tm=tm_mm, tn=512)
            kvc = _compress(proj, n_b, t, d_q // HEAD_DIM, nsa_cmp_pe[a], w1, w2)
            o = _nsa_attn(proj, kvc, band_t, imap_t, bias_sel, bias_win, n_b, t)
            xp = _proj_res(o, nw_out, xp, mp[5], ln_g[i, 1], ln_b[i, 1], rows_per_batch=t, tm=tm_p)
            kv_all = proj[:, d_q:d_q + 3 * kvw].reshape(n_b, t, 3, 2, N_KV, HEAD_DIM)
            cmp_p.append(kv_all[:, :, 0])
            slc_p.append(kv_all[:, :, 1])
            win_p.append(kv_all[:, -min(WINDOW, t):, 2])
            proj_s = _mod_mm(xs, ms[3], ms[4], nw_in, rows_per_batch=SAMPLE_ROWS, tm=SAMPLE_ROWS, tn=512)[:n_s]
            past_cmp = cmp_pages[a * n_phys + page_table].reshape(n_s * past, kvw)
            kvc_s = _compress(past_cmp, n_s, past, 0, nsa_cmp_pe[a], w1, w2)
            q_s = proj_s[:, :d_q].reshape(n_s, N_KV, HPG, HEAD_DIM)
            kv_new = proj_s[:, d_q:d_q + 3 * kvw].reshape(n_s, 3, 2, N_KV, HEAD_DIM)
            gates_s = jax.nn.sigmoid(
                proj_s[:, d_q + 3 * kvw:].reshape(n_s, N_KV, HEAD_DIM)[:, :, :3 * HPG].reshape(n_s, N_KV, 3, HPG))
            o_samp, new_win = _sample_attention(q_s, kv_new, gates_s, kvc_s, slc_rows, a * n_phys,
                                                state_win_kv[a], page_table, rel_table)
            xs = _proj_res(_pad_rows(o_samp, SAMPLE_ROWS).astype(BF16), nw_out, xs, ms[5], ln_g[i, 1], ln_b[i, 1],
                           rows_per_batch=SAMPLE_ROWS, tm=SAMPLE_ROWS)
            cmp_s.append(kv_new[:, 0][:, None])
            slc_s.append(kv_new[:, 1][:, None])
            win_s.append(new_win)

        w_in = ffn_post_w_in[i].astype(BF16)
        w_out = ffn_post_w_out[i].astype(BF16)
        xp = _ffn(xp, mp[6], mp[7], mp[8], w_in, w_out, ln_g[i, 2], ln_b[i, 2], rows_per_batch=t, tm=tm_p)
        xs = _ffn(xs, ms[6], ms[7], ms[8], w_in, w_out, ln_g[i, 2], ln_b[i, 2],
                  rows_per_batch=SAMPLE_ROWS, tm=SAMPLE_ROWS)

    return (xp.reshape(n_b, t, d), xs[:n_s].reshape(n_s, 1, d),
            jnp.stack(cmp_p), jnp.stack(cmp_s), jnp.stack(slc_p), jnp.stack(slc_s),
            jnp.stack(win_p), jnp.stack(win_s), jnp.stack(gv_s))
```

```python
import functools
import math

import numpy as np
import jax
import jax.numpy as jnp
from jax import lax
from jax.experimental import pallas as pl
from jax.experimental.pallas import tpu as pltpu

F32 = jnp.float32
BF16 = jnp.bfloat16

DEPTH = 4
N_ADA = 9
N_HEADS = 16
HEAD_DIM = 128
N_KV = 4
HPG = N_HEADS // N_KV
GMLP_GROUPS = 16
CHUNK = 128
PAGE_SIZE = 128
CMP_LEN = 32
CMP_STRIDE = 16
SLC_LEN = 64
N_SEL = 16
WINDOW = 512
Q_BLOCK = 128
REL_BUCKETS = 32
REL_MAX_DIST = 128
LN_EPS = 1e-5
DN_ALPHA = (2 * DEPTH) ** 0.25
FORCE_BONUS = 1e4
ATT_SCALE = HEAD_DIM ** -0.5

SAMPLE_ROWS = 16
NEG_MASK = -30000.0
NEG_BIAS = -1e30
SEL_MASK = 32768.0
KEY_PAD = WINDOW
SEL_CHUNK = 512
VMEM_LIMIT = 56 * 1024 * 1024


def _cparams(sem):
    return pltpu.CompilerParams(dimension_semantics=sem, vmem_limit_bytes=VMEM_LIMIT)


def _dot(a, b):
    return jnp.dot(a, b, preferred_element_type=F32)


def _dot_nt(a, b):
    return lax.dot_general(a, b, (((1,), (1,)), ((), ())), preferred_element_type=F32)


def _ln_rows(z, g, b):
    mu = jnp.mean(z, axis=-1, keepdims=True)
    zc = z - mu
    var = jnp.mean(zc * zc, axis=-1, keepdims=True)
    return zc * lax.rsqrt(var + LN_EPS) * g + b


def _silu(x):
    return x * jax.nn.sigmoid(x)


def _ada_kernel(c_ref, w_ref, b_ref, o_ref):
    h = _silu(c_ref[...]).astype(BF16)
    o_ref[0] = _dot(h, w_ref[0].astype(BF16)) + b_ref[0]


def _ada(c_all, ada_w, ada_b, tn=1024):
    depth, d, n = ada_w.shape
    r = c_all.shape[0]
    return pl.pallas_call(
        _ada_kernel,
        out_shape=jax.ShapeDtypeStruct((depth, r, n), F32),
        grid=(depth, n // tn),
        in_specs=[pl.BlockSpec((r, d), lambda l, j: (0, 0)),
                  pl.BlockSpec((1, d, tn), lambda l, j: (l, 0, j)),
                  pl.BlockSpec((1, 1, tn), lambda l, j: (l, 0, j))],
        out_specs=pl.BlockSpec((1, r, tn), lambda l, j: (l, 0, j)),
        compiler_params=_cparams(("parallel", "parallel")),
        name="ada",
    )(c_all, ada_w, ada_b.reshape(depth, 1, n))


def _ffn_kernel(x_ref, sh_ref, sc_ref, gt_ref, wg_ref, wu_ref, wo_ref, lg_ref, lb_ref,
                o_ref, h_ref, acc_ref):
    j = pl.program_id(1)

    @pl.when(j == 0)
    def _():
        h_ref[...] = (x_ref[...] * (1.0 + sc_ref[0]) + sh_ref[0]).astype(BF16)
        acc_ref[...] = jnp.zeros_like(acc_ref)

    h = h_ref[...]
    g = _dot(h, wg_ref[...])
    u = _dot(h, wu_ref[...])
    a = (_silu(g) * u).astype(BF16)
    acc_ref[...] += _dot(a, wo_ref[...])

    @pl.when(j == pl.num_programs(1) - 1)
    def _():
        z = DN_ALPHA * x_ref[...] + (1.0 + gt_ref[0]) * (0.5 * acc_ref[...])
        o_ref[...] = _ln_rows(z, lg_ref[...], lb_ref[...])


def _ffn(x, shift, scale, gate, w_in, w_out, ln_g, ln_b, *, rows_per_batch, tm, tf=512):
    m, d = x.shape
    dff = w_out.shape[0]
    nf = dff // tf
    r = shift.shape[1]
    tpb = rows_per_batch // tm
    mod_spec = pl.BlockSpec((1, r, d), lambda i, j: (i // tpb, 0, 0))
    vec_spec = pl.BlockSpec((1, d), lambda i, j: (0, 0))
    return pl.pallas_call(
        _ffn_kernel,
        out_shape=jax.ShapeDtypeStruct((m, d), F32),
        grid=(m // tm, nf),
        in_specs=[pl.BlockSpec((tm, d), lambda i, j: (i, 0)),
                  mod_spec, mod_spec, mod_spec,
                  pl.BlockSpec((d, tf), lambda i, j: (0, j)),
                  pl.BlockSpec((d, tf), lambda i, j: (0, j + nf)),
                  pl.BlockSpec((tf, d), lambda i, j: (j, 0)),
                  vec_spec, vec_spec],
        out_specs=pl.BlockSpec((tm, d), lambda i, j: (i, 0)),
        scratch_shapes=[pltpu.VMEM((tm, d), BF16), pltpu.VMEM((tm, d), F32)],
        compiler_params=_cparams(("parallel", "arbitrary")),
        name="ffn",
    )(x, shift, scale, gate, w_in, w_in, w_out, ln_g.reshape(1, d), ln_b.reshape(1, d))


def _mm_kernel(x_ref, sh_ref, sc_ref, w_ref, o_ref, h_ref, *, act):
    @pl.when(pl.program_id(1) == 0)
    def _():
        h_ref[...] = (x_ref[...] * (1.0 + sc_ref[0]) + sh_ref[0]).astype(BF16)

    y = _dot(h_ref[...], w_ref[...])
    if act == "gelu":
        y = jax.nn.gelu(y)
    o_ref[...] = y.astype(o_ref.dtype)


def _mod_mm(x, shift, scale, w, *, rows_per_batch, tm, tn, act=None):
    m, d = x.shape
    n = w.shape[1]
    r = shift.shape[1]
    tpb = rows_per_batch // tm
    mod_spec = pl.BlockSpec((1, r, d), lambda i, j: (i // tpb, 0, 0))
    return pl.pallas_call(
        functools.partial(_mm_kernel, act=act),
        out_shape=jax.ShapeDtypeStruct((m, n), F32),
        grid=(m // tm, n // tn),
        in_specs=[pl.BlockSpec((tm, d), lambda i, j: (i, 0)),
                  mod_spec, mod_spec,
                  pl.BlockSpec((d, tn), lambda i, j: (0, j))],
        out_specs=pl.BlockSpec((tm, tn), lambda i, j: (i, j)),
        scratch_shapes=[pltpu.VMEM((tm, d), BF16)],
        compiler_params=_cparams(("parallel", "arbitrary")),
        name="mod_mm",
    )(x, shift, scale, w)


GMLP_GROUPS_PER_STEP = 2


def _gmlp_gate_kernel(x_ref, gt_ref, u_ref, vfull_ref, vg_ref, vlg_ref, vlb_ref, ws_ref, bs_ref,
                      wo_ref, lg_ref, lb_ref, o_ref, *rest, chunk, gw, emit_vn):
    if emit_vn:
        vn_ref, acc_ref, mu_ref, rs_ref = rest
    else:
        acc_ref, mu_ref, rs_ref = rest
    k = pl.program_id(1)
    tm = x_ref.shape[0]

    @pl.when(k == 0)
    def _():
        v = vfull_ref[...]
        mu = jnp.mean(v, axis=-1, keepdims=True)
        vc = v - mu
        var = jnp.mean(vc * vc, axis=-1, keepdims=True)
        mu_ref[...] = mu
        rs_ref[...] = lax.rsqrt(var + LN_EPS)
        acc_ref[...] = jnp.zeros_like(acc_ref)

    vn = (vg_ref[...] - mu_ref[...]) * rs_ref[...] * vlg_ref[...] + vlb_ref[...]
    if emit_vn:
        vn_ref[...] = vn
    parts = []
    for gi in range(GMLP_GROUPS_PER_STEP):
        vn_g = vn[:, gi * gw:(gi + 1) * gw]
        if chunk == 1:
            s_g = ws_ref[gi, 0:1, 0:1] * vn_g + bs_ref[gi, 0:1, 0:1]
        else:
            row = lax.broadcasted_iota(jnp.int32, (chunk, chunk), 0)
            col = lax.broadcasted_iota(jnp.int32, (chunk, chunk), 1)
            w_tri = jnp.where(col <= row, ws_ref[gi], 0.0).astype(BF16)
            vb = vn_g.astype(BF16)
            s_g = jnp.concatenate(
                [_dot(w_tri, vb[c * chunk:(c + 1) * chunk]) + bs_ref[gi]
                 for c in range(tm // chunk)], axis=0)
        parts.append(s_g)
    s = jnp.concatenate(parts, axis=1)
    a = (u_ref[...] * s).astype(BF16)
    acc_ref[...] += _dot(a, wo_ref[...])

    @pl.when(k == pl.num_programs(1) - 1)
    def _():
        z = DN_ALPHA * x_ref[...] + (1.0 + gt_ref[0]) * acc_ref[...]
        o_ref[...] = _ln_rows(z, lg_ref[...], lb_ref[...])


def _gmlp_gate(x, gate, uv, v_ln_g, v_ln_b, w_s, b_s, w_out, ln_g, ln_b, *, rows_per_batch, tm, chunk,
               emit_vn):
    m, d = x.shape
    gdim = uv.shape[1] // 2
    gw = gdim // GMLP_GROUPS
    gw2 = gw * GMLP_GROUPS_PER_STEP
    ns = GMLP_GROUPS // GMLP_GROUPS_PER_STEP
    r = gate.shape[1]
    tpb = rows_per_batch // tm
    vec_spec = pl.BlockSpec((1, d), lambda i, k: (0, 0))
    out_shape = [jax.ShapeDtypeStruct((m, d), F32)]
    out_specs = [pl.BlockSpec((tm, d), lambda i, k: (i, 0))]
    if emit_vn:
        out_shape.append(jax.ShapeDtypeStruct((m, gdim), F32))
        out_specs.append(pl.BlockSpec((tm, gw2), lambda i, k: (i, k)))
    return pl.pallas_call(
        functools.partial(_gmlp_gate_kernel, chunk=chunk, gw=gw, emit_vn=emit_vn),
        out_shape=tuple(out_shape),
        grid=(m // tm, ns),
        in_specs=[pl.BlockSpec((tm, d), lambda i, k: (i, 0)),
                  pl.BlockSpec((1, r, d), lambda i, k: (i // tpb, 0, 0)),
                  pl.BlockSpec((tm, gw2), lambda i, k: (i, k)),
                  pl.BlockSpec((tm, gdim), lambda i, k: (i, 1)),
                  pl.BlockSpec((tm, gw2), lambda i, k: (i, k + ns)),
                  pl.BlockSpec((1, gw2), lambda i, k: (0, k)),
                  pl.BlockSpec((1, gw2), lambda i, k: (0, k)),
                  pl.BlockSpec((GMLP_GROUPS_PER_STEP, CHUNK, CHUNK), lambda i, k: (k, 0, 0)),
                  pl.BlockSpec((GMLP_GROUPS_PER_STEP, CHUNK, 1), lambda i, k: (k, 0, 0)),
                  pl.BlockSpec((gw2, d), lambda i, k: (k, 0)),
                  vec_spec, vec_spec],
        out_specs=tuple(out_specs),
        scratch_shapes=[pltpu.VMEM((tm, d), F32), pltpu.VMEM((tm, 1), F32), pltpu.VMEM((tm, 1), F32)],
        compiler_params=_cparams(("parallel", "arbitrary")),
        name="gmlp_gate",
    )(x, gate, uv, uv, uv, v_ln_g.reshape(1, gdim), v_ln_b.reshape(1, gdim), w_s,
      b_s.reshape(GMLP_GROUPS, CHUNK, 1), w_out, ln_g.reshape(1, d), ln_b.reshape(1, d))


def _compress_kernel(r_ref, pe_ref, w1_ref, w2_ref, o_ref, *, nseg):
    half = CMP_STRIDE * HEAD_DIM
    a = jnp.concatenate(
        [r_ref[pl.ds(r, nseg, stride=CMP_STRIDE), :].astype(BF16) for r in range(CMP_STRIDE)], axis=1)
    p0 = _dot(a, w1_ref[0, 0:half, :])
    p1 = _dot(a, w1_ref[0, half:2 * half, :])
    pe = jnp.broadcast_to(pe_ref[0], (8, 2 * half)).astype(BF16)
    peh = _dot(pe, w1_ref[0])[0:1]
    hid = peh + p0 + pltpu.roll(p1, nseg - 1, 0)
    o_ref[0, 0, 0] = _dot(jax.nn.gelu(hid).astype(BF16), w2_ref[0])


def _compress(rows2d, n_batch, t, col0, pe, w1, w2):
    nseg = t // CMP_STRIDE
    return pl.pallas_call(
        functools.partial(_compress_kernel, nseg=nseg),
        out_shape=jax.ShapeDtypeStruct((n_batch, 2, N_KV, nseg, HEAD_DIM), F32),
        grid=(n_batch, 2, N_KV),
        in_specs=[pl.BlockSpec((t, HEAD_DIM), lambda b, kv, g: (b, col0 + kv * N_KV + g)),
                  pl.BlockSpec((1, 1, CMP_LEN * HEAD_DIM), lambda b, kv, g: (kv, 0, 0)),
                  pl.BlockSpec((1, CMP_LEN * HEAD_DIM, w1.shape[2]), lambda b, kv, g: (kv, 0, 0)),
                  pl.BlockSpec((1, w2.shape[1], HEAD_DIM), lambda b, kv, g: (kv, 0, 0))],
        out_specs=pl.BlockSpec((1, 1, 1, nseg, HEAD_DIM), lambda b, kv, g: (b, kv, g, 0, 0)),
        compiler_params=_cparams(("parallel", "parallel", "parallel")),
        name="compress",
    )(rows2d, pe.reshape(2, 1, CMP_LEN * HEAD_DIM), w1, w2)


def _softmax_update_t(state, s, vt):
    m, l, acc = state
    m_new = jnp.maximum(m, jnp.max(s, axis=0, keepdims=True))
    alpha = jnp.exp(m - m_new)
    p = jnp.exp(s - m_new)
    l = alpha * l + jnp.sum(p, axis=0, keepdims=True)
    acc = alpha * acc + _dot(vt, p.astype(BF16))
    return m_new, l, acc


def _nsa_attn_kernel(q_ref, kc_ref, vc_ref, band_ref, imap_ref, ks_ref, vs_ref, kw_ref, vw_ref,
                     bsel_ref, bwin_ref, gate_ref, o_ref, ksa, vst, kwa, vwt, *, nslc, nsel):
    qb = pl.program_id(2)
    tq = Q_BLOCK
    cols = HPG * tq
    t = ks_ref.shape[0]
    nseg = kc_ref.shape[3]

    @pl.when(qb == 0)
    def _():
        lane_p = lax.broadcasted_iota(jnp.int32, (KEY_PAD, HEAD_DIM), 1)
        pad_mark = jnp.where(lane_p == HEAD_DIM - 1, 1.0, 0.0).astype(BF16)
        key_blk = lax.broadcasted_iota(jnp.int32, (t, HEAD_DIM), 0) // SLC_LEN
        lane = lax.broadcasted_iota(jnp.int32, (t, HEAD_DIM), 1)
        for ref in (ksa, kwa):
            ref[0:KEY_PAD, 0:HEAD_DIM] = jnp.zeros((KEY_PAD, HEAD_DIM), BF16)
            ref[0:KEY_PAD, HEAD_DIM:2 * HEAD_DIM] = pad_mark
        ksa[KEY_PAD:KEY_PAD + t, 0:HEAD_DIM] = ks_ref[...].astype(BF16)
        ksa[KEY_PAD:KEY_PAD + t, HEAD_DIM:2 * HEAD_DIM] = jnp.where(lane == key_blk, 1.0, 0.0).astype(BF16)
        kwa[KEY_PAD:KEY_PAD + t, 0:HEAD_DIM] = kw_ref[...].astype(BF16)
        kwa[KEY_PAD:KEY_PAD + t, HEAD_DIM:2 * HEAD_DIM] = jnp.zeros((t, HEAD_DIM), BF16)
        vst[:, 0:KEY_PAD] = jnp.zeros((HEAD_DIM, KEY_PAD), BF16)
        vwt[:, 0:KEY_PAD] = jnp.zeros((HEAD_DIM, KEY_PAD), BF16)
        for c in range(t // tq):
            vst[:, KEY_PAD + c * tq:KEY_PAD + (c + 1) * tq] = vs_ref[c * tq:(c + 1) * tq, :].T.astype(BF16)
            vwt[:, KEY_PAD + c * tq:KEY_PAD + (c + 1) * tq] = vw_ref[c * tq:(c + 1) * tq, :].T.astype(BF16)

    q_t = jnp.concatenate(
        [(q_ref[:, h * HEAD_DIM:(h + 1) * HEAD_DIM] * ATT_SCALE).T for h in range(HPG)], axis=1).astype(BF16)

    off = pl.multiple_of(nseg - (tq // CMP_STRIDE) * qb, 8)
    bias = jnp.concatenate([band_ref[h, pl.ds(off, nseg), :] for h in range(HPG)], axis=1)
    valid = bias > 0.5 * NEG_BIAS
    s = _dot(kc_ref[0, 0, 0].astype(BF16), q_t) + bias
    m = jnp.max(s, axis=0, keepdims=True)
    e = jnp.where(valid, jnp.exp(s - m), 0.0)
    den = jnp.sum(e, axis=0, keepdims=True)
    p = e / jnp.where(den > 0.0, den, 1.0)
    o_c = _dot(vc_ref[0, 0, 0].T.astype(BF16), p.astype(BF16))

    psum = p[:, 0:tq]
    for h in range(1, HPG):
        psum = psum + p[:, h * tq:(h + 1) * tq]
    p_hi = psum.astype(BF16)
    p_lo = (psum - p_hi.astype(F32)).astype(BF16)
    imap = imap_ref[...]
    imp = _dot(imap, p_hi) + _dot(imap, p_lo)
    jidx = lax.broadcasted_iota(jnp.int32, (nslc, tq), 0)
    qpos = qb * tq + lax.broadcasted_iota(jnp.int32, (nslc, tq), 1)
    cur = qpos // SLC_LEN
    valid_blk = jidx * SLC_LEN <= qpos
    forced = (jidx == 0) | (jidx == cur) | (jidx == cur - 1)
    score = jnp.where(valid_blk, imp + jnp.where(forced, FORCE_BONUS, 0.0), -jnp.inf)
    cnt = jnp.zeros((nslc, tq), jnp.int32)
    for jp in range(nslc):
        row = score[jp:jp + 1, :]
        ahead = (row > score) | ((row == score) & (jidx > jp))
        cnt = cnt + jnp.where(ahead, 1, 0)
    unsel = jnp.where((cnt < nsel) & valid_blk, 0.0, -SEL_MASK)

    marker_row = lax.broadcasted_iota(jnp.int32, (8, cols), 0) == 7
    aug = jnp.concatenate([jnp.concatenate([unsel] * HPG, axis=1),
                           jnp.zeros((HEAD_DIM - nslc - 8, cols), F32),
                           jnp.where(marker_row, -SEL_MASK, 0.0)], axis=0)
    rhs = jnp.concatenate([q_t, aug.astype(BF16)], axis=0)

    last = pl.multiple_of((qb + 1) * tq, tq)
    s = _dot(ksa[pl.ds(last, SEL_CHUNK), :], rhs) + bsel_ref[0]
    m = jnp.max(s, axis=0, keepdims=True)
    p = jnp.exp(s - m)
    st = (m, jnp.sum(p, axis=0, keepdims=True), _dot(vst[:, pl.ds(last, SEL_CHUNK)], p.astype(BF16)))

    def far_body(c, state):
        start = pl.multiple_of(last - (c + 1) * SEL_CHUNK, tq)
        return _softmax_update_t(state, _dot(ksa[pl.ds(start, SEL_CHUNK), :], rhs), vst[:, pl.ds(start, SEL_CHUNK)])

    _, l_s, acc_s = lax.fori_loop(0, qb // (SEL_CHUNK // tq), far_body, st)
    o_s = acc_s / l_s

    first = pl.multiple_of(qb * tq, tq)
    s = _dot(kwa[pl.ds(first, WINDOW + tq), :], rhs) + bwin_ref[0]
    p = jnp.exp(s - jnp.max(s, axis=0, keepdims=True))
    o_w = _dot(vwt[:, pl.ds(first, WINDOW + tq)], p.astype(BF16)) / jnp.sum(p, axis=0, keepdims=True)

    gates_t = jax.nn.sigmoid(gate_ref[...]).T
    outs = []
    for h in range(HPG):
        hs = slice(h * tq, (h + 1) * tq)
        merged = (gates_t[h:h + 1, :] * o_c[:, hs] + gates_t[HPG + h:HPG + h + 1, :] * o_s[:, hs]
                  + gates_t[2 * HPG + h:2 * HPG + h + 1, :] * o_w[:, hs])
        outs.append(merged.T)
    o_ref[...] = jnp.concatenate(outs, axis=1).astype(o_ref.dtype)


def _nsa_attn(proj, kvc, band_t, imap_t, bias_sel, bias_win, n_batch, t):
    nqb = t // Q_BLOCK
    nseg = t // CMP_STRIDE
    nslc = t // SLC_LEN
    nsel = min(N_SEL, nslc)
    d_q = N_HEADS * HEAD_DIM
    gcols = HPG * HEAD_DIM
    kv0 = d_q // HEAD_DIM
    gate0 = kv0 + 6 * N_KV

    def kv_spec(branch, kv):
        return pl.BlockSpec((t, HEAD_DIM), lambda b, g, i: (b, kv0 + (branch * 2 + kv) * N_KV + g))

    return pl.pallas_call(
        functools.partial(_nsa_attn_kernel, nslc=nslc, nsel=nsel),
        out_shape=jax.ShapeDtypeStruct((n_batch * t, d_q), BF16),
        grid=(n_batch, N_KV, nqb),
        in_specs=[pl.BlockSpec((Q_BLOCK, gcols), lambda b, g, i: (b * nqb + i, g)),
                  pl.BlockSpec((1, 1, 1, nseg, HEAD_DIM), lambda b, g, i: (b, 0, g, 0, 0)),
                  pl.BlockSpec((1, 1, 1, nseg, HEAD_DIM), lambda b, g, i: (b, 1, g, 0, 0)),
                  pl.BlockSpec((HPG, 2 * nseg, Q_BLOCK), lambda b, g, i: (g, 0, 0)),
                  pl.BlockSpec((nslc, nseg), lambda b, g, i: (0, 0)),
                  kv_spec(1, 0), kv_spec(1, 1), kv_spec(2, 0), kv_spec(2, 1),
                  pl.BlockSpec((1, SEL_CHUNK, HPG * Q_BLOCK), lambda b, g, i: (g, 0, 0)),
                  pl.BlockSpec((1, WINDOW + Q_BLOCK, HPG * Q_BLOCK), lambda b, g, i: (g, 0, 0)),
                  pl.BlockSpec((Q_BLOCK, HEAD_DIM), lambda b, g, i: (b * nqb + i, gate0 + g))],
        out_specs=pl.BlockSpec((Q_BLOCK, gcols), lambda b, g, i: (b * nqb + i, g)),
        scratch_shapes=[pltpu.VMEM((KEY_PAD + t, 2 * HEAD_DIM), BF16), pltpu.VMEM((HEAD_DIM, KEY_PAD + t), BF16),
                        pltpu.VMEM((KEY_PAD + t, 2 * HEAD_DIM), BF16), pltpu.VMEM((HEAD_DIM, KEY_PAD + t), BF16)],
        compiler_params=_cparams(("parallel", "parallel", "arbitrary")),
        name="nsa_attn",
    )(proj, kvc, kvc, band_t, imap_t, proj, proj, proj, proj, bias_sel, bias_win, proj)


def _proj_res_kernel(a_ref, w_ref, x_ref, gt_ref, lg_ref, lb_ref, o_ref):
    y = _dot(a_ref[...], w_ref[...])
    z = DN_ALPHA * x_ref[...] + (1.0 + gt_ref[0]) * y
    o_ref[...] = _ln_rows(z, lg_ref[...], lb_ref[...])


def _proj_res(a, w, x, gate, ln_g, ln_b, *, rows_per_batch, tm):
    m, d = x.shape
    k = a.shape[1]
    r = gate.shape[1]
    tpb = rows_per_batch // tm
    vec_spec = pl.BlockSpec((1, d), lambda i: (0, 0))
    return pl.pallas_call(
        _proj_res_kernel,
        out_shape=jax.ShapeDtypeStruct((m, d), F32),
        grid=(m // tm,),
        in_specs=[pl.BlockSpec((tm, k), lambda i: (i, 0)),
                  pl.BlockSpec((k, d), lambda i: (0, 0)),
                  pl.BlockSpec((tm, d), lambda i: (i, 0)),
                  pl.BlockSpec((1, r, d), lambda i: (i // tpb, 0, 0)),
                  vec_spec, vec_spec],
        out_specs=pl.BlockSpec((tm, d), lambda i: (i, 0)),
        compiler_params=_cparams(("parallel",)),
        name="proj_res",
    )(a, w, x, gate, ln_g.reshape(1, d), ln_b.reshape(1, d))


def _rel_bucket_np(dist):
    dist = np.maximum(dist, 0)
    exact = REL_BUCKETS // 2
    logv = (np.log(np.maximum(dist, 1).astype(np.float32) / np.float32(exact))
            / np.float32(math.log(REL_MAX_DIST / exact))).astype(np.float32)
    large = exact + (logv * np.float32(REL_BUCKETS - exact)).astype(np.int32)
    return np.where(dist < exact, dist, np.minimum(large, REL_BUCKETS - 1)).astype(np.int32)


def _cmp_to_slc_np(nseg, nslc):
    i = np.arange(nseg)[:, None] * CMP_STRIDE
    j = np.arange(nslc)[None, :] * SLC_LEN
    ov = np.minimum(i + CMP_LEN, j + SLC_LEN) - np.maximum(i, j)
    return (np.maximum(ov, 0) / CMP_STRIDE).astype(np.float32)


def _bias_tables(rel_table, t):
    tbl_c = rel_table.T - rel_table[REL_BUCKETS - 1][:, None]
    nseg = t // CMP_STRIDE
    ql = np.arange(Q_BLOCK)[None, :]
    dist_c = ql - ((np.arange(2 * nseg)[:, None] - nseg) * CMP_STRIDE + CMP_LEN - 1)
    band_t = jnp.where(dist_c >= 0, jnp.take(tbl_c, _rel_bucket_np(dist_c), axis=1), NEG_BIAS)

    def per_group(dist, ok):
        tab = jnp.where(ok, jnp.take(tbl_c, _rel_bucket_np(dist), axis=1), NEG_MASK)
        n_keys = dist.shape[0]
        return tab.reshape(N_KV, HPG, n_keys, Q_BLOCK).transpose(0, 2, 1, 3).reshape(N_KV, n_keys, HPG * Q_BLOCK)

    dist_s = ql + (SEL_CHUNK - Q_BLOCK) - np.arange(SEL_CHUNK)[:, None]
    dist_w = ql + WINDOW - np.arange(WINDOW + Q_BLOCK)[:, None]
    return band_t, per_group(dist_s, dist_s >= 0), per_group(dist_w, (dist_w >= 0) & (dist_w < WINDOW))


CMP_PAGES_PER_STEP = 16


def _compress_paged_kernel(pt_ref, *refs, n_pg):
    pages = refs[:n_pg]
    pe_ref, w1a_ref, w1b_ref, w2_ref, o_ref, carry_ref = refs[n_pg:]
    n_c = 2 * N_KV
    seg_rows = CMP_STRIDE * n_c
    segs = PAGE_SIZE // CMP_STRIDE
    rows = n_pg * segs * n_c
    hid_w = w2_ref.shape[0]

    @pl.when(pl.program_id(1) == 0)
    def _():
        carry_ref[...] = jnp.zeros_like(carry_ref)

    a = jnp.concatenate(
        [jnp.concatenate(
            [jnp.concatenate([pg[0, 0, seg_rows * s + n_c * r:seg_rows * s + n_c * (r + 1), :]
                              for s in range(segs)], axis=0)
             for r in range(CMP_STRIDE)], axis=1)
         for pg in pages], axis=0).astype(BF16)
    is_k = lax.broadcasted_iota(jnp.int32, (rows, 1), 0) % n_c < N_KV
    is_k8 = lax.broadcasted_iota(jnp.int32, (n_c, 1), 0) < N_KV

    def pick(y, width, k_rows):
        return jnp.where(k_rows, y[:, 0:width], y[:, width:2 * width])

    p0 = pick(_dot(a, w1a_ref[...]), hid_w, is_k)
    p1 = pick(_dot(a, w1b_ref[...]), hid_w, is_k)
    half = CMP_STRIDE * HEAD_DIM
    pe = pe_ref[...].astype(BF16)
    peh = pick(_dot(pe[:, 0:half], w1a_ref[...]) + _dot(pe[:, half:2 * half], w1b_ref[...]), hid_w, is_k8)
    shifted = jnp.concatenate([carry_ref[...], p0[0:rows - n_c]], axis=0)
    carry_ref[...] = p0[rows - n_c:rows]
    hid = (shifted + p1).reshape(rows // n_c, n_c, hid_w) + peh[None]
    y = jax.nn.gelu(hid.reshape(rows, hid_w)).astype(BF16)
    o_ref[0] = pick(_dot(y, w2_ref[...]), HEAD_DIM, is_k)


def _compress_paged(cache_pages, page_table, page0, pe, w1, w2):
    n_s, n_pages = page_table.shape
    n_pg = min(CMP_PAGES_PER_STEP, n_pages)
    n_c = 2 * N_KV
    half = CMP_STRIDE * HEAD_DIM
    rows_pg = PAGE_SIZE * n_c
    out_rows = n_pg * (PAGE_SIZE // CMP_STRIDE) * n_c
    hid_w = w1.shape[2]
    w1a = jnp.concatenate([w1[0, :half], w1[1, :half]], axis=1)
    w1b = jnp.concatenate([w1[0, half:], w1[1, half:]], axis=1)
    w2c = jnp.concatenate([w2[0], w2[1]], axis=1)
    pe8 = jnp.repeat(pe.reshape(2, 2 * half), N_KV, axis=0)

    def page_spec(p):
        return pl.BlockSpec((1, 1, rows_pg, HEAD_DIM), lambda b, c, pt: (0, page0 + pt[b, c * n_pg + p], 0, 0))

    const = lambda b, c, pt: (0, 0)
    grid_spec = pltpu.PrefetchScalarGridSpec(
        num_scalar_prefetch=1,
        grid=(n_s, n_pages // n_pg),
        in_specs=[page_spec(p) for p in range(n_pg)] + [
            pl.BlockSpec((n_c, 2 * half), const),
            pl.BlockSpec((half, 2 * hid_w), const),
            pl.BlockSpec((half, 2 * hid_w), const),
            pl.BlockSpec((hid_w, 2 * HEAD_DIM), const)],
        out_specs=pl.BlockSpec((1, out_rows, HEAD_DIM), lambda b, c, pt: (b, c, 0)),
        scratch_shapes=[pltpu.VMEM((n_c, hid_w), F32)])
    return pl.pallas_call(
        functools.partial(_compress_paged_kernel, n_pg=n_pg),
        out_shape=jax.ShapeDtypeStruct((n_s, n_pages * (PAGE_SIZE // CMP_STRIDE) * n_c, HEAD_DIM), F32),
        grid_spec=grid_spec,
        compiler_params=_cparams(("parallel", "arbitrary")),
        name="compress_paged",
    )(page_table, *([cache_pages[None]] * n_pg), pe8, w1a, w1b, w2c)


def _sample_cmp_kernel(q_ref, kv_ref, bias_ref, im_ref, oc_ref, idx_ref, *, nslc, nsel, pos):
    n_c = 2 * N_KV
    q = (q_ref[0] * ATT_SCALE).astype(BF16)
    kv = kv_ref[0].astype(BF16)
    bias = bias_ref[...]
    valid = bias > 0.5 * NEG_BIAS
    s = _dot_nt(q, kv) + bias
    m = jnp.max(s, axis=-1, keepdims=True)
    e = jnp.where(valid, jnp.exp(s - m), 0.0)
    den = jnp.sum(e, axis=-1, keepdims=True)
    p = e / jnp.where(den > 0.0, den, 1.0)
    oc_ref[0] = _dot(pltpu.roll(p, N_KV, 1).astype(BF16), kv)

    p_hi = p.astype(BF16)
    p_lo = (p - p_hi.astype(F32)).astype(BF16)
    imp_h = _dot(p_hi, im_ref[...]) + _dot(p_lo, im_ref[...])
    lanes = im_ref.shape[1]
    head_grp = lax.broadcasted_iota(jnp.int32, (N_HEADS, lanes), 0) // HPG
    row8 = lax.broadcasted_iota(jnp.int32, (n_c, lanes), 0)
    imp = jnp.zeros((n_c, lanes), F32)
    for g in range(N_KV):
        imp_g = jnp.sum(jnp.where(head_grp == g, imp_h, 0.0), axis=0, keepdims=True)
        imp = imp + jnp.where(row8 == g, imp_g, 0.0)
    jidx = lax.broadcasted_iota(jnp.int32, (n_c, lanes), 1)
    cur = pos // SLC_LEN
    valid_blk = (jidx * SLC_LEN <= pos) & (jidx < nslc)
    forced = (jidx == 0) | (jidx == cur) | (jidx == cur - 1)
    score = jnp.where(valid_blk, imp + jnp.where(forced, FORCE_BONUS, 0.0), -jnp.inf)
    cnt = jnp.zeros((n_c, lanes), jnp.int32)
    for jp in range(nslc):
        col = score[:, jp:jp + 1]
        ahead = (col > score) | ((col == score) & (jidx > jp))
        cnt = cnt + jnp.where(ahead, 1, 0)
    out_lane = lax.broadcasted_iota(jnp.int32, (n_c, HEAD_DIM), 1)
    jf = jidx.astype(F32)
    out = jnp.zeros((n_c, HEAD_DIM), F32)
    for r in range(nsel):
        blk_r = jnp.sum(jnp.where((cnt == r) & valid_blk, jf, 0.0), axis=-1, keepdims=True)
        out = jnp.where(out_lane == r, blk_r, out)
    idx_ref[0] = out.astype(jnp.int32)


def _sample_cmp(q_heads, kvc, bias, imap, *, nslc, nsel, pos):
    n_s, rows, _ = kvc.shape
    lanes = imap.shape[1]
    return pl.pallas_call(
        functools.partial(_sample_cmp_kernel, nslc=nslc, nsel=nsel, pos=pos),
        out_shape=(jax.ShapeDtypeStruct((n_s, N_HEADS, HEAD_DIM), F32),
                   jax.ShapeDtypeStruct((n_s, 2 * N_KV, HEAD_DIM), jnp.int32)),
        grid=(n_s,),
        in_specs=[pl.BlockSpec((1, N_HEADS, HEAD_DIM), lambda b: (b, 0, 0)),
                  pl.BlockSpec((1, rows, HEAD_DIM), lambda b: (b, 0, 0)),
                  pl.BlockSpec((N_HEADS, rows), lambda b: (0, 0)),
                  pl.BlockSpec((rows, lanes), lambda b: (0, 0))],
        out_specs=(pl.BlockSpec((1, N_HEADS, HEAD_DIM), lambda b: (b, 0, 0)),
                   pl.BlockSpec((1, 2 * N_KV, HEAD_DIM), lambda b: (b, 0, 0))),
        compiler_params=_cparams(("parallel",)),
        name="sample_cmp",
    )(q_heads, kvc, bias, imap)


def _sample_sel_win_kernel(idx_ref, pt_ref, *refs, n_slot, pos, n_past_blk, buf_len, thresholds):
    slots = refs[:n_slot]
    q_ref, new_ref, win_ref, tb_ref, oc_ref, gate_ref, o_ref = refs[n_slot:]
    b = pl.program_id(0)
    g = pl.program_id(1)
    n_c = 2 * N_KV
    blk_lanes = SLC_LEN * n_c
    q = (q_ref[0, 0] * ATT_SCALE).astype(BF16)
    tb = tb_ref[0]

    def bias_of(dist):
        d = jnp.maximum(dist, 0)
        out = jnp.zeros(d.shape, F32) + tb[:, 0:1]
        for k in range(1, REL_BUCKETS):
            out = out + jnp.where(d >= thresholds[k], tb[:, k:k + 1] - tb[:, k - 1:k], 0.0)
        return out

    def softmax(s, ok):
        s = jnp.where(ok, s, NEG_BIAS)
        m = jnp.max(s, axis=-1, keepdims=True)
        e = jnp.where(ok, jnp.exp(s - m), 0.0)
        den = jnp.sum(e, axis=-1, keepdims=True)
        return e / jnp.where(den > 0.0, den, 1.0)

    new_rows = new_ref[0, 0]
    s_new = _dot_nt(q, new_rows.astype(BF16))
    lane_new = lax.broadcasted_iota(jnp.int32, s_new.shape, 1)
    bias_new = bias_of(jnp.zeros(s_new.shape, jnp.int32))

    lane = lax.broadcasted_iota(jnp.int32, (n_c, blk_lanes), 1)
    mine = lane % n_c == g
    r_in = lane // n_c
    base = (b * N_KV + g) * n_slot
    s_parts, d_parts, ok_parts = [], [], []
    n_new = 0
    for i in range(n_slot):
        j = idx_ref[base + i]
        dist = jnp.where(j < n_past_blk, pos - j * SLC_LEN, -1) - r_in
        s_parts.append(_dot_nt(q, slots[i][0, 0].astype(BF16)))
        d_parts.append(dist)
        ok_parts.append(mine & (dist >= 0))
        n_new = n_new + jnp.where(j >= n_past_blk, 1, 0)
    s_all = jnp.concatenate(s_parts + [s_new], axis=1)
    ok_all = jnp.concatenate(ok_parts + [lane_new == jnp.where(n_new > 0, 0, -1)], axis=1)
    b_all = jnp.concatenate([bias_of(jnp.concatenate(d_parts, axis=1)), bias_new], axis=1)
    p = softmax(s_all + b_all, ok_all)
    n_old = n_slot * blk_lanes
    p_v = pltpu.roll(p[:, 0:n_old], N_KV, 1).astype(BF16)
    o_s = p[:, n_old:n_old + 1] * new_rows[1:2, :]
    for i in range(n_slot):
        o_s = o_s + _dot(p_v[:, i * blk_lanes:(i + 1) * blk_lanes], slots[i][0, 0].astype(BF16))

    win = win_ref[0].astype(BF16)
    lane_w = lax.broadcasted_iota(jnp.int32, (n_c, buf_len * n_c), 1)
    dist_w = buf_len - lane_w // n_c
    ok_w = (lane_w % n_c == g) & (dist_w >= 0) & (dist_w < WINDOW)
    s_w = jnp.concatenate([_dot_nt(q, win) + bias_of(dist_w), s_new + bias_new], axis=1)
    p = softmax(s_w, jnp.concatenate([ok_w, lane_new == 2], axis=1))
    n_old = buf_len * n_c
    o_w = _dot(pltpu.roll(p[:, 0:n_old], N_KV, 1).astype(BF16), win) + p[:, n_old + 2:n_old + 3] * new_rows[3:4, :]

    gates = jax.nn.sigmoid(gate_ref[0, 0])
    o_ref[0, 0] = gates[:, 0:1] * oc_ref[0, 0] + gates[:, 1:2] * o_s + gates[:, 2:3] * o_w


def _sample_sel_win(idx, page_table, slc_halves, page0, q_g, new_rows, win_rows, tb, o_c, gates, *, pos, nsel):
    n_s, n_pages = page_table.shape
    n_c = 2 * N_KV
    per_page = PAGE_SIZE // SLC_LEN
    n_past_blk = n_pages * per_page
    buf_len = win_rows.shape[1] // n_c
    blk_rows = SLC_LEN * n_c
    thresholds = tuple(int(np.argmax(_rel_bucket_np(np.arange(4 * REL_MAX_DIST)) >= k)) for k in range(REL_BUCKETS))

    def slot_spec(i):
        def index_map(b, g, idx_ref, pt_ref):
            j = jnp.minimum(idx_ref[(b * N_KV + g) * nsel + i], n_past_blk - 1)
            return (page0 + pt_ref[b, j // per_page], j % per_page, 0, 0)
        return pl.BlockSpec((1, 1, blk_rows, HEAD_DIM), index_map)

    grp = lambda b, g, idx_ref, pt_ref: (b, g, 0, 0)
    grid_spec = pltpu.PrefetchScalarGridSpec(
        num_scalar_prefetch=2,
        grid=(n_s, N_KV),
        in_specs=[slot_spec(i) for i in range(nsel)] + [
            pl.BlockSpec((1, 1, n_c, HEAD_DIM), grp),
            pl.BlockSpec((1, 1, HEAD_DIM, HEAD_DIM), grp),
            pl.BlockSpec((1, buf_len * n_c, HEAD_DIM), lambda b, g, idx_ref, pt_ref: (b, 0, 0)),
            pl.BlockSpec((1, n_c, HEAD_DIM), lambda b, g, idx_ref, pt_ref: (g, 0, 0)),
            pl.BlockSpec((1, 1, n_c, HEAD_DIM), grp),
            pl.BlockSpec((1, 1, n_c, HEAD_DIM), grp)],
        out_specs=pl.BlockSpec((1, 1, n_c, HEAD_DIM), grp))
    return pl.pallas_call(
        functools.partial(_sample_sel_win_kernel, n_slot=nsel, pos=pos, n_past_blk=n_past_blk, buf_len=buf_len,
                          thresholds=thresholds),
        out_shape=jax.ShapeDtypeStruct((n_s, N_KV, n_c, HEAD_DIM), F32),
        grid_spec=grid_spec,
        compiler_params=_cparams(("parallel", "parallel")),
        name="sample_sel_win",
    )(idx, page_table, *([slc_halves] * nsel), q_g, new_rows, win_rows, tb, o_c, gates)


def _sample_nsa(proj_s, cmp_pages, slc_halves, page0, win_buf, page_table, pe, w1, w2, rel_table):
    n_s, n_pages = page_table.shape
    past = n_pages * PAGE_SIZE
    pos = past
    n_c = 2 * N_KV
    d_q = N_HEADS * HEAD_DIM
    kvw = n_c * HEAD_DIM
    nseg = past // CMP_STRIDE
    nslc = -(-(past + 1) // SLC_LEN)
    nsel = min(N_SEL, nslc)
    kvc = _compress_paged(cmp_pages, page_table, page0, pe, w1, w2)
    x = np.arange(nseg)[:, None]
    c = np.arange(n_c)[None, :]
    dist = pos - ((x - 1) * CMP_STRIDE + CMP_LEN - 1)
    head_grp = np.arange(N_HEADS)[:, None, None] // HPG
    ok = ((x >= 1) & (dist >= 0))[None] & (c[None] == head_grp)
    bias = jnp.where(ok, jnp.take(rel_table.T, _rel_bucket_np(np.broadcast_to(dist, (nseg, n_c))), axis=1),
                     NEG_BIAS).reshape(N_HEADS, nseg * n_c)
    lanes = -(-nslc // HEAD_DIM) * HEAD_DIM
    imap = np.zeros((nseg, n_c, lanes), np.float32)
    imap[1:, :, :nslc] = _cmp_to_slc_np(nseg - 1, nslc)[:, None, :]
    q_heads = proj_s[:, :d_q].reshape(n_s, N_HEADS, HEAD_DIM)
    o_c, idx = _sample_cmp(q_heads, kvc, bias, jnp.asarray(imap.reshape(nseg * n_c, lanes)).astype(BF16),
                           nslc=nslc, nsel=nsel, pos=pos)
    pad_heads = lambda a: jnp.pad(a.reshape(n_s, N_KV, HPG, -1), ((0, 0), (0, 0), (0, n_c - HPG), (0, 0)))
    kv_new = proj_s[:, d_q:d_q + 3 * kvw].reshape(n_s, 3, 2, N_KV, HEAD_DIM)
    new_rows = kv_new[:, 1:3].transpose(0, 3, 1, 2, 4).reshape(n_s, N_KV, 4, HEAD_DIM)
    new_rows = jnp.pad(new_rows, ((0, 0), (0, 0), (0, HEAD_DIM - 4), (0, 0)))
    gates = proj_s[:, d_q + 3 * kvw:].reshape(n_s, N_KV, HEAD_DIM)[:, :, :3 * HPG].reshape(n_s, N_KV, 3, HPG)
    gates = jnp.pad(gates.transpose(0, 1, 3, 2), ((0, 0), (0, 0), (0, n_c - HPG), (0, HEAD_DIM - 3)))
    tb = jnp.pad(rel_table.T.reshape(N_KV, HPG, REL_BUCKETS), ((0, 0), (0, n_c - HPG), (0, HEAD_DIM - REL_BUCKETS)))
    o = _sample_sel_win(idx[:, :N_KV, :nsel].reshape(-1), page_table, slc_halves, page0,
                        pad_heads(proj_s[:, :d_q]), new_rows, win_buf.reshape(n_s, -1, HEAD_DIM), tb,
                        pad_heads(o_c), gates, pos=pos, nsel=nsel)
    new_win = jnp.concatenate([win_buf, kv_new[:, 2][:, None]], axis=1)[:, -min(WINDOW, win_buf.shape[1] + 1):]
    return o[:, :, :HPG].reshape(n_s, d_q), kv_new, new_win


def _pad_rows(a, rows):
    return jnp.pad(a, ((0, rows - a.shape[0]),) + ((0, 0),) * (a.ndim - 1))


def _nsa_w_in_layout(w):
    d = w.shape[0]
    n_main = N_HEADS * HEAD_DIM + 6 * N_KV * HEAD_DIM
    wg = w[:, n_main:].reshape(d, 3, N_KV, HPG).transpose(0, 2, 1, 3).reshape(d, N_KV, 3 * HPG)
    wg = jnp.pad(wg, ((0, 0), (0, 0), (0, HEAD_DIM - 3 * HPG))).reshape(d, N_KV * HEAD_DIM)
    return jnp.concatenate([w[:, :n_main], wg], axis=1).astype(BF16)


def kernel(x_prompt, x_sample, cache_cmp_kv, cache_slc_kv, state_win_kv, page_table, c_prompt, c_sample,
           ada_w, ada_b, ln_g, ln_b, ffn_pre_w_in, ffn_pre_w_out, ffn_post_w_in, ffn_post_w_out,
           gmlp_w_in, gmlp_ln_g, gmlp_ln_b, gmlp_w_s, gmlp_b_s, gmlp_w_out,
           nsa_w_in, nsa_cmp_pe, nsa_cmp_w1, nsa_cmp_w2, nsa_w_out, rel_table):
    n_b, t, d = x_prompt.shape
    n_s = x_sample.shape[0]
    n_phys = cache_cmp_kv.shape[1]
    d_q = N_HEADS * HEAD_DIM
    kvw = 2 * N_KV * HEAD_DIM
    tm_p = min(512, t)
    tm_mm = min(1024, t)
    tm_g = min(256, t)

    xp = x_prompt.reshape(n_b * t, d)
    xs = _pad_rows(x_sample.reshape(n_s, d), SAMPLE_ROWS)
    c_all = _pad_rows(jnp.concatenate([c_prompt, c_sample], axis=0), SAMPLE_ROWS)
    mod = _ada(c_all, ada_w, ada_b)
    band_t, bias_sel, bias_win = _bias_tables(rel_table, t)
    imap_t = jnp.asarray(_cmp_to_slc_np(t // CMP_STRIDE, t // SLC_LEN).T).astype(BF16)
    cmp_pages = cache_cmp_kv.reshape(-1, PAGE_SIZE * 2 * N_KV, HEAD_DIM)
    slc_halves = cache_slc_kv.reshape(-1, PAGE_SIZE // SLC_LEN, SLC_LEN * 2 * N_KV, HEAD_DIM)

    cmp_p, cmp_s, slc_p, slc_s, win_p, win_s, gv_s = [], [], [], [], [], [], []
    for i in range(DEPTH):
        mi = mod[i].reshape(SAMPLE_ROWS, N_ADA, d)
        mp = [mi[:n_b, k][:, None, :] for k in range(N_ADA)]
        ms = [_pad_rows(mi[n_b:n_b + n_s, k], SAMPLE_ROWS)[None] for k in range(N_ADA)]

        w_in = ffn_pre_w_in[i].astype(BF16)
        w_out = ffn_pre_w_out[i].astype(BF16)
        xp = _ffn(xp, mp[0], mp[1], mp[2], w_in, w_out, ln_g[i, 0], ln_b[i, 0], rows_per_batch=t, tm=tm_p)
        xs = _ffn(xs, ms[0], ms[1], ms[2], w_in, w_out, ln_g[i, 0], ln_b[i, 0],
                  rows_per_batch=SAMPLE_ROWS, tm=SAMPLE_ROWS)

        if i % 2 == 0:
            a = i // 2
            gw_in = gmlp_w_in[a].astype(BF16)
            gw_out = gmlp_w_out[a].astype(BF16)
            uv_p = _mod_mm(xp, mp[3], mp[4], gw_in, rows_per_batch=t, tm=tm_mm, tn=1024, act="gelu")
            xp, = _gmlp_gate(xp, mp[5], uv_p, gmlp_ln_g[a], gmlp_ln_b[a], gmlp_w_s[a], gmlp_b_s[a], gw_out,
                             ln_g[i, 1], ln_b[i, 1], rows_per_batch=t, tm=tm_g, chunk=min(CHUNK, t),
                             emit_vn=False)
            uv_s = _mod_mm(xs, ms[3], ms[4], gw_in, rows_per_batch=SAMPLE_ROWS, tm=SAMPLE_ROWS, tn=1024,
                           act="gelu")
            xs, vn_s = _gmlp_gate(xs, ms[5], uv_s, gmlp_ln_g[a], gmlp_ln_b[a], gmlp_w_s[a], gmlp_b_s[a], gw_out,
                                  ln_g[i, 1], ln_b[i, 1], rows_per_batch=SAMPLE_ROWS, tm=SAMPLE_ROWS, chunk=1,
                                  emit_vn=True)
            gv_s.append(vn_s[:n_s].reshape(n_s, 1, -1))
        else:
            a = i // 2
            nw_in = _nsa_w_in_layout(nsa_w_in[a])
            nw_out = nsa_w_out[a].astype(BF16)
            w1 = nsa_cmp_w1[a].astype(BF16)
            w2 = nsa_cmp_w2[a].astype(BF16)
            proj = _mod_mm(xp, mp[3], mp[4], nw_in, rows_per_batch=t, tm=tm_mm, tn=512)
            kvc = _compress(proj, n_b, t, d_q // HEAD_DIM, nsa_cmp_pe[a], w1, w2)
            o = _nsa_attn(proj, kvc, band_t, imap_t, bias_sel, bias_win, n_b, t)
            xp = _proj_res(o, nw_out, xp, mp[5], ln_g[i, 1], ln_b[i, 1], rows_per_batch=t, tm=tm_p)
            kv_all = proj[:, d_q:d_q + 3 * kvw].reshape(n_b, t, 3, 2, N_KV, HEAD_DIM)
            cmp_p.append(kv_all[:, :, 0])
            slc_p.append(kv_all[:, :, 1])
            win_p.append(kv_all[:, -min(WINDOW, t):, 2])
            proj_s = _mod_mm(xs, ms[3], ms[4], nw_in, rows_per_batch=SAMPLE_ROWS, tm=SAMPLE_ROWS, tn=512)[:n_s]
            o_samp, kv_new, new_win = _sample_nsa(proj_s, cmp_pages, slc_halves, a * n_phys, state_win_kv[a],
                                                  page_table, nsa_cmp_pe[a], w1, w2, rel_table)
            xs = _proj_res(_pad_rows(o_samp, SAMPLE_ROWS).astype(BF16), nw_out, xs, ms[5], ln_g[i, 1], ln_b[i, 1],
                           rows_per_batch=SAMPLE_ROWS, tm=SAMPLE_ROWS)
            cmp_s.append(kv_new[:, 0][:, None])
            slc_s.append(kv_new[:, 1][:, None])
            win_s.append(new_win)

        w_in = ffn_post_w_in[i].astype(BF16)
        w_out = ffn_post_w_out[i].astype(BF16)
        xp = _ffn(xp, mp[6], mp[7], mp[8], w_in, w_out, ln_g[i, 2], ln_b[i, 2], rows_per_batch=t, tm=tm_p)
        xs = _ffn(xs, ms[6], ms[7], ms[8], w_in, w_out, ln_g[i, 2], ln_b[i, 2],
                  rows_per_batch=SAMPLE_ROWS, tm=SAMPLE_ROWS)

    return (xp.reshape(n_b, t, d), xs[:n_s].reshape(n_s, 1, d),
            jnp.stack(cmp_p), jnp.stack(cmp_s), jnp.stack(slc_p), jnp.stack(slc_s),
            jnp.stack(win_p), jnp.stack(win_s), jnp.stack(gv_s))
```

```python
import functools
import math

import numpy as np
import jax
import jax.numpy as jnp
from jax import lax
from jax.experimental import pallas as pl
from jax.experimental.pallas import tpu as pltpu

F32 = jnp.float32
BF16 = jnp.bfloat16

DEPTH = 4
N_ADA = 9
N_HEADS = 16
HEAD_DIM = 128
N_KV = 4
HPG = N_HEADS // N_KV
GMLP_GROUPS = 16
CHUNK = 128
PAGE_SIZE = 128
CMP_LEN = 32
CMP_STRIDE = 16
SLC_LEN = 64
N_SEL = 16
WINDOW = 512
Q_BLOCK = 128
REL_BUCKETS = 32
REL_MAX_DIST = 128
LN_EPS = 1e-5
DN_ALPHA = (2 * DEPTH) ** 0.25
FORCE_BONUS = 1e4
ATT_SCALE = HEAD_DIM ** -0.5

SAMPLE_ROWS = 16
NEG_MASK = -30000.0
NEG_BIAS = -1e30
SEL_MASK = 32768.0
KEY_PAD = WINDOW
SEL_CHUNK = 512
VMEM_LIMIT = 56 * 1024 * 1024


def _cparams(sem):
    return pltpu.CompilerParams(dimension_semantics=sem, vmem_limit_bytes=VMEM_LIMIT)


def _dot(a, b):
    return jnp.dot(a, b, preferred_element_type=F32)


def _dot_nt(a, b):
    return lax.dot_general(a, b, (((1,), (1,)), ((), ())), preferred_element_type=F32)


def _ln_rows(z, g, b):
    mu = jnp.mean(z, axis=-1, keepdims=True)
    zc = z - mu
    var = jnp.mean(zc * zc, axis=-1, keepdims=True)
    return zc * lax.rsqrt(var + LN_EPS) * g + b


def _silu(x):
    return x * jax.nn.sigmoid(x)


def _ada_kernel(c_ref, w_ref, b_ref, o_ref):
    h = _silu(c_ref[...]).astype(BF16)
    o_ref[0] = _dot(h, w_ref[0].astype(BF16)) + b_ref[0]


def _ada(c_all, ada_w, ada_b, tn=1024):
    depth, d, n = ada_w.shape
    r = c_all.shape[0]
    return pl.pallas_call(
        _ada_kernel,
        out_shape=jax.ShapeDtypeStruct((depth, r, n), F32),
        grid=(depth, n // tn),
        in_specs=[pl.BlockSpec((r, d), lambda l, j: (0, 0)),
                  pl.BlockSpec((1, d, tn), lambda l, j: (l, 0, j)),
                  pl.BlockSpec((1, 1, tn), lambda l, j: (l, 0, j))],
        out_specs=pl.BlockSpec((1, r, tn), lambda l, j: (l, 0, j)),
        compiler_params=_cparams(("parallel", "parallel")),
        name="ada",
    )(c_all, ada_w, ada_b.reshape(depth, 1, n))


def _ffn_kernel(x_ref, sh_ref, sc_ref, gt_ref, wg_ref, wu_ref, wo_ref, lg_ref, lb_ref, o_ref, h_ref):
    j = pl.program_id(1)

    @pl.when(j == 0)
    def _():
        h_ref[...] = (x_ref[...] * (1.0 + sc_ref[0]) + sh_ref[0]).astype(BF16)
        o_ref[...] = jnp.zeros_like(o_ref)

    h = h_ref[...]
    g = _dot(h, wg_ref[...])
    u = _dot(h, wu_ref[...])
    a = (_silu(g) * u).astype(BF16)
    o_ref[...] += _dot(a, wo_ref[...])

    @pl.when(j == pl.num_programs(1) - 1)
    def _():
        z = DN_ALPHA * x_ref[...] + (1.0 + gt_ref[0]) * (0.5 * o_ref[...])
        o_ref[...] = _ln_rows(z, lg_ref[...], lb_ref[...])


def _ffn(x, shift, scale, gate, w_in, w_out, ln_g, ln_b, *, rows_per_batch, tm, tf=512):
    m, d = x.shape
    dff = w_out.shape[0]
    nf = dff // tf
    r = shift.shape[1]
    tpb = rows_per_batch // tm
    mod_spec = pl.BlockSpec((1, r, d), lambda i, j: (i // tpb, 0, 0))
    vec_spec = pl.BlockSpec((1, d), lambda i, j: (0, 0))
    return pl.pallas_call(
        _ffn_kernel,
        out_shape=jax.ShapeDtypeStruct((m, d), F32),
        grid=(m // tm, nf),
        in_specs=[pl.BlockSpec((tm, d), lambda i, j: (i, 0), pipeline_mode=pl.Buffered(1)),
                  mod_spec, mod_spec, mod_spec,
                  pl.BlockSpec((d, tf), lambda i, j: (0, j)),
                  pl.BlockSpec((d, tf), lambda i, j: (0, j + nf)),
                  pl.BlockSpec((tf, d), lambda i, j: (j, 0)),
                  vec_spec, vec_spec],
        out_specs=pl.BlockSpec((tm, d), lambda i, j: (i, 0)),
        scratch_shapes=[pltpu.VMEM((tm, d), BF16)],
        compiler_params=_cparams(("parallel", "arbitrary")),
        name="ffn",
    )(x, shift, scale, gate, w_in, w_in, w_out, ln_g.reshape(1, d), ln_b.reshape(1, d))


def _mm_kernel(x_ref, sh_ref, sc_ref, w_ref, o_ref, h_ref):
    @pl.when(pl.program_id(1) == 0)
    def _():
        h_ref[...] = (x_ref[...] * (1.0 + sc_ref[0]) + sh_ref[0]).astype(BF16)

    o_ref[...] = _dot(h_ref[...], w_ref[...])


def _mod_mm(x, shift, scale, w, *, rows_per_batch, tm, tn):
    m, d = x.shape
    n = w.shape[1]
    r = shift.shape[1]
    tpb = rows_per_batch // tm
    mod_spec = pl.BlockSpec((1, r, d), lambda i, j: (i // tpb, 0, 0))
    return pl.pallas_call(
        _mm_kernel,
        out_shape=jax.ShapeDtypeStruct((m, n), F32),
        grid=(m // tm, n // tn),
        in_specs=[pl.BlockSpec((tm, d), lambda i, j: (i, 0)),
                  mod_spec, mod_spec,
                  pl.BlockSpec((d, tn), lambda i, j: (0, j))],
        out_specs=pl.BlockSpec((tm, tn), lambda i, j: (i, j)),
        scratch_shapes=[pltpu.VMEM((tm, d), BF16)],
        compiler_params=_cparams(("parallel", "arbitrary")),
        name="mod_mm",
    )(x, shift, scale, w)


GMLP_GROUPS_PER_STEP = 2


def _gmlp_kernel(x_ref, sh_ref, sc_ref, gt_ref, win_ref, vlg_ref, vlb_ref, ws_ref, bs_ref,
                 wo_ref, lg_ref, lb_ref, o_ref, *rest, chunk, gw, emit_vn):
    if emit_vn:
        vn_ref, h_ref, v_ref, mu_ref, rs_ref = rest
    else:
        h_ref, v_ref, mu_ref, rs_ref = rest
    k = pl.program_id(1)
    ns = pl.num_programs(1) // 2
    tm = x_ref.shape[0]
    gdim = ns * v_ref.shape[2]

    @pl.when(k == 0)
    def _():
        h_ref[...] = (x_ref[...] * (1.0 + sc_ref[0]) + sh_ref[0]).astype(BF16)
        o_ref[...] = jnp.zeros_like(o_ref)

    y = jax.nn.gelu(_dot(h_ref[...], win_ref[...]))

    @pl.when(k < ns)
    def _():
        v_ref[k] = y

    @pl.when(k == ns)
    def _():
        tot = jnp.zeros((tm, 1), F32)
        for c in range(ns):
            tot = tot + jnp.sum(v_ref[c], axis=-1, keepdims=True)
        mu = tot / gdim
        sq = jnp.zeros((tm, 1), F32)
        for c in range(ns):
            dv = v_ref[c] - mu
            sq = sq + jnp.sum(dv * dv, axis=-1, keepdims=True)
        mu_ref[...] = mu
        rs_ref[...] = lax.rsqrt(sq / gdim + LN_EPS)

    @pl.when(k >= ns)
    def _():
        vn = (v_ref[k - ns] - mu_ref[...]) * rs_ref[...] * vlg_ref[...] + vlb_ref[...]
        if emit_vn:
            vn_ref[...] = vn
        parts = []
        for gi in range(GMLP_GROUPS_PER_STEP):
            vn_g = vn[:, gi * gw:(gi + 1) * gw]
            if chunk == 1:
                s_g = ws_ref[gi, 0:1, 0:1] * vn_g + bs_ref[gi, 0:1, 0:1]
            else:
                row = lax.broadcasted_iota(jnp.int32, (chunk, chunk), 0)
                col = lax.broadcasted_iota(jnp.int32, (chunk, chunk), 1)
                w_tri = jnp.where(col <= row, ws_ref[gi], 0.0).astype(BF16)
                vb = vn_g.astype(BF16)
                s_g = jnp.concatenate(
                    [_dot(w_tri, vb[c * chunk:(c + 1) * chunk]) + bs_ref[gi]
                     for c in range(tm // chunk)], axis=0)
            parts.append(s_g)
        a = (y * jnp.concatenate(parts, axis=1)).astype(BF16)
        o_ref[...] += _dot(a, wo_ref[...])

    @pl.when(k == 2 * ns - 1)
    def _():
        z = DN_ALPHA * x_ref[...] + (1.0 + gt_ref[0]) * o_ref[...]
        o_ref[...] = _ln_rows(z, lg_ref[...], lb_ref[...])


def _gmlp(x, shift, scale, gate, w_in, v_ln_g, v_ln_b, w_s, b_s, w_out, ln_g, ln_b, *, rows_per_batch, tm,
          chunk, emit_vn):
    m, d = x.shape
    gdim = w_out.shape[0]
    gw = gdim // GMLP_GROUPS
    gw2 = gw * GMLP_GROUPS_PER_STEP
    ns = GMLP_GROUPS // GMLP_GROUPS_PER_STEP
    r = gate.shape[1]
    tpb = rows_per_batch // tm
    mod_spec = pl.BlockSpec((1, r, d), lambda i, k: (i // tpb, 0, 0))
    vec_spec = pl.BlockSpec((1, d), lambda i, k: (0, 0))
    second = lambda k: jnp.maximum(k - ns, 0)
    out_shape = [jax.ShapeDtypeStruct((m, d), F32)]
    out_specs = [pl.BlockSpec((tm, d), lambda i, k: (i, 0))]
    if emit_vn:
        out_shape.append(jax.ShapeDtypeStruct((m, gdim), F32))
        out_specs.append(pl.BlockSpec((tm, gw2), lambda i, k: (i, second(k))))
    return pl.pallas_call(
        functools.partial(_gmlp_kernel, chunk=chunk, gw=gw, emit_vn=emit_vn),
        out_shape=tuple(out_shape),
        grid=(m // tm, 2 * ns),
        in_specs=[pl.BlockSpec((tm, d), lambda i, k: (i, 0)),
                  mod_spec, mod_spec, mod_spec,
                  pl.BlockSpec((d, gw2), lambda i, k: (0, jnp.where(k < ns, k + ns, k - ns))),
                  pl.BlockSpec((1, gw2), lambda i, k: (0, second(k))),
                  pl.BlockSpec((1, gw2), lambda i, k: (0, second(k))),
                  pl.BlockSpec((GMLP_GROUPS_PER_STEP, CHUNK, CHUNK), lambda i, k: (second(k), 0, 0)),
                  pl.BlockSpec((GMLP_GROUPS_PER_STEP, CHUNK, 1), lambda i, k: (second(k), 0, 0)),
                  pl.BlockSpec((gw2, d), lambda i, k: (second(k), 0)),
                  vec_spec, vec_spec],
        out_specs=tuple(out_specs),
        scratch_shapes=[pltpu.VMEM((tm, d), BF16), pltpu.VMEM((ns, tm, gw2), F32),
                        pltpu.VMEM((tm, 1), F32), pltpu.VMEM((tm, 1), F32)],
        compiler_params=_cparams(("parallel", "arbitrary")),
        name="gmlp",
    )(x, shift, scale, gate, w_in, v_ln_g.reshape(1, gdim), v_ln_b.reshape(1, gdim), w_s,
      b_s.reshape(GMLP_GROUPS, CHUNK, 1), w_out, ln_g.reshape(1, d), ln_b.reshape(1, d))


def _compress_kernel(r_ref, pe_ref, w1_ref, w2_ref, o_ref, *, nseg):
    half = CMP_STRIDE * HEAD_DIM
    a = jnp.concatenate(
        [r_ref[pl.ds(r, nseg, stride=CMP_STRIDE), :].astype(BF16) for r in range(CMP_STRIDE)], axis=1)
    p0 = _dot(a, w1_ref[0, 0:half, :])
    p1 = _dot(a, w1_ref[0, half:2 * half, :])
    pe = jnp.broadcast_to(pe_ref[0], (8, 2 * half)).astype(BF16)
    peh = _dot(pe, w1_ref[0])[0:1]
    hid = peh + p0 + pltpu.roll(p1, nseg - 1, 0)
    o_ref[0, 0, 0] = _dot(jax.nn.gelu(hid).astype(BF16), w2_ref[0])


def _compress(rows2d, n_batch, t, col0, pe, w1, w2):
    nseg = t // CMP_STRIDE
    return pl.pallas_call(
        functools.partial(_compress_kernel, nseg=nseg),
        out_shape=jax.ShapeDtypeStruct((n_batch, 2, N_KV, nseg, HEAD_DIM), F32),
        grid=(n_batch, 2, N_KV),
        in_specs=[pl.BlockSpec((t, HEAD_DIM), lambda b, kv, g: (b, col0 + kv * N_KV + g)),
                  pl.BlockSpec((1, 1, CMP_LEN * HEAD_DIM), lambda b, kv, g: (kv, 0, 0)),
                  pl.BlockSpec((1, CMP_LEN * HEAD_DIM, w1.shape[2]), lambda b, kv, g: (kv, 0, 0)),
                  pl.BlockSpec((1, w2.shape[1], HEAD_DIM), lambda b, kv, g: (kv, 0, 0))],
        out_specs=pl.BlockSpec((1, 1, 1, nseg, HEAD_DIM), lambda b, kv, g: (b, kv, g, 0, 0)),
        compiler_params=_cparams(("parallel", "parallel", "parallel")),
        name="compress",
    )(rows2d, pe.reshape(2, 1, CMP_LEN * HEAD_DIM), w1, w2)


def _softmax_update_t(state, s, vt):
    m, l, acc = state
    m_new = jnp.maximum(m, jnp.max(s, axis=0, keepdims=True))
    alpha = jnp.exp(m - m_new)
    p = jnp.exp(s - m_new)
    l = alpha * l + jnp.sum(p, axis=0, keepdims=True)
    acc = alpha * acc + _dot(vt, p.astype(BF16))
    return m_new, l, acc


def _nsa_attn_kernel(q_ref, kc_ref, vc_ref, band_ref, imap_ref, ks_ref, vs_ref, kw_ref, vw_ref,
                     bsel_ref, bwin_ref, gate_ref, o_ref, ksa, vst, kwa, vwt, *, nslc, nsel):
    qb = pl.program_id(2)
    tq = Q_BLOCK
    cols = HPG * tq
    t = ks_ref.shape[0]
    nseg = kc_ref.shape[3]

    @pl.when(qb == 0)
    def _():
        lane_p = lax.broadcasted_iota(jnp.int32, (KEY_PAD, HEAD_DIM), 1)
        pad_mark = jnp.where(lane_p == HEAD_DIM - 1, 1.0, 0.0).astype(BF16)
        key_blk = lax.broadcasted_iota(jnp.int32, (t, HEAD_DIM), 0) // SLC_LEN
        lane = lax.broadcasted_iota(jnp.int32, (t, HEAD_DIM), 1)
        for ref in (ksa, kwa):
            ref[0:KEY_PAD, 0:HEAD_DIM] = jnp.zeros((KEY_PAD, HEAD_DIM), BF16)
            ref[0:KEY_PAD, HEAD_DIM:2 * HEAD_DIM] = pad_mark
        ksa[KEY_PAD:KEY_PAD + t, 0:HEAD_DIM] = ks_ref[...].astype(BF16)
        ksa[KEY_PAD:KEY_PAD + t, HEAD_DIM:2 * HEAD_DIM] = jnp.where(lane == key_blk, 1.0, 0.0).astype(BF16)
        kwa[KEY_PAD:KEY_PAD + t, 0:HEAD_DIM] = kw_ref[...].astype(BF16)
        kwa[KEY_PAD:KEY_PAD + t, HEAD_DIM:2 * HEAD_DIM] = jnp.zeros((t, HEAD_DIM), BF16)
        vst[:, 0:KEY_PAD] = jnp.zeros((HEAD_DIM, KEY_PAD), BF16)
        vwt[:, 0:KEY_PAD] = jnp.zeros((HEAD_DIM, KEY_PAD), BF16)
        for c in range(t // tq):
            vst[:, KEY_PAD + c * tq:KEY_PAD + (c + 1) * tq] = vs_ref[c * tq:(c + 1) * tq, :].T.astype(BF16)
            vwt[:, KEY_PAD + c * tq:KEY_PAD + (c + 1) * tq] = vw_ref[c * tq:(c + 1) * tq, :].T.astype(BF16)

    q_t = jnp.concatenate(
        [(q_ref[:, h * HEAD_DIM:(h + 1) * HEAD_DIM] * ATT_SCALE).T for h in range(HPG)], axis=1).astype(BF16)

    off = pl.multiple_of(nseg - (tq // CMP_STRIDE) * qb, 8)
    bias = jnp.concatenate([band_ref[h, pl.ds(off, nseg), :] for h in range(HPG)], axis=1)
    valid = bias > 0.5 * NEG_BIAS
    s = _dot(kc_ref[0, 0, 0].astype(BF16), q_t) + bias
    m = jnp.max(s, axis=0, keepdims=True)
    e = jnp.where(valid, jnp.exp(s - m), 0.0)
    den = jnp.sum(e, axis=0, keepdims=True)
    p = e / jnp.where(den > 0.0, den, 1.0)
    o_c = _dot(vc_ref[0, 0, 0].T.astype(BF16), p.astype(BF16))

    psum = p[:, 0:tq]
    for h in range(1, HPG):
        psum = psum + p[:, h * tq:(h + 1) * tq]
    p_hi = psum.astype(BF16)
    p_lo = (psum - p_hi.astype(F32)).astype(BF16)
    imap = imap_ref[...]
    imp = _dot(imap, p_hi) + _dot(imap, p_lo)
    jidx = lax.broadcasted_iota(jnp.int32, (nslc, tq), 0)
    qpos = qb * tq + lax.broadcasted_iota(jnp.int32, (nslc, tq), 1)
    cur = qpos // SLC_LEN
    valid_blk = jidx * SLC_LEN <= qpos
    forced = (jidx == 0) | (jidx == cur) | (jidx == cur - 1)
    score = jnp.where(valid_blk, imp + jnp.where(forced, FORCE_BONUS, 0.0), -jnp.inf)
    cnt = jnp.zeros((nslc, tq), jnp.int32)
    for jp in range(nslc):
        row = score[jp:jp + 1, :]
        ahead = (row > score) | ((row == score) & (jidx > jp))
        cnt = cnt + jnp.where(ahead, 1, 0)
    unsel = jnp.where((cnt < nsel) & valid_blk, 0.0, -SEL_MASK)

    marker_row = lax.broadcasted_iota(jnp.int32, (8, cols), 0) == 7
    aug = jnp.concatenate([jnp.concatenate([unsel] * HPG, axis=1),
                           jnp.zeros((HEAD_DIM - nslc - 8, cols), F32),
                           jnp.where(marker_row, -SEL_MASK, 0.0)], axis=0)
    rhs = jnp.concatenate([q_t, aug.astype(BF16)], axis=0)

    last = pl.multiple_of((qb + 1) * tq, tq)
    s = _dot(ksa[pl.ds(last, SEL_CHUNK), :], rhs) + bsel_ref[0]
    m = jnp.max(s, axis=0, keepdims=True)
    p = jnp.exp(s - m)
    st = (m, jnp.sum(p, axis=0, keepdims=True), _dot(vst[:, pl.ds(last, SEL_CHUNK)], p.astype(BF16)))

    def far_start(c):
        return pl.multiple_of(jnp.maximum(last - (c + 1) * SEL_CHUNK, 0), tq)

    def far_logits(c):
        return _dot(ksa[pl.ds(far_start(c), SEL_CHUNK), :], rhs)

    def far_body(c, carry):
        state, s_c = carry
        s_next = far_logits(c + 1)
        return _softmax_update_t(state, s_c, vst[:, pl.ds(far_start(c), SEL_CHUNK)]), s_next

    (_, l_s, acc_s), _ = lax.fori_loop(0, qb // (SEL_CHUNK // tq), far_body, (st, far_logits(0)))
    o_s = acc_s / l_s

    first = pl.multiple_of(qb * tq, tq)
    s = _dot(kwa[pl.ds(first, WINDOW + tq), :], rhs) + bwin_ref[0]
    p = jnp.exp(s - jnp.max(s, axis=0, keepdims=True))
    o_w = _dot(vwt[:, pl.ds(first, WINDOW + tq)], p.astype(BF16)) / jnp.sum(p, axis=0, keepdims=True)

    gates_t = jax.nn.sigmoid(gate_ref[...]).T
    outs = []
    for h in range(HPG):
        hs = slice(h * tq, (h + 1) * tq)
        merged = (gates_t[h:h + 1, :] * o_c[:, hs] + gates_t[HPG + h:HPG + h + 1, :] * o_s[:, hs]
                  + gates_t[2 * HPG + h:2 * HPG + h + 1, :] * o_w[:, hs])
        outs.append(merged.T)
    o_ref[...] = jnp.concatenate(outs, axis=1).astype(o_ref.dtype)


def _nsa_attn(proj, kvc, band_t, imap_t, bias_sel, bias_win, n_batch, t):
    nqb = t // Q_BLOCK
    nseg = t // CMP_STRIDE
    nslc = t // SLC_LEN
    nsel = min(N_SEL, nslc)
    d_q = N_HEADS * HEAD_DIM
    gcols = HPG * HEAD_DIM
    kv0 = d_q // HEAD_DIM
    gate0 = kv0 + 6 * N_KV

    def kv_spec(branch, kv):
        return pl.BlockSpec((t, HEAD_DIM), lambda b, g, i: (b, kv0 + (branch * 2 + kv) * N_KV + g))

    return pl.pallas_call(
        functools.partial(_nsa_attn_kernel, nslc=nslc, nsel=nsel),
        out_shape=jax.ShapeDtypeStruct((n_batch * t, d_q), BF16),
        grid=(n_batch, N_KV, nqb),
        in_specs=[pl.BlockSpec((Q_BLOCK, gcols), lambda b, g, i: (b * nqb + i, g)),
                  pl.BlockSpec((1, 1, 1, nseg, HEAD_DIM), lambda b, g, i: (b, 0, g, 0, 0)),
                  pl.BlockSpec((1, 1, 1, nseg, HEAD_DIM), lambda b, g, i: (b, 1, g, 0, 0)),
                  pl.BlockSpec((HPG, 2 * nseg, Q_BLOCK), lambda b, g, i: (g, 0, 0)),
                  pl.BlockSpec((nslc, nseg), lambda b, g, i: (0, 0)),
                  kv_spec(1, 0), kv_spec(1, 1), kv_spec(2, 0), kv_spec(2, 1),
                  pl.BlockSpec((1, SEL_CHUNK, HPG * Q_BLOCK), lambda b, g, i: (g, 0, 0)),
                  pl.BlockSpec((1, WINDOW + Q_BLOCK, HPG * Q_BLOCK), lambda b, g, i: (g, 0, 0)),
                  pl.BlockSpec((Q_BLOCK, HEAD_DIM), lambda b, g, i: (b * nqb + i, gate0 + g))],
        out_specs=pl.BlockSpec((Q_BLOCK, gcols), lambda b, g, i: (b * nqb + i, g)),
        scratch_shapes=[pltpu.VMEM((KEY_PAD + t, 2 * HEAD_DIM), BF16), pltpu.VMEM((HEAD_DIM, KEY_PAD + t), BF16),
                        pltpu.VMEM((KEY_PAD + t, 2 * HEAD_DIM), BF16), pltpu.VMEM((HEAD_DIM, KEY_PAD + t), BF16)],
        compiler_params=_cparams(("parallel", "parallel", "arbitrary")),
        name="nsa_attn",
    )(proj, kvc, kvc, band_t, imap_t, proj, proj, proj, proj, bias_sel, bias_win, proj)


def _proj_res_kernel(a_ref, w_ref, x_ref, gt_ref, lg_ref, lb_ref, o_ref):
    y = _dot(a_ref[...], w_ref[...])
    z = DN_ALPHA * x_ref[...] + (1.0 + gt_ref[0]) * y
    o_ref[...] = _ln_rows(z, lg_ref[...], lb_ref[...])


def _proj_res(a, w, x, gate, ln_g, ln_b, *, rows_per_batch, tm):
    m, d = x.shape
    k = a.shape[1]
    r = gate.shape[1]
    tpb = rows_per_batch // tm
    vec_spec = pl.BlockSpec((1, d), lambda i: (0, 0))
    return pl.pallas_call(
        _proj_res_kernel,
        out_shape=jax.ShapeDtypeStruct((m, d), F32),
        grid=(m // tm,),
        in_specs=[pl.BlockSpec((tm, k), lambda i: (i, 0)),
                  pl.BlockSpec((k, d), lambda i: (0, 0)),
                  pl.BlockSpec((tm, d), lambda i: (i, 0)),
                  pl.BlockSpec((1, r, d), lambda i: (i // tpb, 0, 0)),
                  vec_spec, vec_spec],
        out_specs=pl.BlockSpec((tm, d), lambda i: (i, 0)),
        compiler_params=_cparams(("parallel",)),
        name="proj_res",
    )(a, w, x, gate, ln_g.reshape(1, d), ln_b.reshape(1, d))


def _rel_bucket_np(dist):
    dist = np.maximum(dist, 0)
    exact = REL_BUCKETS // 2
    logv = (np.log(np.maximum(dist, 1).astype(np.float32) / np.float32(exact))
            / np.float32(math.log(REL_MAX_DIST / exact))).astype(np.float32)
    large = exact + (logv * np.float32(REL_BUCKETS - exact)).astype(np.int32)
    return np.where(dist < exact, dist, np.minimum(large, REL_BUCKETS - 1)).astype(np.int32)


def _cmp_to_slc_np(nseg, nslc):
    i = np.arange(nseg)[:, None] * CMP_STRIDE
    j = np.arange(nslc)[None, :] * SLC_LEN
    ov = np.minimum(i + CMP_LEN, j + SLC_LEN) - np.maximum(i, j)
    return (np.maximum(ov, 0) / CMP_STRIDE).astype(np.float32)


def _bias_tables(rel_table, t):
    tbl_c = rel_table.T - rel_table[REL_BUCKETS - 1][:, None]
    nseg = t // CMP_STRIDE
    ql = np.arange(Q_BLOCK)[None, :]
    dist_c = ql - ((np.arange(2 * nseg)[:, None] - nseg) * CMP_STRIDE + CMP_LEN - 1)

    def lookup(dist):
        bucket = _rel_bucket_np(dist)
        out = jnp.broadcast_to(tbl_c[:, 0][:, None, None], (N_HEADS,) + dist.shape)
        for k in range(1, REL_BUCKETS):
            out = jnp.where(bucket == k, tbl_c[:, k][:, None, None], out)
        return out

    band_t = jnp.where(dist_c >= 0, lookup(dist_c), NEG_BIAS)

    def per_group(dist, ok):
        tab = jnp.where(ok, lookup(dist), NEG_MASK)
        n_keys = dist.shape[0]
        return tab.reshape(N_KV, HPG, n_keys, Q_BLOCK).transpose(0, 2, 1, 3).reshape(N_KV, n_keys, HPG * Q_BLOCK)

    dist_s = ql + (SEL_CHUNK - Q_BLOCK) - np.arange(SEL_CHUNK)[:, None]
    dist_w = ql + WINDOW - np.arange(WINDOW + Q_BLOCK)[:, None]
    return band_t, per_group(dist_s, dist_s >= 0), per_group(dist_w, (dist_w >= 0) & (dist_w < WINDOW))


CMP_PAGES_PER_STEP = 16


def _compress_paged_kernel(pt_ref, *refs, n_pg):
    pages = refs[:n_pg]
    pe_ref, w1a_ref, w1b_ref, w2_ref, o_ref, carry_ref = refs[n_pg:]
    n_c = 2 * N_KV
    seg_rows = CMP_STRIDE * n_c
    segs = PAGE_SIZE // CMP_STRIDE
    rows = n_pg * segs * n_c
    hid_w = w2_ref.shape[0]

    @pl.when(pl.program_id(1) == 0)
    def _():
        carry_ref[...] = jnp.zeros_like(carry_ref)

    a = jnp.concatenate(
        [jnp.concatenate(
            [jnp.concatenate([pg[0, 0, seg_rows * s + n_c * r:seg_rows * s + n_c * (r + 1), :]
                              for s in range(segs)], axis=0)
             for r in range(CMP_STRIDE)], axis=1)
         for pg in pages], axis=0).astype(BF16)
    is_k = lax.broadcasted_iota(jnp.int32, (rows, 1), 0) % n_c < N_KV
    is_k8 = lax.broadcasted_iota(jnp.int32, (n_c, 1), 0) < N_KV

    def pick(y, width, k_rows):
        return jnp.where(k_rows, y[:, 0:width], y[:, width:2 * width])

    p0 = pick(_dot(a, w1a_ref[...]), hid_w, is_k)
    p1 = pick(_dot(a, w1b_ref[...]), hid_w, is_k)
    half = CMP_STRIDE * HEAD_DIM
    pe = pe_ref[...].astype(BF16)
    peh = pick(_dot(pe[:, 0:half], w1a_ref[...]) + _dot(pe[:, half:2 * half], w1b_ref[...]), hid_w, is_k8)
    shifted = jnp.concatenate([carry_ref[...], p0[0:rows - n_c]], axis=0)
    carry_ref[...] = p0[rows - n_c:rows]
    hid = (shifted + p1).reshape(rows // n_c, n_c, hid_w) + peh[None]
    y = jax.nn.gelu(hid.reshape(rows, hid_w)).astype(BF16)
    o_ref[0] = pick(_dot(y, w2_ref[...]), HEAD_DIM, is_k)


def _compress_paged(cache_pages, page_table, page0, pe, w1, w2):
    n_s, n_pages = page_table.shape
    n_pg = min(CMP_PAGES_PER_STEP, n_pages)
    n_c = 2 * N_KV
    half = CMP_STRIDE * HEAD_DIM
    rows_pg = PAGE_SIZE * n_c
    out_rows = n_pg * (PAGE_SIZE // CMP_STRIDE) * n_c
    hid_w = w1.shape[2]
    w1a = jnp.concatenate([w1[0, :half], w1[1, :half]], axis=1)
    w1b = jnp.concatenate([w1[0, half:], w1[1, half:]], axis=1)
    w2c = jnp.concatenate([w2[0], w2[1]], axis=1)
    pe8 = jnp.repeat(pe.reshape(2, 2 * half), N_KV, axis=0)

    def page_spec(p):
        return pl.BlockSpec((1, 1, rows_pg, HEAD_DIM), lambda b, c, pt: (0, page0 + pt[b, c * n_pg + p], 0, 0))

    const = lambda b, c, pt: (0, 0)
    grid_spec = pltpu.PrefetchScalarGridSpec(
        num_scalar_prefetch=1,
        grid=(n_s, n_pages // n_pg),
        in_specs=[page_spec(p) for p in range(n_pg)] + [
            pl.BlockSpec((n_c, 2 * half), const),
            pl.BlockSpec((half, 2 * hid_w), const),
            pl.BlockSpec((half, 2 * hid_w), const),
            pl.BlockSpec((hid_w, 2 * HEAD_DIM), const)],
        out_specs=pl.BlockSpec((1, out_rows, HEAD_DIM), lambda b, c, pt: (b, c, 0)),
        scratch_shapes=[pltpu.VMEM((n_c, hid_w), F32)])
    return pl.pallas_call(
        functools.partial(_compress_paged_kernel, n_pg=n_pg),
        out_shape=jax.ShapeDtypeStruct((n_s, n_pages * (PAGE_SIZE // CMP_STRIDE) * n_c, HEAD_DIM), F32),
        grid_spec=grid_spec,
        compiler_params=_cparams(("parallel", "arbitrary")),
        name="compress_paged",
    )(page_table, *([cache_pages[None]] * n_pg), pe8, w1a, w1b, w2c)


def _sample_cmp_kernel(q_ref, kv_ref, bias_ref, im_ref, oc_ref, idx_ref, *, nslc, nsel, pos):
    n_c = 2 * N_KV
    q = (q_ref[0] * ATT_SCALE).astype(BF16)
    kv = kv_ref[0].astype(BF16)
    bias = bias_ref[...]
    valid = bias > 0.5 * NEG_BIAS
    s = _dot_nt(q, kv) + bias
    m = jnp.max(s, axis=-1, keepdims=True)
    e = jnp.where(valid, jnp.exp(s - m), 0.0)
    den = jnp.sum(e, axis=-1, keepdims=True)
    p = e / jnp.where(den > 0.0, den, 1.0)
    oc_ref[0] = _dot(pltpu.roll(p, N_KV, 1).astype(BF16), kv)

    p_hi = p.astype(BF16)
    p_lo = (p - p_hi.astype(F32)).astype(BF16)
    imp_h = _dot(p_hi, im_ref[...]) + _dot(p_lo, im_ref[...])
    lanes = im_ref.shape[1]
    head_grp = lax.broadcasted_iota(jnp.int32, (N_HEADS, lanes), 0) // HPG
    row8 = lax.broadcasted_iota(jnp.int32, (n_c, lanes), 0)
    imp = jnp.zeros((n_c, lanes), F32)
    for g in range(N_KV):
        imp_g = jnp.sum(jnp.where(head_grp == g, imp_h, 0.0), axis=0, keepdims=True)
        imp = imp + jnp.where(row8 == g, imp_g, 0.0)
    jidx = lax.broadcasted_iota(jnp.int32, (n_c, lanes), 1)
    cur = pos // SLC_LEN
    valid_blk = (jidx * SLC_LEN <= pos) & (jidx < nslc)
    forced = (jidx == 0) | (jidx == cur) | (jidx == cur - 1)
    score = jnp.where(valid_blk, imp + jnp.where(forced, FORCE_BONUS, 0.0), -jnp.inf)
    cnt = jnp.zeros((n_c, lanes), jnp.int32)
    for jp in range(nslc):
        col = score[:, jp:jp + 1]
        ahead = (col > score) | ((col == score) & (jidx > jp))
        cnt = cnt + jnp.where(ahead, 1, 0)
    out_lane = lax.broadcasted_iota(jnp.int32, (n_c, HEAD_DIM), 1)
    jf = jidx.astype(F32)
    out = jnp.zeros((n_c, HEAD_DIM), F32)
    for r in range(nsel):
        blk_r = jnp.sum(jnp.where((cnt == r) & valid_blk, jf, 0.0), axis=-1, keepdims=True)
        out = jnp.where(out_lane == r, blk_r, out)
    idx_ref[0] = out.astype(jnp.int32)


def _sample_cmp(q_heads, kvc, bias, imap, *, nslc, nsel, pos):
    n_s, rows, _ = kvc.shape
    lanes = imap.shape[1]
    return pl.pallas_call(
        functools.partial(_sample_cmp_kernel, nslc=nslc, nsel=nsel, pos=pos),
        out_shape=(jax.ShapeDtypeStruct((n_s, N_HEADS, HEAD_DIM), F32),
                   jax.ShapeDtypeStruct((n_s, 2 * N_KV, HEAD_DIM), jnp.int32)),
        grid=(n_s,),
        in_specs=[pl.BlockSpec((1, N_HEADS, HEAD_DIM), lambda b: (b, 0, 0)),
                  pl.BlockSpec((1, rows, HEAD_DIM), lambda b: (b, 0, 0)),
                  pl.BlockSpec((N_HEADS, rows), lambda b: (0, 0)),
                  pl.BlockSpec((rows, lanes), lambda b: (0, 0))],
        out_specs=(pl.BlockSpec((1, N_HEADS, HEAD_DIM), lambda b: (b, 0, 0)),
                   pl.BlockSpec((1, 2 * N_KV, HEAD_DIM), lambda b: (b, 0, 0))),
        compiler_params=_cparams(("parallel",)),
        name="sample_cmp",
    )(q_heads, kvc, bias, imap)


def _sample_sel_win_kernel(idx_ref, pt_ref, *refs, n_slot, pos, n_past_blk, buf_len, thresholds):
    slots = refs[:n_slot]
    q_ref, new_ref, win_ref, tb_ref, oc_ref, gate_ref, o_ref = refs[n_slot:]
    b = pl.program_id(0)
    g = pl.program_id(1)
    n_c = 2 * N_KV
    blk_lanes = SLC_LEN * n_c
    q = (q_ref[0, 0] * ATT_SCALE).astype(BF16)
    tb = tb_ref[0]

    def bias_of(dist):
        d = jnp.maximum(dist, 0)
        out = jnp.zeros(d.shape, F32) + tb[:, 0:1]
        for k in range(1, REL_BUCKETS):
            out = out + jnp.where(d >= thresholds[k], tb[:, k:k + 1] - tb[:, k - 1:k], 0.0)
        return out

    def softmax(s, ok):
        s = jnp.where(ok, s, NEG_BIAS)
        m = jnp.max(s, axis=-1, keepdims=True)
        e = jnp.where(ok, jnp.exp(s - m), 0.0)
        den = jnp.sum(e, axis=-1, keepdims=True)
        return e / jnp.where(den > 0.0, den, 1.0)

    new_rows = new_ref[0, 0]
    s_new = _dot_nt(q, new_rows.astype(BF16))
    lane_new = lax.broadcasted_iota(jnp.int32, s_new.shape, 1)
    bias_new = bias_of(jnp.zeros(s_new.shape, jnp.int32))

    lane = lax.broadcasted_iota(jnp.int32, (n_c, blk_lanes), 1)
    mine = lane % n_c == g
    r_in = lane // n_c
    base = (b * N_KV + g) * n_slot
    s_parts, d_parts, ok_parts = [], [], []
    n_new = 0
    for i in range(n_slot):
        j = idx_ref[base + i]
        dist = jnp.where(j < n_past_blk, pos - j * SLC_LEN, -1) - r_in
        s_parts.append(_dot_nt(q, slots[i][0, 0].astype(BF16)))
        d_parts.append(dist)
        ok_parts.append(mine & (dist >= 0))
        n_new = n_new + jnp.where(j >= n_past_blk, 1, 0)
    s_all = jnp.concatenate(s_parts + [s_new], axis=1)
    ok_all = jnp.concatenate(ok_parts + [lane_new == jnp.where(n_new > 0, 0, -1)], axis=1)
    b_all = jnp.concatenate([bias_of(jnp.concatenate(d_parts, axis=1)), bias_new], axis=1)
    p = softmax(s_all + b_all, ok_all)
    n_old = n_slot * blk_lanes
    p_v = pltpu.roll(p[:, 0:n_old], N_KV, 1).astype(BF16)
    o_s = p[:, n_old:n_old + 1] * new_rows[1:2, :]
    for i in range(n_slot):
        o_s = o_s + _dot(p_v[:, i * blk_lanes:(i + 1) * blk_lanes], slots[i][0, 0].astype(BF16))

    win = win_ref[0].astype(BF16)
    lane_w = lax.broadcasted_iota(jnp.int32, (n_c, buf_len * n_c), 1)
    dist_w = buf_len - lane_w // n_c
    ok_w = (lane_w % n_c == g) & (dist_w >= 0) & (dist_w < WINDOW)
    s_w = jnp.concatenate([_dot_nt(q, win) + bias_of(dist_w), s_new + bias_new], axis=1)
    p = softmax(s_w, jnp.concatenate([ok_w, lane_new == 2], axis=1))
    n_old = buf_len * n_c
    o_w = _dot(pltpu.roll(p[:, 0:n_old], N_KV, 1).astype(BF16), win) + p[:, n_old + 2:n_old + 3] * new_rows[3:4, :]

    gates = jax.nn.sigmoid(gate_ref[0, 0])
    o_ref[0, 0] = gates[:, 0:1] * oc_ref[0, 0] + gates[:, 1:2] * o_s + gates[:, 2:3] * o_w


def _sample_sel_win(idx, page_table, slc_halves, page0, q_g, new_rows, win_rows, tb, o_c, gates, *, pos, nsel):
    n_s, n_pages = page_table.shape
    n_c = 2 * N_KV
    per_page = PAGE_SIZE // SLC_LEN
    n_past_blk = n_pages * per_page
    buf_len = win_rows.shape[1] // n_c
    blk_rows = SLC_LEN * n_c
    thresholds = tuple(int(np.argmax(_rel_bucket_np(np.arange(4 * REL_MAX_DIST)) >= k)) for k in range(REL_BUCKETS))

    def slot_spec(i):
        def index_map(b, g, idx_ref, pt_ref):
            j = jnp.minimum(idx_ref[(b * N_KV + g) * nsel + i], n_past_blk - 1)
            return (page0 + pt_ref[b, j // per_page], j % per_page, 0, 0)
        return pl.BlockSpec((1, 1, blk_rows, HEAD_DIM), index_map)

    grp = lambda b, g, idx_ref, pt_ref: (b, g, 0, 0)
    grid_spec = pltpu.PrefetchScalarGridSpec(
        num_scalar_prefetch=2,
        grid=(n_s, N_KV),
        in_specs=[slot_spec(i) for i in range(nsel)] + [
            pl.BlockSpec((1, 1, n_c, HEAD_DIM), grp),
            pl.BlockSpec((1, 1, HEAD_DIM, HEAD_DIM), grp),
            pl.BlockSpec((1, buf_len * n_c, HEAD_DIM), lambda b, g, idx_ref, pt_ref: (b, 0, 0)),
            pl.BlockSpec((1, n_c, HEAD_DIM), lambda b, g, idx_ref, pt_ref: (g, 0, 0)),
            pl.BlockSpec((1, 1, n_c, HEAD_DIM), grp),
            pl.BlockSpec((1, 1, n_c, HEAD_DIM), grp)],
        out_specs=pl.BlockSpec((1, 1, n_c, HEAD_DIM), grp))
    return pl.pallas_call(
        functools.partial(_sample_sel_win_kernel, n_slot=nsel, pos=pos, n_past_blk=n_past_blk, buf_len=buf_len,
                          thresholds=thresholds),
        out_shape=jax.ShapeDtypeStruct((n_s, N_KV, n_c, HEAD_DIM), F32),
        grid_spec=grid_spec,
        compiler_params=_cparams(("parallel", "parallel")),
        name="sample_sel_win",
    )(idx, page_table, *([slc_halves] * nsel), q_g, new_rows, win_rows, tb, o_c, gates)


def _sample_nsa(proj_s, cmp_pages, slc_halves, page0, win_buf, page_table, pe, w1, w2, rel_table):
    n_s, n_pages = page_table.shape
    past = n_pages * PAGE_SIZE
    pos = past
    n_c = 2 * N_KV
    d_q = N_HEADS * HEAD_DIM
    kvw = n_c * HEAD_DIM
    nseg = past // CMP_STRIDE
    nslc = -(-(past + 1) // SLC_LEN)
    nsel = min(N_SEL, nslc)
    kvc = _compress_paged(cmp_pages, page_table, page0, pe, w1, w2)
    x = np.arange(nseg)[:, None]
    c = np.arange(n_c)[None, :]
    dist = pos - ((x - 1) * CMP_STRIDE + CMP_LEN - 1)
    head_grp = np.arange(N_HEADS)[:, None, None] // HPG
    ok = ((x >= 1) & (dist >= 0))[None] & (c[None] == head_grp)
    bias = jnp.where(ok, jnp.take(rel_table.T, _rel_bucket_np(np.broadcast_to(dist, (nseg, n_c))), axis=1),
                     NEG_BIAS).reshape(N_HEADS, nseg * n_c)
    lanes = -(-nslc // HEAD_DIM) * HEAD_DIM
    imap = np.zeros((nseg, n_c, lanes), np.float32)
    imap[1:, :, :nslc] = _cmp_to_slc_np(nseg - 1, nslc)[:, None, :]
    q_heads = proj_s[:, :d_q].reshape(n_s, N_HEADS, HEAD_DIM)
    o_c, idx = _sample_cmp(q_heads, kvc, bias, jnp.asarray(imap.reshape(nseg * n_c, lanes)).astype(BF16),
                           nslc=nslc, nsel=nsel, pos=pos)
    pad_heads = lambda a: jnp.pad(a.reshape(n_s, N_KV, HPG, -1), ((0, 0), (0, 0), (0, n_c - HPG), (0, 0)))
    kv_new = proj_s[:, d_q:d_q + 3 * kvw].reshape(n_s, 3, 2, N_KV, HEAD_DIM)
    new_rows = kv_new[:, 1:3].transpose(0, 3, 1, 2, 4).reshape(n_s, N_KV, 4, HEAD_DIM)
    new_rows = jnp.pad(new_rows, ((0, 0), (0, 0), (0, HEAD_DIM - 4), (0, 0)))
    gates = proj_s[:, d_q + 3 * kvw:].reshape(n_s, N_KV, HEAD_DIM)[:, :, :3 * HPG].reshape(n_s, N_KV, 3, HPG)
    gates = jnp.pad(gates.transpose(0, 1, 3, 2), ((0, 0), (0, 0), (0, n_c - HPG), (0, HEAD_DIM - 3)))
    tb = jnp.pad(rel_table.T.reshape(N_KV, HPG, REL_BUCKETS), ((0, 0), (0, n_c - HPG), (0, HEAD_DIM - REL_BUCKETS)))
    o = _sample_sel_win(idx[:, :N_KV, :nsel].reshape(-1), page_table, slc_halves, page0,
                        pad_heads(proj_s[:, :d_q]), new_rows, win_buf.reshape(n_s, -1, HEAD_DIM), tb,
                        pad_heads(o_c), gates, pos=pos, nsel=nsel)
    new_win = jnp.concatenate([win_buf, kv_new[:, 2][:, None]], axis=1)[:, -min(WINDOW, win_buf.shape[1] + 1):]
    return o[:, :, :HPG].reshape(n_s, d_q), kv_new, new_win


def _pad_rows(a, rows):
    return jnp.pad(a, ((0, rows - a.shape[0]),) + ((0, 0),) * (a.ndim - 1))


def _nsa_w_in_layout(w):
    d = w.shape[0]
    n_main = N_HEADS * HEAD_DIM + 6 * N_KV * HEAD_DIM
    wg = w[:, n_main:].reshape(d, 3, N_KV, HPG).transpose(0, 2, 1, 3).reshape(d, N_KV, 3 * HPG)
    wg = jnp.pad(wg, ((0, 0), (0, 0), (0, HEAD_DIM - 3 * HPG))).reshape(d, N_KV * HEAD_DIM)
    return jnp.concatenate([w[:, :n_main], wg], axis=1).astype(BF16)


def kernel(x_prompt, x_sample, cache_cmp_kv, cache_slc_kv, state_win_kv, page_table, c_prompt, c_sample,
           ada_w, ada_b, ln_g, ln_b, ffn_pre_w_in, ffn_pre_w_out, ffn_post_w_in, ffn_post_w_out,
           gmlp_w_in, gmlp_ln_g, gmlp_ln_b, gmlp_w_s, gmlp_b_s, gmlp_w_out,
           nsa_w_in, nsa_cmp_pe, nsa_cmp_w1, nsa_cmp_w2, nsa_w_out, rel_table):
    n_b, t, d = x_prompt.shape
    n_s = x_sample.shape[0]
    n_phys = cache_cmp_kv.shape[1]
    d_q = N_HEADS * HEAD_DIM
    kvw = 2 * N_KV * HEAD_DIM
    tm_p = min(512, t)
    tm_f = min(1024, t)
    tm_mm = min(1024, t)
    tm_g = min(512, t)

    xp = x_prompt.reshape(n_b * t, d)
    xs = _pad_rows(x_sample.reshape(n_s, d), SAMPLE_ROWS)
    c_all = _pad_rows(jnp.concatenate([c_prompt, c_sample], axis=0), SAMPLE_ROWS)
    mod = _ada(c_all, ada_w, ada_b)
    band_t, bias_sel, bias_win = _bias_tables(rel_table, t)
    imap_t = jnp.asarray(_cmp_to_slc_np(t // CMP_STRIDE, t // SLC_LEN).T).astype(BF16)
    cmp_pages = cache_cmp_kv.reshape(-1, PAGE_SIZE * 2 * N_KV, HEAD_DIM)
    slc_halves = cache_slc_kv.reshape(-1, PAGE_SIZE // SLC_LEN, SLC_LEN * 2 * N_KV, HEAD_DIM)

    cmp_p, cmp_s, slc_p, slc_s, win_p, win_s, gv_s = [], [], [], [], [], [], []
    for i in range(DEPTH):
        mi = mod[i].reshape(SAMPLE_ROWS, N_ADA, d)
        mp = [mi[:n_b, k][:, None, :] for k in range(N_ADA)]
        ms = [_pad_rows(mi[n_b:n_b + n_s, k], SAMPLE_ROWS)[None] for k in range(N_ADA)]

        w_in = ffn_pre_w_in[i].astype(BF16)
        w_out = ffn_pre_w_out[i].astype(BF16)
        xp = _ffn(xp, mp[0], mp[1], mp[2], w_in, w_out, ln_g[i, 0], ln_b[i, 0], rows_per_batch=t, tm=tm_f)
        xs = _ffn(xs, ms[0], ms[1], ms[2], w_in, w_out, ln_g[i, 0], ln_b[i, 0],
                  rows_per_batch=SAMPLE_ROWS, tm=SAMPLE_ROWS)

        if i % 2 == 0:
            a = i // 2
            gw_in = gmlp_w_in[a].astype(BF16)
            gw_out = gmlp_w_out[a].astype(BF16)
            xp, = _gmlp(xp, mp[3], mp[4], mp[5], gw_in, gmlp_ln_g[a], gmlp_ln_b[a], gmlp_w_s[a], gmlp_b_s[a],
                        gw_out, ln_g[i, 1], ln_b[i, 1], rows_per_batch=t, tm=tm_g, chunk=min(CHUNK, t),
                        emit_vn=False)
            xs, vn_s = _gmlp(xs, ms[3], ms[4], ms[5], gw_in, gmlp_ln_g[a], gmlp_ln_b[a], gmlp_w_s[a], gmlp_b_s[a],
                             gw_out, ln_g[i, 1], ln_b[i, 1], rows_per_batch=SAMPLE_ROWS, tm=SAMPLE_ROWS, chunk=1,
                             emit_vn=True)
            gv_s.append(vn_s[:n_s].reshape(n_s, 1, -1))
        else:
            a = i // 2
            nw_in = _nsa_w_in_layout(nsa_w_in[a])
            nw_out = nsa_w_out[a].astype(BF16)
            w1 = nsa_cmp_w1[a].astype(BF16)
            w2 = nsa_cmp_w2[a].astype(BF16)
            proj = _mod_mm(xp, mp[3], mp[4], nw_in, rows_per_batch=t, tm=tm_mm, tn=512)
            kvc = _compress(proj, n_b, t, d_q // HEAD_DIM, nsa_cmp_pe[a], w1, w2)
            o = _nsa_attn(proj, kvc, band_t, imap_t, bias_sel, bias_win, n_b, t)
            xp = _proj_res(o, nw_out, xp, mp[5], ln_g[i, 1], ln_b[i, 1], rows_per_batch=t, tm=tm_p)
            kv_all = proj[:, d_q:d_q + 3 * kvw].reshape(n_b, t, 3, 2, N_KV, HEAD_DIM)
            cmp_p.append(kv_all[:, :, 0])
            slc_p.append(kv_all[:, :, 1])
            win_p.append(kv_all[:, -min(WINDOW, t):, 2])
            proj_s = _mod_mm(xs, ms[3], ms[4], nw_in, rows_per_batch=SAMPLE_ROWS, tm=SAMPLE_ROWS, tn=512)[:n_s]
            o_samp, kv_new, new_win = _sample_nsa(proj_s, cmp_pages, slc_halves, a * n_phys, state_win_kv[a],
                                                  page_table, nsa_cmp_pe[a], w1, w2, rel_table)
            xs = _proj_res(_pad_rows(o_samp, SAMPLE_ROWS).astype(BF16), nw_out, xs, ms[5], ln_g[i, 1], ln_b[i, 1],
                           rows_per_batch=SAMPLE_ROWS, tm=SAMPLE_ROWS)
            cmp_s.append(kv_new[:, 0][:, None])
            slc_s.append(kv_new[:, 1][:, None])
            win_s.append(new_win)

        w_in = ffn_post_w_in[i].astype(BF16)
        w_out = ffn_post_w_out[i].astype(BF16)
        xp = _ffn(xp, mp[6], mp[7], mp[8], w_in, w_out, ln_g[i, 2], ln_b[i, 2], rows_per_batch=t, tm=tm_f)
        xs = _ffn(xs, ms[6], ms[7], ms[8], w_in, w_out, ln_g[i, 2], ln_b[i, 2],
                  rows_per_batch=SAMPLE_ROWS, tm=SAMPLE_ROWS)

    return (xp.reshape(n_b, t, d), xs[:n_s].reshape(n_s, 1, d),
            jnp.stack(cmp_p), jnp.stack(cmp_s), jnp.stack(slc_p), jnp.stack(slc_s),
            jnp.stack(win_p), jnp.stack(win_s), jnp.stack(gv_s))
```

```python
import functools
import math

import numpy as np
import jax
import jax.numpy as jnp
from jax import lax
from jax.experimental import pallas as pl
from jax.experimental.pallas import tpu as pltpu

F32 = jnp.float32
BF16 = jnp.bfloat16

DEPTH = 4
N_ADA = 9
N_HEADS = 16
HEAD_DIM = 128
N_KV = 4
HPG = N_HEADS // N_KV
GMLP_GROUPS = 16
CHUNK = 128
PAGE_SIZE = 128
CMP_LEN = 32
CMP_STRIDE = 16
SLC_LEN = 64
N_SEL = 16
WINDOW = 512
Q_BLOCK = 128
REL_BUCKETS = 32
REL_MAX_DIST = 128
LN_EPS = 1e-5
DN_ALPHA = (2 * DEPTH) ** 0.25
FORCE_BONUS = 1e4
ATT_SCALE = HEAD_DIM ** -0.5

SAMPLE_ROWS = 16
NEG_MASK = -30000.0
NEG_BIAS = -1e30
SEL_MASK = 32768.0
KEY_PAD = WINDOW
SEL_CHUNK = 512
VMEM_LIMIT = 56 * 1024 * 1024


def _cparams(sem):
    return pltpu.CompilerParams(dimension_semantics=sem, vmem_limit_bytes=VMEM_LIMIT)


def _dot(a, b):
    return jnp.dot(a, b, preferred_element_type=F32)


def _dot_nt(a, b):
    return lax.dot_general(a, b, (((1,), (1,)), ((), ())), preferred_element_type=F32)


def _ln_rows(z, g, b):
    mu = jnp.mean(z, axis=-1, keepdims=True)
    zc = z - mu
    var = jnp.mean(zc * zc, axis=-1, keepdims=True)
    return zc * lax.rsqrt(var + LN_EPS) * g + b


def _silu(x):
    return x * jax.nn.sigmoid(x)


def _ada_kernel(c_ref, w_ref, b_ref, o_ref):
    h = _silu(c_ref[...]).astype(BF16)
    o_ref[0] = _dot(h, w_ref[0].astype(BF16)) + b_ref[0]


def _ada(c_all, ada_w, ada_b, tn=1024):
    depth, d, n = ada_w.shape
    r = c_all.shape[0]
    return pl.pallas_call(
        _ada_kernel,
        out_shape=jax.ShapeDtypeStruct((depth, r, n), F32),
        grid=(depth, n // tn),
        in_specs=[pl.BlockSpec((r, d), lambda l, j: (0, 0)),
                  pl.BlockSpec((1, d, tn), lambda l, j: (l, 0, j)),
                  pl.BlockSpec((1, 1, tn), lambda l, j: (l, 0, j))],
        out_specs=pl.BlockSpec((1, r, tn), lambda l, j: (l, 0, j)),
        compiler_params=_cparams(("parallel", "parallel")),
        name="ada",
    )(c_all, ada_w, ada_b.reshape(depth, 1, n))


def _ffn_kernel(x_ref, sh_ref, sc_ref, gt_ref, wg_ref, wu_ref, wo_ref, lg_ref, lb_ref, o_ref, h_ref):
    j = pl.program_id(1)

    @pl.when(j == 0)
    def _():
        h_ref[...] = (x_ref[...] * (1.0 + sc_ref[0]) + sh_ref[0]).astype(BF16)
        o_ref[...] = jnp.zeros_like(o_ref)

    h = h_ref[...]
    g = _dot(h, wg_ref[...])
    u = _dot(h, wu_ref[...])
    a = (_silu(g) * u).astype(BF16)
    o_ref[...] += _dot(a, wo_ref[...])

    @pl.when(j == pl.num_programs(1) - 1)
    def _():
        z = DN_ALPHA * x_ref[...] + (1.0 + gt_ref[0]) * (0.5 * o_ref[...])
        o_ref[...] = _ln_rows(z, lg_ref[...], lb_ref[...])


def _ffn(x, mod, mod_spec, slots, w_in, w_out, ln_g, ln_b, *, tm, tf=512):
    m, d = x.shape
    dff = w_out.shape[0]
    nf = dff // tf
    vec_spec = pl.BlockSpec((1, d), lambda i, j: (0, 0))
    return pl.pallas_call(
        _ffn_kernel,
        out_shape=jax.ShapeDtypeStruct((m, d), F32),
        grid=(m // tm, nf),
        in_specs=[pl.BlockSpec((tm, d), lambda i, j: (i, 0)),
                  mod_spec(slots[0]), mod_spec(slots[1]), mod_spec(slots[2]),
                  pl.BlockSpec((d, tf), lambda i, j: (0, j)),
                  pl.BlockSpec((d, tf), lambda i, j: (0, j + nf)),
                  pl.BlockSpec((tf, d), lambda i, j: (j, 0)),
                  vec_spec, vec_spec],
        out_specs=pl.BlockSpec((tm, d), lambda i, j: (i, 0)),
        scratch_shapes=[pltpu.VMEM((tm, d), BF16)],
        compiler_params=_cparams(("parallel", "arbitrary")),
        name="ffn",
    )(x, mod, mod, mod, w_in, w_in, w_out, ln_g.reshape(1, d), ln_b.reshape(1, d))


def _mm_kernel(x_ref, sh_ref, sc_ref, w_ref, o_ref, h_ref):
    @pl.when(pl.program_id(1) == 0)
    def _():
        h_ref[...] = (x_ref[...] * (1.0 + sc_ref[0]) + sh_ref[0]).astype(BF16)

    o_ref[...] = _dot(h_ref[...], w_ref[...])


def _mod_mm(x, mod, mod_spec, slots, w, *, tm, tn):
    m, d = x.shape
    n = w.shape[1]
    return pl.pallas_call(
        _mm_kernel,
        out_shape=jax.ShapeDtypeStruct((m, n), F32),
        grid=(m // tm, n // tn),
        in_specs=[pl.BlockSpec((tm, d), lambda i, j: (i, 0)),
                  mod_spec(slots[0]), mod_spec(slots[1]),
                  pl.BlockSpec((d, tn), lambda i, j: (0, j))],
        out_specs=pl.BlockSpec((tm, tn), lambda i, j: (i, j)),
        scratch_shapes=[pltpu.VMEM((tm, d), BF16)],
        compiler_params=_cparams(("parallel", "arbitrary")),
        name="mod_mm",
    )(x, mod, mod, w)


GMLP_GROUPS_PER_STEP = 2


def _gmlp_kernel(x_ref, sh_ref, sc_ref, gt_ref, win_ref, vlg_ref, vlb_ref, ws_ref, bs_ref,
                 wo_ref, lg_ref, lb_ref, o_ref, *rest, chunk, gw, emit_vn):
    if emit_vn:
        vn_ref, h_ref, v_ref, mu_ref, rs_ref = rest
    else:
        h_ref, v_ref, mu_ref, rs_ref = rest
    k = pl.program_id(1)
    ns = pl.num_programs(1) // 2
    tm = x_ref.shape[0]
    gdim = ns * v_ref.shape[2]

    @pl.when(k == 0)
    def _():
        h_ref[...] = (x_ref[...] * (1.0 + sc_ref[0]) + sh_ref[0]).astype(BF16)
        o_ref[...] = jnp.zeros_like(o_ref)

    y = jax.nn.gelu(_dot(h_ref[...], win_ref[...]))

    @pl.when(k < ns)
    def _():
        v_ref[k] = y

    @pl.when(k == ns)
    def _():
        tot = jnp.zeros((tm, 1), F32)
        for c in range(ns):
            tot = tot + jnp.sum(v_ref[c], axis=-1, keepdims=True)
        mu = tot / gdim
        sq = jnp.zeros((tm, 1), F32)
        for c in range(ns):
            dv = v_ref[c] - mu
            sq = sq + jnp.sum(dv * dv, axis=-1, keepdims=True)
        mu_ref[...] = mu
        rs_ref[...] = lax.rsqrt(sq / gdim + LN_EPS)

    @pl.when(k >= ns)
    def _():
        vn = (v_ref[k - ns] - mu_ref[...]) * rs_ref[...] * vlg_ref[...] + vlb_ref[...]
        if emit_vn:
            vn_ref[...] = vn
        parts = []
        for gi in range(GMLP_GROUPS_PER_STEP):
            vn_g = vn[:, gi * gw:(gi + 1) * gw]
            if chunk == 1:
                s_g = ws_ref[gi, 0:1, 0:1] * vn_g + bs_ref[gi, 0:1, 0:1]
            else:
                row = lax.broadcasted_iota(jnp.int32, (chunk, chunk), 0)
                col = lax.broadcasted_iota(jnp.int32, (chunk, chunk), 1)
                w_tri = jnp.where(col <= row, ws_ref[gi], 0.0).astype(BF16)
                vb = vn_g.astype(BF16)
                s_g = jnp.concatenate(
                    [_dot(w_tri, vb[c * chunk:(c + 1) * chunk]) + bs_ref[gi]
                     for c in range(tm // chunk)], axis=0)
            parts.append(s_g)
        a = (y * jnp.concatenate(parts, axis=1)).astype(BF16)
        o_ref[...] += _dot(a, wo_ref[...])

    @pl.when(k == 2 * ns - 1)
    def _():
        z = DN_ALPHA * x_ref[...] + (1.0 + gt_ref[0]) * o_ref[...]
        o_ref[...] = _ln_rows(z, lg_ref[...], lb_ref[...])


def _gmlp(x, mod, mod_spec, slots, w_in, v_ln_g, v_ln_b, w_s, b_s, w_out, ln_g, ln_b, *, tm, chunk, emit_vn):
    m, d = x.shape
    gdim = w_out.shape[0]
    gw = gdim // GMLP_GROUPS
    gw2 = gw * GMLP_GROUPS_PER_STEP
    ns = GMLP_GROUPS // GMLP_GROUPS_PER_STEP
    vec_spec = pl.BlockSpec((1, d), lambda i, k: (0, 0))
    second = lambda k: jnp.maximum(k - ns, 0)
    out_shape = [jax.ShapeDtypeStruct((m, d), F32)]
    out_specs = [pl.BlockSpec((tm, d), lambda i, k: (i, 0))]
    if emit_vn:
        out_shape.append(jax.ShapeDtypeStruct((m, gdim), F32))
        out_specs.append(pl.BlockSpec((tm, gw2), lambda i, k: (i, second(k))))
    return pl.pallas_call(
        functools.partial(_gmlp_kernel, chunk=chunk, gw=gw, emit_vn=emit_vn),
        out_shape=tuple(out_shape),
        grid=(m // tm, 2 * ns),
        in_specs=[pl.BlockSpec((tm, d), lambda i, k: (i, 0)),
                  mod_spec(slots[0]), mod_spec(slots[1]), mod_spec(slots[2]),
                  pl.BlockSpec((d, gw2), lambda i, k: (0, jnp.where(k < ns, k + ns, k - ns))),
                  pl.BlockSpec((1, gw2), lambda i, k: (0, second(k))),
                  pl.BlockSpec((1, gw2), lambda i, k: (0, second(k))),
                  pl.BlockSpec((GMLP_GROUPS_PER_STEP, CHUNK, CHUNK), lambda i, k: (second(k), 0, 0)),
                  pl.BlockSpec((GMLP_GROUPS_PER_STEP, CHUNK, 1), lambda i, k: (second(k), 0, 0)),
                  pl.BlockSpec((gw2, d), lambda i, k: (second(k), 0)),
                  vec_spec, vec_spec],
        out_specs=tuple(out_specs),
        scratch_shapes=[pltpu.VMEM((tm, d), BF16), pltpu.VMEM((ns, tm, gw2), F32),
                        pltpu.VMEM((tm, 1), F32), pltpu.VMEM((tm, 1), F32)],
        compiler_params=_cparams(("parallel", "arbitrary")),
        name="gmlp",
    )(x, mod, mod, mod, w_in, v_ln_g.reshape(1, gdim), v_ln_b.reshape(1, gdim), w_s,
      b_s.reshape(GMLP_GROUPS, CHUNK, 1), w_out, ln_g.reshape(1, d), ln_b.reshape(1, d))


def _compress_kernel(r_ref, pe_ref, w1_ref, w2_ref, o_ref, *, nseg):
    half = CMP_STRIDE * HEAD_DIM
    a = jnp.concatenate(
        [r_ref[pl.ds(r, nseg, stride=CMP_STRIDE), :].astype(BF16) for r in range(CMP_STRIDE)], axis=1)
    p0 = _dot(a, w1_ref[0, 0:half, :])
    p1 = _dot(a, w1_ref[0, half:2 * half, :])
    pe = jnp.broadcast_to(pe_ref[0], (8, 2 * half)).astype(BF16)
    peh = _dot(pe, w1_ref[0])[0:1]
    hid = peh + p0 + pltpu.roll(p1, nseg - 1, 0)
    o_ref[0, 0, 0] = _dot(jax.nn.gelu(hid).astype(BF16), w2_ref[0])


def _compress(rows2d, n_batch, t, col0, pe, w1, w2):
    nseg = t // CMP_STRIDE
    return pl.pallas_call(
        functools.partial(_compress_kernel, nseg=nseg),
        out_shape=jax.ShapeDtypeStruct((n_batch, 2, N_KV, nseg, HEAD_DIM), F32),
        grid=(n_batch, 2, N_KV),
        in_specs=[pl.BlockSpec((t, HEAD_DIM), lambda b, kv, g: (b, col0 + kv * N_KV + g)),
                  pl.BlockSpec((1, 1, CMP_LEN * HEAD_DIM), lambda b, kv, g: (kv, 0, 0)),
                  pl.BlockSpec((1, CMP_LEN * HEAD_DIM, w1.shape[2]), lambda b, kv, g: (kv, 0, 0)),
                  pl.BlockSpec((1, w2.shape[1], HEAD_DIM), lambda b, kv, g: (kv, 0, 0))],
        out_specs=pl.BlockSpec((1, 1, 1, nseg, HEAD_DIM), lambda b, kv, g: (b, kv, g, 0, 0)),
        compiler_params=_cparams(("parallel", "parallel", "parallel")),
        name="compress",
    )(rows2d, pe.reshape(2, 1, CMP_LEN * HEAD_DIM), w1, w2)


def _softmax_update_t(state, s, vt):
    m, l, acc = state
    m_new = jnp.maximum(m, jnp.max(s, axis=0, keepdims=True))
    alpha = jnp.exp(m - m_new)
    p = jnp.exp(s - m_new)
    l = alpha * l + jnp.sum(p, axis=0, keepdims=True)
    acc = alpha * acc + _dot(vt, p.astype(BF16))
    return m_new, l, acc


def _nsa_attn_kernel(q_ref, kc_ref, vc_ref, band_ref, imap_ref, ks_ref, vs_ref, kw_ref, vw_ref,
                     bsel_ref, bwin_ref, gate_ref, o_ref, ksa, vst, kwa, vwt, *, nslc, nsel):
    qb = pl.program_id(2)
    tq = Q_BLOCK
    cols = HPG * tq
    t = ks_ref.shape[0]
    nseg = kc_ref.shape[3]

    @pl.when(qb == 0)
    def _():
        lane_p = lax.broadcasted_iota(jnp.int32, (KEY_PAD, HEAD_DIM), 1)
        pad_mark = jnp.where(lane_p == HEAD_DIM - 1, 1.0, 0.0).astype(BF16)
        key_blk = lax.broadcasted_iota(jnp.int32, (t, HEAD_DIM), 0) // SLC_LEN
        lane = lax.broadcasted_iota(jnp.int32, (t, HEAD_DIM), 1)
        for ref in (ksa, kwa):
            ref[0:KEY_PAD, 0:HEAD_DIM] = jnp.zeros((KEY_PAD, HEAD_DIM), BF16)
            ref[0:KEY_PAD, HEAD_DIM:2 * HEAD_DIM] = pad_mark
        ksa[KEY_PAD:KEY_PAD + t, 0:HEAD_DIM] = ks_ref[...].astype(BF16)
        ksa[KEY_PAD:KEY_PAD + t, HEAD_DIM:2 * HEAD_DIM] = jnp.where(lane == key_blk, 1.0, 0.0).astype(BF16)
        kwa[KEY_PAD:KEY_PAD + t, 0:HEAD_DIM] = kw_ref[...].astype(BF16)
        kwa[KEY_PAD:KEY_PAD + t, HEAD_DIM:2 * HEAD_DIM] = jnp.zeros((t, HEAD_DIM), BF16)
        vst[:, 0:KEY_PAD] = jnp.zeros((HEAD_DIM, KEY_PAD), BF16)
        vwt[:, 0:KEY_PAD] = jnp.zeros((HEAD_DIM, KEY_PAD), BF16)
        for c in range(t // tq):
            vst[:, KEY_PAD + c * tq:KEY_PAD + (c + 1) * tq] = vs_ref[c * tq:(c + 1) * tq, :].T.astype(BF16)
            vwt[:, KEY_PAD + c * tq:KEY_PAD + (c + 1) * tq] = vw_ref[c * tq:(c + 1) * tq, :].T.astype(BF16)

    q_t = jnp.concatenate(
        [(q_ref[:, h * HEAD_DIM:(h + 1) * HEAD_DIM] * ATT_SCALE).T for h in range(HPG)], axis=1).astype(BF16)

    off = pl.multiple_of(nseg - (tq // CMP_STRIDE) * qb, 8)
    bias = jnp.concatenate([band_ref[h, pl.ds(off, nseg), :] for h in range(HPG)], axis=1)
    valid = bias > 0.5 * NEG_BIAS
    s = _dot(kc_ref[0, 0, 0].astype(BF16), q_t) + bias
    m = jnp.max(s, axis=0, keepdims=True)
    e = jnp.where(valid, jnp.exp(s - m), 0.0)
    den = jnp.sum(e, axis=0, keepdims=True)
    p = e / jnp.where(den > 0.0, den, 1.0)
    o_c = _dot(vc_ref[0, 0, 0].T.astype(BF16), p.astype(BF16))

    psum = p[:, 0:tq]
    for h in range(1, HPG):
        psum = psum + p[:, h * tq:(h + 1) * tq]
    p_hi = psum.astype(BF16)
    p_lo = (psum - p_hi.astype(F32)).astype(BF16)
    imap = imap_ref[...]
    imp = _dot(imap, p_hi) + _dot(imap, p_lo)
    jidx = lax.broadcasted_iota(jnp.int32, (nslc, tq), 0)
    qpos = qb * tq + lax.broadcasted_iota(jnp.int32, (nslc, tq), 1)
    cur = qpos // SLC_LEN
    valid_blk = jidx * SLC_LEN <= qpos
    forced = (jidx == 0) | (jidx == cur) | (jidx == cur - 1)
    score = jnp.where(valid_blk, imp + jnp.where(forced, FORCE_BONUS, 0.0), -jnp.inf)
    slab = 8
    slabs = [score[v:v + slab] for v in range(0, nslc, slab)]
    cnts = [jnp.zeros((slab, tq), jnp.int32) for _ in slabs]
    jloc = lax.broadcasted_iota(jnp.int32, (slab, tq), 0)
    for jp in range(nslc):
        row = score[jp:jp + 1, :]
        for v, blk in enumerate(slabs):
            if v * slab > jp:
                ahead = row >= blk
            elif (v + 1) * slab <= jp:
                ahead = row > blk
            else:
                ahead = (row > blk) | ((row == blk) & (jloc > jp - v * slab))
            cnts[v] = cnts[v] + jnp.where(ahead, 1, 0)
    cnt = jnp.concatenate(cnts, axis=0)
    unsel = jnp.where((cnt < nsel) & valid_blk, 0.0, -SEL_MASK)

    marker_row = lax.broadcasted_iota(jnp.int32, (8, cols), 0) == 7
    aug = jnp.concatenate([jnp.concatenate([unsel] * HPG, axis=1),
                           jnp.zeros((HEAD_DIM - nslc - 8, cols), F32),
                           jnp.where(marker_row, -SEL_MASK, 0.0)], axis=0)
    rhs = jnp.concatenate([q_t, aug.astype(BF16)], axis=0)

    last = pl.multiple_of((qb + 1) * tq, tq)
    s = _dot(ksa[pl.ds(last, SEL_CHUNK), :], rhs) + bsel_ref[0]
    m = jnp.max(s, axis=0, keepdims=True)
    p = jnp.exp(s - m)
    st = (m, jnp.sum(p, axis=0, keepdims=True), _dot(vst[:, pl.ds(last, SEL_CHUNK)], p.astype(BF16)))

    def far_start(c):
        return pl.multiple_of(jnp.maximum(last - (c + 1) * SEL_CHUNK, 0), tq)

    def far_logits(c):
        return _dot(ksa[pl.ds(far_start(c), SEL_CHUNK), :], rhs)

    def far_body(c, carry):
        state, s_c = carry
        s_next = far_logits(c + 1)
        return _softmax_update_t(state, s_c, vst[:, pl.ds(far_start(c), SEL_CHUNK)]), s_next

    (_, l_s, acc_s), _ = lax.fori_loop(0, qb // (SEL_CHUNK // tq), far_body, (st, far_logits(0)))
    o_s = acc_s / l_s

    first = pl.multiple_of(qb * tq, tq)
    s = _dot(kwa[pl.ds(first, WINDOW + tq), :], rhs) + bwin_ref[0]
    p = jnp.exp(s - jnp.max(s, axis=0, keepdims=True))
    o_w = _dot(vwt[:, pl.ds(first, WINDOW + tq)], p.astype(BF16)) / jnp.sum(p, axis=0, keepdims=True)

    gates_t = jax.nn.sigmoid(gate_ref[...]).T
    outs = []
    for h in range(HPG):
        hs = slice(h * tq, (h + 1) * tq)
        merged = (gates_t[h:h + 1, :] * o_c[:, hs] + gates_t[HPG + h:HPG + h + 1, :] * o_s[:, hs]
                  + gates_t[2 * HPG + h:2 * HPG + h + 1, :] * o_w[:, hs])
        outs.append(merged.T)
    o_ref[...] = jnp.concatenate(outs, axis=1).astype(o_ref.dtype)


def _nsa_attn(proj, kvc, band_t, imap_t, bias_sel, bias_win, n_batch, t):
    nqb = t // Q_BLOCK
    nseg = t // CMP_STRIDE
    nslc = t // SLC_LEN
    nsel = min(N_SEL, nslc)
    d_q = N_HEADS * HEAD_DIM
    gcols = HPG * HEAD_DIM
    kv0 = d_q // HEAD_DIM
    gate0 = kv0 + 6 * N_KV

    def kv_spec(branch, kv):
        return pl.BlockSpec((t, HEAD_DIM), lambda b, g, i: (b, kv0 + (branch * 2 + kv) * N_KV + g))

    return pl.pallas_call(
        functools.partial(_nsa_attn_kernel, nslc=nslc, nsel=nsel),
        out_shape=jax.ShapeDtypeStruct((n_batch * t, d_q), BF16),
        grid=(n_batch, N_KV, nqb),
        in_specs=[pl.BlockSpec((Q_BLOCK, gcols), lambda b, g, i: (b * nqb + i, g)),
                  pl.BlockSpec((1, 1, 1, nseg, HEAD_DIM), lambda b, g, i: (b, 0, g, 0, 0)),
                  pl.BlockSpec((1, 1, 1, nseg, HEAD_DIM), lambda b, g, i: (b, 1, g, 0, 0)),
                  pl.BlockSpec((HPG, 2 * nseg, Q_BLOCK), lambda b, g, i: (g, 0, 0)),
                  pl.BlockSpec((nslc, nseg), lambda b, g, i: (0, 0)),
                  kv_spec(1, 0), kv_spec(1, 1), kv_spec(2, 0), kv_spec(2, 1),
                  pl.BlockSpec((1, SEL_CHUNK, HPG * Q_BLOCK), lambda b, g, i: (g, 0, 0)),
                  pl.BlockSpec((1, WINDOW + Q_BLOCK, HPG * Q_BLOCK), lambda b, g, i: (g, 0, 0)),
                  pl.BlockSpec((Q_BLOCK, HEAD_DIM), lambda b, g, i: (b * nqb + i, gate0 + g))],
        out_specs=pl.BlockSpec((Q_BLOCK, gcols), lambda b, g, i: (b * nqb + i, g)),
        scratch_shapes=[pltpu.VMEM((KEY_PAD + t, 2 * HEAD_DIM), BF16), pltpu.VMEM((HEAD_DIM, KEY_PAD + t), BF16),
                        pltpu.VMEM((KEY_PAD + t, 2 * HEAD_DIM), BF16), pltpu.VMEM((HEAD_DIM, KEY_PAD + t), BF16)],
        compiler_params=_cparams(("parallel", "parallel", "arbitrary")),
        name="nsa_attn",
    )(proj, kvc, kvc, band_t, imap_t, proj, proj, proj, proj, bias_sel, bias_win, proj)


def _proj_res_kernel(a_ref, w_ref, x_ref, gt_ref, lg_ref, lb_ref, o_ref):
    y = _dot(a_ref[...], w_ref[...])
    z = DN_ALPHA * x_ref[...] + (1.0 + gt_ref[0]) * y
    o_ref[...] = _ln_rows(z, lg_ref[...], lb_ref[...])


def _proj_res(a, w, x, mod, mod_spec, gate_slot, ln_g, ln_b, *, tm):
    m, d = x.shape
    k = a.shape[1]
    vec_spec = pl.BlockSpec((1, d), lambda i: (0, 0))
    return pl.pallas_call(
        _proj_res_kernel,
        out_shape=jax.ShapeDtypeStruct((m, d), F32),
        grid=(m // tm,),
        in_specs=[pl.BlockSpec((tm, k), lambda i: (i, 0)),
                  pl.BlockSpec((k, d), lambda i: (0, 0)),
                  pl.BlockSpec((tm, d), lambda i: (i, 0)),
                  mod_spec(gate_slot),
                  vec_spec, vec_spec],
        out_specs=pl.BlockSpec((tm, d), lambda i: (i, 0)),
        compiler_params=_cparams(("parallel",)),
        name="proj_res",
    )(a, w, x, mod, ln_g.reshape(1, d), ln_b.reshape(1, d))


def _rel_bucket_np(dist):
    dist = np.maximum(dist, 0)
    exact = REL_BUCKETS // 2
    logv = (np.log(np.maximum(dist, 1).astype(np.float32) / np.float32(exact))
            / np.float32(math.log(REL_MAX_DIST / exact))).astype(np.float32)
    large = exact + (logv * np.float32(REL_BUCKETS - exact)).astype(np.int32)
    return np.where(dist < exact, dist, np.minimum(large, REL_BUCKETS - 1)).astype(np.int32)


def _cmp_to_slc_np(nseg, nslc):
    i = np.arange(nseg)[:, None] * CMP_STRIDE
    j = np.arange(nslc)[None, :] * SLC_LEN
    ov = np.minimum(i + CMP_LEN, j + SLC_LEN) - np.maximum(i, j)
    return (np.maximum(ov, 0) / CMP_STRIDE).astype(np.float32)


def _bias_tables(rel_table, t):
    tbl_c = rel_table.T - rel_table[REL_BUCKETS - 1][:, None]
    nseg = t // CMP_STRIDE
    ql = np.arange(Q_BLOCK)[None, :]
    dist_c = ql - ((np.arange(2 * nseg)[:, None] - nseg) * CMP_STRIDE + CMP_LEN - 1)

    def lookup(dist):
        bucket = _rel_bucket_np(dist)
        out = jnp.broadcast_to(tbl_c[:, 0][:, None, None], (N_HEADS,) + dist.shape)
        for k in range(1, REL_BUCKETS):
            out = jnp.where(bucket == k, tbl_c[:, k][:, None, None], out)
        return out

    band_t = jnp.where(dist_c >= 0, lookup(dist_c), NEG_BIAS)

    def per_group(dist, ok):
        tab = jnp.where(ok, lookup(dist), NEG_MASK)
        n_keys = dist.shape[0]
        return tab.reshape(N_KV, HPG, n_keys, Q_BLOCK).transpose(0, 2, 1, 3).reshape(N_KV, n_keys, HPG * Q_BLOCK)

    dist_s = ql + (SEL_CHUNK - Q_BLOCK) - np.arange(SEL_CHUNK)[:, None]
    dist_w = ql + WINDOW - np.arange(WINDOW + Q_BLOCK)[:, None]
    return band_t, per_group(dist_s, dist_s >= 0), per_group(dist_w, (dist_w >= 0) & (dist_w < WINDOW))


CMP_PAGES_PER_STEP = 16


def _compress_paged_kernel(pt_ref, *refs, n_pg):
    pages = refs[:n_pg]
    pe_ref, w1a_ref, w1b_ref, w2_ref, o_ref, carry_ref = refs[n_pg:]
    n_c = 2 * N_KV
    seg_rows = CMP_STRIDE * n_c
    segs = PAGE_SIZE // CMP_STRIDE
    rows = n_pg * segs * n_c
    hid_w = w2_ref.shape[0]

    @pl.when(pl.program_id(1) == 0)
    def _():
        carry_ref[...] = jnp.zeros_like(carry_ref)

    a = jnp.concatenate(
        [jnp.concatenate(
            [jnp.concatenate([pg[0, 0, seg_rows * s + n_c * r:seg_rows * s + n_c * (r + 1), :]
                              for s in range(segs)], axis=0)
             for r in range(CMP_STRIDE)], axis=1)
         for pg in pages], axis=0).astype(BF16)
    is_k = lax.broadcasted_iota(jnp.int32, (rows, 1), 0) % n_c < N_KV
    is_k8 = lax.broadcasted_iota(jnp.int32, (n_c, 1), 0) < N_KV

    def pick(y, width, k_rows):
        return jnp.where(k_rows, y[:, 0:width], y[:, width:2 * width])

    p0 = pick(_dot(a, w1a_ref[...]), hid_w, is_k)
    p1 = pick(_dot(a, w1b_ref[...]), hid_w, is_k)
    half = CMP_STRIDE * HEAD_DIM
    pe = pe_ref[...].astype(BF16)
    peh = pick(_dot(pe[:, 0:half], w1a_ref[...]) + _dot(pe[:, half:2 * half], w1b_ref[...]), hid_w, is_k8)
    shifted = jnp.concatenate([carry_ref[...], p0[0:rows - n_c]], axis=0)
    carry_ref[...] = p0[rows - n_c:rows]
    hid = (shifted + p1).reshape(rows // n_c, n_c, hid_w) + peh[None]
    y = jax.nn.gelu(hid.reshape(rows, hid_w)).astype(BF16)
    o_ref[0] = pick(_dot(y, w2_ref[...]), HEAD_DIM, is_k)


def _compress_paged(cache_pages, page_table, page0, pe, w1, w2):
    n_s, n_pages = page_table.shape
    n_pg = min(CMP_PAGES_PER_STEP, n_pages)
    n_c = 2 * N_KV
    half = CMP_STRIDE * HEAD_DIM
    rows_pg = PAGE_SIZE * n_c
    out_rows = n_pg * (PAGE_SIZE // CMP_STRIDE) * n_c
    hid_w = w1.shape[2]
    w1a = jnp.concatenate([w1[0, :half], w1[1, :half]], axis=1)
    w1b = jnp.concatenate([w1[0, half:], w1[1, half:]], axis=1)
    w2c = jnp.concatenate([w2[0], w2[1]], axis=1)
    pe8 = jnp.repeat(pe.reshape(2, 2 * half), N_KV, axis=0)

    def page_spec(p):
        return pl.BlockSpec((1, 1, rows_pg, HEAD_DIM), lambda b, c, pt: (0, page0 + pt[b, c * n_pg + p], 0, 0))

    const = lambda b, c, pt: (0, 0)
    grid_spec = pltpu.PrefetchScalarGridSpec(
        num_scalar_prefetch=1,
        grid=(n_s, n_pages // n_pg),
        in_specs=[page_spec(p) for p in range(n_pg)] + [
            pl.BlockSpec((n_c, 2 * half), const),
            pl.BlockSpec((half, 2 * hid_w), const),
            pl.BlockSpec((half, 2 * hid_w), const),
            pl.BlockSpec((hid_w, 2 * HEAD_DIM), const)],
        out_specs=pl.BlockSpec((1, out_rows, HEAD_DIM), lambda b, c, pt: (b, c, 0)),
        scratch_shapes=[pltpu.VMEM((n_c, hid_w), F32)])
    return pl.pallas_call(
        functools.partial(_compress_paged_kernel, n_pg=n_pg),
        out_shape=jax.ShapeDtypeStruct((n_s, n_pages * (PAGE_SIZE // CMP_STRIDE) * n_c, HEAD_DIM), F32),
        grid_spec=grid_spec,
        compiler_params=_cparams(("parallel", "arbitrary")),
        name="compress_paged",
    )(page_table, *([cache_pages[None]] * n_pg), pe8, w1a, w1b, w2c)


def _sample_cmp_kernel(q_ref, kv_ref, bias_ref, im_ref, oc_ref, idx_ref, *, nslc, nsel, pos):
    n_c = 2 * N_KV
    q = (q_ref[0] * ATT_SCALE).astype(BF16)
    kv = kv_ref[0].astype(BF16)
    bias = bias_ref[...]
    valid = bias > 0.5 * NEG_BIAS
    s = _dot_nt(q, kv) + bias
    m = jnp.max(s, axis=-1, keepdims=True)
    e = jnp.where(valid, jnp.exp(s - m), 0.0)
    den = jnp.sum(e, axis=-1, keepdims=True)
    p = e / jnp.where(den > 0.0, den, 1.0)
    oc_ref[0] = _dot(pltpu.roll(p, N_KV, 1).astype(BF16), kv)

    p_hi = p.astype(BF16)
    p_lo = (p - p_hi.astype(F32)).astype(BF16)
    imp_h = _dot(p_hi, im_ref[...]) + _dot(p_lo, im_ref[...])
    lanes = im_ref.shape[1]
    head_grp = lax.broadcasted_iota(jnp.int32, (N_HEADS, lanes), 0) // HPG
    row8 = lax.broadcasted_iota(jnp.int32, (n_c, lanes), 0)
    imp = jnp.zeros((n_c, lanes), F32)
    for g in range(N_KV):
        imp_g = jnp.sum(jnp.where(head_grp == g, imp_h, 0.0), axis=0, keepdims=True)
        imp = imp + jnp.where(row8 == g, imp_g, 0.0)
    jidx = lax.broadcasted_iota(jnp.int32, (n_c, lanes), 1)
    cur = pos // SLC_LEN
    valid_blk = (jidx * SLC_LEN <= pos) & (jidx < nslc)
    forced = (jidx == 0) | (jidx == cur) | (jidx == cur - 1)
    score = jnp.where(valid_blk, imp + jnp.where(forced, FORCE_BONUS, 0.0), -jnp.inf)
    cnt = jnp.zeros((n_c, lanes), jnp.int32)
    for jp in range(nslc):
        col = score[:, jp:jp + 1]
        ahead = (col > score) | ((col == score) & (jidx > jp))
        cnt = cnt + jnp.where(ahead, 1, 0)
    out_lane = lax.broadcasted_iota(jnp.int32, (n_c, HEAD_DIM), 1)
    jf = jidx.astype(F32)
    out = jnp.zeros((n_c, HEAD_DIM), F32)
    for r in range(nsel):
        blk_r = jnp.sum(jnp.where((cnt == r) & valid_blk, jf, 0.0), axis=-1, keepdims=True)
        out = jnp.where(out_lane == r, blk_r, out)
    idx_ref[0] = out.astype(jnp.int32)


def _sample_cmp(q_heads, kvc, bias, imap, *, nslc, nsel, pos):
    n_s, rows, _ = kvc.shape
    lanes = imap.shape[1]
    return pl.pallas_call(
        functools.partial(_sample_cmp_kernel, nslc=nslc, nsel=nsel, pos=pos),
        out_shape=(jax.ShapeDtypeStruct((n_s, N_HEADS, HEAD_DIM), F32),
                   jax.ShapeDtypeStruct((n_s, 2 * N_KV, HEAD_DIM), jnp.int32)),
        grid=(n_s,),
        in_specs=[pl.BlockSpec((1, N_HEADS, HEAD_DIM), lambda b: (b, 0, 0)),
                  pl.BlockSpec((1, rows, HEAD_DIM), lambda b: (b, 0, 0)),
                  pl.BlockSpec((N_HEADS, rows), lambda b: (0, 0)),
                  pl.BlockSpec((rows, lanes), lambda b: (0, 0))],
        out_specs=(pl.BlockSpec((1, N_HEADS, HEAD_DIM), lambda b: (b, 0, 0)),
                   pl.BlockSpec((1, 2 * N_KV, HEAD_DIM), lambda b: (b, 0, 0))),
        compiler_params=_cparams(("parallel",)),
        name="sample_cmp",
    )(q_heads, kvc, bias, imap)


def _sample_sel_win_kernel(idx_ref, pt_ref, *refs, n_slot, pos, n_past_blk, buf_len, thresholds):
    slots = refs[:n_slot]
    q_ref, new_ref, win_ref, tb_ref, oc_ref, gate_ref, o_ref = refs[n_slot:]
    b = pl.program_id(0)
    g = pl.program_id(1)
    n_c = 2 * N_KV
    blk_lanes = SLC_LEN * n_c
    q = (q_ref[0, 0] * ATT_SCALE).astype(BF16)
    tb = tb_ref[0]

    def bias_of(dist):
        d = jnp.maximum(dist, 0)
        out = jnp.zeros(d.shape, F32) + tb[:, 0:1]
        for k in range(1, REL_BUCKETS):
            out = out + jnp.where(d >= thresholds[k], tb[:, k:k + 1] - tb[:, k - 1:k], 0.0)
        return out

    def softmax(s, ok):
        s = jnp.where(ok, s, NEG_BIAS)
        m = jnp.max(s, axis=-1, keepdims=True)
        e = jnp.where(ok, jnp.exp(s - m), 0.0)
        den = jnp.sum(e, axis=-1, keepdims=True)
        return e / jnp.where(den > 0.0, den, 1.0)

    new_rows = new_ref[0, 0]
    s_new = _dot_nt(q, new_rows.astype(BF16))
    lane_new = lax.broadcasted_iota(jnp.int32, s_new.shape, 1)
    bias_new = bias_of(jnp.zeros(s_new.shape, jnp.int32))

    lane = lax.broadcasted_iota(jnp.int32, (n_c, blk_lanes), 1)
    mine = lane % n_c == g
    r_in = lane // n_c
    base = (b * N_KV + g) * n_slot
    s_parts, d_parts, ok_parts = [], [], []
    n_new = 0
    for i in range(n_slot):
        j = idx_ref[base + i]
        dist = jnp.where(j < n_past_blk, pos - j * SLC_LEN, -1) - r_in
        s_parts.append(_dot_nt(q, slots[i][0, 0].astype(BF16)))
        d_parts.append(dist)
        ok_parts.append(mine & (dist >= 0))
        n_new = n_new + jnp.where(j >= n_past_blk, 1, 0)
    s_all = jnp.concatenate(s_parts + [s_new], axis=1)
    ok_all = jnp.concatenate(ok_parts + [lane_new == jnp.where(n_new > 0, 0, -1)], axis=1)
    b_all = jnp.concatenate([bias_of(jnp.concatenate(d_parts, axis=1)), bias_new], axis=1)
    p = softmax(s_all + b_all, ok_all)
    n_old = n_slot * blk_lanes
    p_v = pltpu.roll(p[:, 0:n_old], N_KV, 1).astype(BF16)
    o_s = p[:, n_old:n_old + 1] * new_rows[1:2, :]
    for i in range(n_slot):
        o_s = o_s + _dot(p_v[:, i * blk_lanes:(i + 1) * blk_lanes], slots[i][0, 0].astype(BF16))

    win = win_ref[0].astype(BF16)
    lane_w = lax.broadcasted_iota(jnp.int32, (n_c, buf_len * n_c), 1)
    dist_w = buf_len - lane_w // n_c
    ok_w = (lane_w % n_c == g) & (dist_w >= 0) & (dist_w < WINDOW)
    s_w = jnp.concatenate([_dot_nt(q, win) + bias_of(dist_w), s_new + bias_new], axis=1)
    p = softmax(s_w, jnp.concatenate([ok_w, lane_new == 2], axis=1))
    n_old = buf_len * n_c
    o_w = _dot(pltpu.roll(p[:, 0:n_old], N_KV, 1).astype(BF16), win) + p[:, n_old + 2:n_old + 3] * new_rows[3:4, :]

    gates = jax.nn.sigmoid(gate_ref[0, 0])
    o_ref[0, 0] = gates[:, 0:1] * oc_ref[0, 0] + gates[:, 1:2] * o_s + gates[:, 2:3] * o_w


def _sample_sel_win(idx, page_table, slc_halves, page0, q_g, new_rows, win_rows, tb, o_c, gates, *, pos, nsel):
    n_s, n_pages = page_table.shape
    n_c = 2 * N_KV
    per_page = PAGE_SIZE // SLC_LEN
    n_past_blk = n_pages * per_page
    buf_len = win_rows.shape[1] // n_c
    blk_rows = SLC_LEN * n_c
    thresholds = tuple(int(np.argmax(_rel_bucket_np(np.arange(4 * REL_MAX_DIST)) >= k)) for k in range(REL_BUCKETS))

    def slot_spec(i):
        def index_map(b, g, idx_ref, pt_ref):
            j = jnp.minimum(idx_ref[(b * N_KV + g) * nsel + i], n_past_blk - 1)
            return (page0 + pt_ref[b, j // per_page], j % per_page, 0, 0)
        return pl.BlockSpec((1, 1, blk_rows, HEAD_DIM), index_map)

    grp = lambda b, g, idx_ref, pt_ref: (b, g, 0, 0)
    grid_spec = pltpu.PrefetchScalarGridSpec(
        num_scalar_prefetch=2,
        grid=(n_s, N_KV),
        in_specs=[slot_spec(i) for i in range(nsel)] + [
            pl.BlockSpec((1, 1, n_c, HEAD_DIM), grp),
            pl.BlockSpec((1, 1, HEAD_DIM, HEAD_DIM), grp),
            pl.BlockSpec((1, buf_len * n_c, HEAD_DIM), lambda b, g, idx_ref, pt_ref: (b, 0, 0)),
            pl.BlockSpec((1, n_c, HEAD_DIM), lambda b, g, idx_ref, pt_ref: (g, 0, 0)),
            pl.BlockSpec((1, 1, n_c, HEAD_DIM), grp),
            pl.BlockSpec((1, 1, n_c, HEAD_DIM), grp)],
        out_specs=pl.BlockSpec((1, 1, n_c, HEAD_DIM), grp))
    return pl.pallas_call(
        functools.partial(_sample_sel_win_kernel, n_slot=nsel, pos=pos, n_past_blk=n_past_blk, buf_len=buf_len,
                          thresholds=thresholds),
        out_shape=jax.ShapeDtypeStruct((n_s, N_KV, n_c, HEAD_DIM), F32),
        grid_spec=grid_spec,
        compiler_params=_cparams(("parallel", "parallel")),
        name="sample_sel_win",
    )(idx, page_table, *([slc_halves] * nsel), q_g, new_rows, win_rows, tb, o_c, gates)


def _sample_nsa(proj_s, cmp_pages, slc_halves, page0, win_buf, page_table, pe, w1, w2, rel_table):
    n_s, n_pages = page_table.shape
    past = n_pages * PAGE_SIZE
    pos = past
    n_c = 2 * N_KV
    d_q = N_HEADS * HEAD_DIM
    kvw = n_c * HEAD_DIM
    nseg = past // CMP_STRIDE
    nslc = -(-(past + 1) // SLC_LEN)
    nsel = min(N_SEL, nslc)
    kvc = _compress_paged(cmp_pages, page_table, page0, pe, w1, w2)
    x = np.arange(nseg)[:, None]
    c = np.arange(n_c)[None, :]
    dist = pos - ((x - 1) * CMP_STRIDE + CMP_LEN - 1)
    head_grp = np.arange(N_HEADS)[:, None, None] // HPG
    ok = ((x >= 1) & (dist >= 0))[None] & (c[None] == head_grp)
    bias = jnp.where(ok, jnp.take(rel_table.T, _rel_bucket_np(np.broadcast_to(dist, (nseg, n_c))), axis=1),
                     NEG_BIAS).reshape(N_HEADS, nseg * n_c)
    lanes = -(-nslc // HEAD_DIM) * HEAD_DIM
    imap = np.zeros((nseg, n_c, lanes), np.float32)
    imap[1:, :, :nslc] = _cmp_to_slc_np(nseg - 1, nslc)[:, None, :]
    q_heads = proj_s[:, :d_q].reshape(n_s, N_HEADS, HEAD_DIM)
    o_c, idx = _sample_cmp(q_heads, kvc, bias, jnp.asarray(imap.reshape(nseg * n_c, lanes)).astype(BF16),
                           nslc=nslc, nsel=nsel, pos=pos)
    pad_heads = lambda a: jnp.pad(a.reshape(n_s, N_KV, HPG, -1), ((0, 0), (0, 0), (0, n_c - HPG), (0, 0)))
    kv_new = proj_s[:, d_q:d_q + 3 * kvw].reshape(n_s, 3, 2, N_KV, HEAD_DIM)
    new_rows = kv_new[:, 1:3].transpose(0, 3, 1, 2, 4).reshape(n_s, N_KV, 4, HEAD_DIM)
    new_rows = jnp.pad(new_rows, ((0, 0), (0, 0), (0, HEAD_DIM - 4), (0, 0)))
    gates = proj_s[:, d_q + 3 * kvw:].reshape(n_s, N_KV, HEAD_DIM)[:, :, :3 * HPG].reshape(n_s, N_KV, 3, HPG)
    gates = jnp.pad(gates.transpose(0, 1, 3, 2), ((0, 0), (0, 0), (0, n_c - HPG), (0, HEAD_DIM - 3)))
    tb = jnp.pad(rel_table.T.reshape(N_KV, HPG, REL_BUCKETS), ((0, 0), (0, n_c - HPG), (0, HEAD_DIM - REL_BUCKETS)))
    o = _sample_sel_win(idx[:, :N_KV, :nsel].reshape(-1), page_table, slc_halves, page0,
                        pad_heads(proj_s[:, :d_q]), new_rows, win_buf.reshape(n_s, -1, HEAD_DIM), tb,
                        pad_heads(o_c), gates, pos=pos, nsel=nsel)
    new_win = jnp.concatenate([win_buf, kv_new[:, 2][:, None]], axis=1)[:, -min(WINDOW, win_buf.shape[1] + 1):]
    return o[:, :, :HPG].reshape(n_s, d_q), kv_new, new_win


def _pad_rows(a, rows):
    return jnp.pad(a, ((0, rows - a.shape[0]),) + ((0, 0),) * (a.ndim - 1))


def _nsa_w_in_layout(w):
    d = w.shape[0]
    n_main = N_HEADS * HEAD_DIM + 6 * N_KV * HEAD_DIM
    wg = w[:, n_main:].reshape(d, 3, N_KV, HPG).transpose(0, 2, 1, 3).reshape(d, N_KV, 3 * HPG)
    wg = jnp.pad(wg, ((0, 0), (0, 0), (0, HEAD_DIM - 3 * HPG))).reshape(d, N_KV * HEAD_DIM)
    return jnp.concatenate([w[:, :n_main], wg], axis=1).astype(BF16)


def kernel(x_prompt, x_sample, cache_cmp_kv, cache_slc_kv, state_win_kv, page_table, c_prompt, c_sample,
           ada_w, ada_b, ln_g, ln_b, ffn_pre_w_in, ffn_pre_w_out, ffn_post_w_in, ffn_post_w_out,
           gmlp_w_in, gmlp_ln_g, gmlp_ln_b, gmlp_w_s, gmlp_b_s, gmlp_w_out,
           nsa_w_in, nsa_cmp_pe, nsa_cmp_w1, nsa_cmp_w2, nsa_w_out, rel_table):
    n_b, t, d = x_prompt.shape
    n_s = x_sample.shape[0]
    n_phys = cache_cmp_kv.shape[1]
    d_q = N_HEADS * HEAD_DIM
    kvw = 2 * N_KV * HEAD_DIM
    tm_p = min(512, t)
    tm_f = min(512, t)
    tm_mm = min(1024, t)
    tm_g = min(512, t)

    xp = x_prompt.reshape(n_b * t, d)
    place = lambda a: jnp.pad(a, ((n_b, SAMPLE_ROWS - n_b - n_s), (0, 0)))
    xs = place(x_sample.reshape(n_s, d))
    c_all = _pad_rows(jnp.concatenate([c_prompt, c_sample], axis=0), SAMPLE_ROWS)
    mod = _ada(c_all, ada_w, ada_b)
    mod_rows = mod.reshape(DEPTH * SAMPLE_ROWS * N_ADA, 1, d)

    def prompt_mod(layer, tm):
        return lambda k: pl.BlockSpec(
            (1, 1, d), lambda i, *_: ((layer * SAMPLE_ROWS + i // (t // tm)) * N_ADA + k, 0, 0))

    def sample_mod(layer):
        return lambda k: pl.BlockSpec((1, SAMPLE_ROWS, d), lambda i, *_: (layer, 0, k))

    band_t, bias_sel, bias_win = _bias_tables(rel_table, t)
    imap_t = jnp.asarray(_cmp_to_slc_np(t // CMP_STRIDE, t // SLC_LEN).T).astype(BF16)
    cmp_pages = cache_cmp_kv.reshape(-1, PAGE_SIZE * 2 * N_KV, HEAD_DIM)
    slc_halves = cache_slc_kv.reshape(-1, PAGE_SIZE // SLC_LEN, SLC_LEN * 2 * N_KV, HEAD_DIM)

    cmp_p, cmp_s, slc_p, slc_s, win_p, win_s, gv_s = [], [], [], [], [], [], []
    for i in range(DEPTH):
        ms = sample_mod(i)
        w_in = ffn_pre_w_in[i].astype(BF16)
        w_out = ffn_pre_w_out[i].astype(BF16)
        xp = _ffn(xp, mod_rows, prompt_mod(i, tm_f), (0, 1, 2), w_in, w_out, ln_g[i, 0], ln_b[i, 0], tm=tm_f)
        xs = _ffn(xs, mod, ms, (0, 1, 2), w_in, w_out, ln_g[i, 0], ln_b[i, 0], tm=SAMPLE_ROWS)

        if i % 2 == 0:
            a = i // 2
            gw_in = gmlp_w_in[a].astype(BF16)
            gw_out = gmlp_w_out[a].astype(BF16)
            xp, = _gmlp(xp, mod_rows, prompt_mod(i, tm_g), (3, 4, 5), gw_in, gmlp_ln_g[a], gmlp_ln_b[a], gmlp_w_s[a],
                        gmlp_b_s[a], gw_out, ln_g[i, 1], ln_b[i, 1], tm=tm_g, chunk=min(CHUNK, t), emit_vn=False)
            xs, vn_s = _gmlp(xs, mod, ms, (3, 4, 5), gw_in, gmlp_ln_g[a], gmlp_ln_b[a], gmlp_w_s[a], gmlp_b_s[a],
                             gw_out, ln_g[i, 1], ln_b[i, 1], tm=SAMPLE_ROWS, chunk=1, emit_vn=True)
            gv_s.append(vn_s[n_b:n_b + n_s].reshape(n_s, 1, -1))
        else:
            a = i // 2
            nw_in = _nsa_w_in_layout(nsa_w_in[a])
            nw_out = nsa_w_out[a].astype(BF16)
            w1 = nsa_cmp_w1[a].astype(BF16)
            w2 = nsa_cmp_w2[a].astype(BF16)
            proj = _mod_mm(xp, mod_rows, prompt_mod(i, tm_mm), (3, 4), nw_in, tm=tm_mm, tn=512)
            kvc = _compress(proj, n_b, t, d_q // HEAD_DIM, nsa_cmp_pe[a], w1, w2)
            o = _nsa_attn(proj, kvc, band_t, imap_t, bias_sel, bias_win, n_b, t)
            xp = _proj_res(o, nw_out, xp, mod_rows, prompt_mod(i, tm_p), 5, ln_g[i, 1], ln_b[i, 1], tm=tm_p)
            kv_all = proj[:, d_q:d_q + 3 * kvw].reshape(n_b, t, 3, 2, N_KV, HEAD_DIM)
            cmp_p.append(kv_all[:, :, 0])
            slc_p.append(kv_all[:, :, 1])
            win_p.append(kv_all[:, -min(WINDOW, t):, 2])
            proj_s = _mod_mm(xs, mod, ms, (3, 4), nw_in, tm=SAMPLE_ROWS, tn=512)[n_b:n_b + n_s]
            o_samp, kv_new, new_win = _sample_nsa(proj_s, cmp_pages, slc_halves, a * n_phys, state_win_kv[a],
                                                  page_table, nsa_cmp_pe[a], w1, w2, rel_table)
            xs = _proj_res(place(o_samp).astype(BF16), nw_out, xs, mod, ms, 5, ln_g[i, 1], ln_b[i, 1],
                           tm=SAMPLE_ROWS)
            cmp_s.append(kv_new[:, 0][:, None])
            slc_s.append(kv_new[:, 1][:, None])
            win_s.append(new_win)

        w_in = ffn_post_w_in[i].astype(BF16)
        w_out = ffn_post_w_out[i].astype(BF16)
        xp = _ffn(xp, mod_rows, prompt_mod(i, tm_f), (6, 7, 8), w_in, w_out, ln_g[i, 2], ln_b[i, 2], tm=tm_f)
        xs = _ffn(xs, mod, ms, (6, 7, 8), w_in, w_out, ln_g[i, 2], ln_b[i, 2], tm=SAMPLE_ROWS)

    return (xp.reshape(n_b, t, d), xs[n_b:n_b + n_s].reshape(n_s, 1, d),
            jnp.stack(cmp_p), jnp.stack(cmp_s), jnp.stack(slc_p), jnp.stack(slc_s),
            jnp.stack(win_p), jnp.stack(win_s), jnp.stack(gv_s))
```

```python
import functools
import math

import numpy as np
import jax
import jax.numpy as jnp
from jax import lax
from jax.experimental import pallas as pl
from jax.experimental.pallas import tpu as pltpu

F32 = jnp.float32
BF16 = jnp.bfloat16

DEPTH = 4
N_ADA = 9
N_HEADS = 16
HEAD_DIM = 128
N_KV = 4
HPG = N_HEADS // N_KV
GMLP_GROUPS = 16
CHUNK = 128
PAGE_SIZE = 128
CMP_LEN = 32
CMP_STRIDE = 16
SLC_LEN = 64
N_SEL = 16
WINDOW = 512
Q_BLOCK = 128
REL_BUCKETS = 32
REL_MAX_DIST = 128
LN_EPS = 1e-5
DN_ALPHA = (2 * DEPTH) ** 0.25
FORCE_BONUS = 1e4
ATT_SCALE = HEAD_DIM ** -0.5

SAMPLE_ROWS = 16
NEG_MASK = -30000.0
NEG_BIAS = -1e30
SEL_MASK = 32768.0
KEY_PAD = WINDOW
SEL_CHUNK = 512
VMEM_LIMIT = 56 * 1024 * 1024


def _cparams(sem):
    return pltpu.CompilerParams(dimension_semantics=sem, vmem_limit_bytes=VMEM_LIMIT)


def _dot(a, b):
    return jnp.dot(a, b, preferred_element_type=F32)


def _dot_nt(a, b):
    return lax.dot_general(a, b, (((1,), (1,)), ((), ())), preferred_element_type=F32)


def _ln_rows(z, g, b):
    mu = jnp.mean(z, axis=-1, keepdims=True)
    zc = z - mu
    var = jnp.mean(zc * zc, axis=-1, keepdims=True)
    return zc * lax.rsqrt(var + LN_EPS) * g + b


def _silu(x):
    return x * jax.nn.sigmoid(x)


def _ada_kernel(c_ref, w_ref, b_ref, o_ref):
    h = _silu(c_ref[...]).astype(BF16)
    o_ref[0] = _dot(h, w_ref[0].astype(BF16)) + b_ref[0]


def _ada(c_all, ada_w, ada_b, tn=1024):
    depth, d, n = ada_w.shape
    r = c_all.shape[0]
    return pl.pallas_call(
        _ada_kernel,
        out_shape=jax.ShapeDtypeStruct((depth, r, n), F32),
        grid=(depth, n // tn),
        in_specs=[pl.BlockSpec((r, d), lambda l, j: (0, 0)),
                  pl.BlockSpec((1, d, tn), lambda l, j: (l, 0, j)),
                  pl.BlockSpec((1, 1, tn), lambda l, j: (l, 0, j))],
        out_specs=pl.BlockSpec((1, r, tn), lambda l, j: (l, 0, j)),
        compiler_params=_cparams(("parallel", "parallel")),
        name="ada",
    )(c_all, ada_w, ada_b.reshape(depth, 1, n))


def _ffn_kernel(x_ref, sh_ref, sc_ref, gt_ref, wg_ref, wu_ref, wo_ref, lg_ref, lb_ref, o_ref, h_ref):
    j = pl.program_id(1)

    @pl.when(j == 0)
    def _():
        h_ref[...] = (x_ref[...] * (1.0 + sc_ref[0]) + sh_ref[0]).astype(BF16)
        o_ref[...] = jnp.zeros_like(o_ref)

    h = h_ref[...]
    g = _dot(h, wg_ref[...])
    u = _dot(h, wu_ref[...])
    a = (_silu(g) * u).astype(BF16)
    o_ref[...] += _dot(a, wo_ref[...])

    @pl.when(j == pl.num_programs(1) - 1)
    def _():
        z = DN_ALPHA * x_ref[...] + (1.0 + gt_ref[0]) * (0.5 * o_ref[...])
        o_ref[...] = _ln_rows(z, lg_ref[...], lb_ref[...])


def _ffn(x, mod, mod_spec, slots, w_in, w_out, ln_g, ln_b, *, layer, tm, tf=512):
    m, d = x.shape
    dff = w_out.shape[1]
    nf = dff // tf
    vec_spec = pl.BlockSpec((1, d), lambda i, j: (0, 0))
    return pl.pallas_call(
        _ffn_kernel,
        out_shape=jax.ShapeDtypeStruct((m, d), F32),
        grid=(m // tm, nf),
        in_specs=[pl.BlockSpec((tm, d), lambda i, j: (i, 0)),
                  mod_spec(slots[0]), mod_spec(slots[1]), mod_spec(slots[2]),
                  pl.BlockSpec((None, d, tf), lambda i, j: (layer, 0, j)),
                  pl.BlockSpec((None, d, tf), lambda i, j: (layer, 0, j + nf)),
                  pl.BlockSpec((None, tf, d), lambda i, j: (layer, j, 0)),
                  vec_spec, vec_spec],
        out_specs=pl.BlockSpec((tm, d), lambda i, j: (i, 0)),
        scratch_shapes=[pltpu.VMEM((tm, d), BF16)],
        compiler_params=_cparams(("parallel", "arbitrary")),
        name="ffn",
    )(x, mod, mod, mod, w_in, w_in, w_out, ln_g.reshape(1, d), ln_b.reshape(1, d))


def _mm_kernel(x_ref, sh_ref, sc_ref, w_ref, o_ref, h_ref):
    @pl.when(pl.program_id(1) == 0)
    def _():
        h_ref[...] = (x_ref[...] * (1.0 + sc_ref[0]) + sh_ref[0]).astype(BF16)

    o_ref[...] = _dot(h_ref[...], w_ref[...])


def _mod_mm(x, mod, mod_spec, slots, w, *, layer, tm, tn):
    m, d = x.shape
    n = w.shape[2]
    return pl.pallas_call(
        _mm_kernel,
        out_shape=jax.ShapeDtypeStruct((m, n), F32),
        grid=(m // tm, n // tn),
        in_specs=[pl.BlockSpec((tm, d), lambda i, j: (i, 0)),
                  mod_spec(slots[0]), mod_spec(slots[1]),
                  pl.BlockSpec((None, d, tn), lambda i, j: (layer, 0, j))],
        out_specs=pl.BlockSpec((tm, tn), lambda i, j: (i, j)),
        scratch_shapes=[pltpu.VMEM((tm, d), BF16)],
        compiler_params=_cparams(("parallel", "arbitrary")),
        name="mod_mm",
    )(x, mod, mod, w)


GMLP_GROUPS_PER_STEP = 2


def _gmlp_kernel(x_ref, sh_ref, sc_ref, gt_ref, win_ref, vlg_ref, vlb_ref, ws_ref, bs_ref,
                 wo_ref, lg_ref, lb_ref, o_ref, *rest, chunk, gw, emit_vn):
    if emit_vn:
        vn_ref, h_ref, v_ref, mu_ref, rs_ref = rest
    else:
        h_ref, v_ref, mu_ref, rs_ref = rest
    k = pl.program_id(1)
    ns = pl.num_programs(1) // 2
    tm = x_ref.shape[0]
    gdim = ns * v_ref.shape[2]

    @pl.when(k == 0)
    def _():
        h_ref[...] = (x_ref[...] * (1.0 + sc_ref[0]) + sh_ref[0]).astype(BF16)
        o_ref[...] = jnp.zeros_like(o_ref)

    y = jax.nn.gelu(_dot(h_ref[...], win_ref[...]))

    @pl.when(k < ns)
    def _():
        v_ref[k] = y

    @pl.when(k == ns)
    def _():
        tot = jnp.zeros((tm, 1), F32)
        for c in range(ns):
            tot = tot + jnp.sum(v_ref[c], axis=-1, keepdims=True)
        mu = tot / gdim
        sq = jnp.zeros((tm, 1), F32)
        for c in range(ns):
            dv = v_ref[c] - mu
            sq = sq + jnp.sum(dv * dv, axis=-1, keepdims=True)
        mu_ref[...] = mu
        rs_ref[...] = lax.rsqrt(sq / gdim + LN_EPS)

    @pl.when(k >= ns)
    def _():
        vn = (v_ref[k - ns] - mu_ref[...]) * rs_ref[...] * vlg_ref[...] + vlb_ref[...]
        if emit_vn:
            vn_ref[...] = vn
        parts = []
        for gi in range(GMLP_GROUPS_PER_STEP):
            vn_g = vn[:, gi * gw:(gi + 1) * gw]
            if chunk == 1:
                s_g = ws_ref[gi, 0:1, 0:1] * vn_g + bs_ref[gi, 0:1, 0:1]
            else:
                row = lax.broadcasted_iota(jnp.int32, (chunk, chunk), 0)
                col = lax.broadcasted_iota(jnp.int32, (chunk, chunk), 1)
                w_tri = jnp.where(col <= row, ws_ref[gi], 0.0).astype(BF16)
                vb = vn_g.astype(BF16)
                s_g = jnp.concatenate(
                    [_dot(w_tri, vb[c * chunk:(c + 1) * chunk]) + bs_ref[gi]
                     for c in range(tm // chunk)], axis=0)
            parts.append(s_g)
        a = (y * jnp.concatenate(parts, axis=1)).astype(BF16)
        o_ref[...] += _dot(a, wo_ref[...])

    @pl.when(k == 2 * ns - 1)
    def _():
        z = DN_ALPHA * x_ref[...] + (1.0 + gt_ref[0]) * o_ref[...]
        o_ref[...] = _ln_rows(z, lg_ref[...], lb_ref[...])


def _gmlp(x, mod, mod_spec, slots, w_in, v_ln_g, v_ln_b, w_s, b_s, w_out, ln_g, ln_b, *, layer, tm, chunk,
          emit_vn):
    m, d = x.shape
    gdim = w_out.shape[1]
    gw = gdim // GMLP_GROUPS
    gw2 = gw * GMLP_GROUPS_PER_STEP
    ns = GMLP_GROUPS // GMLP_GROUPS_PER_STEP
    vec_spec = pl.BlockSpec((1, d), lambda i, k: (0, 0))
    second = lambda k: jnp.maximum(k - ns, 0)
    out_shape = [jax.ShapeDtypeStruct((m, d), F32)]
    out_specs = [pl.BlockSpec((tm, d), lambda i, k: (i, 0))]
    if emit_vn:
        out_shape.append(jax.ShapeDtypeStruct((m, gdim), F32))
        out_specs.append(pl.BlockSpec((tm, gw2), lambda i, k: (i, second(k))))
    return pl.pallas_call(
        functools.partial(_gmlp_kernel, chunk=chunk, gw=gw, emit_vn=emit_vn),
        out_shape=tuple(out_shape),
        grid=(m // tm, 2 * ns),
        in_specs=[pl.BlockSpec((tm, d), lambda i, k: (i, 0)),
                  mod_spec(slots[0]), mod_spec(slots[1]), mod_spec(slots[2]),
                  pl.BlockSpec((None, d, gw2), lambda i, k: (layer, 0, jnp.where(k < ns, k + ns, k - ns))),
                  pl.BlockSpec((1, gw2), lambda i, k: (0, second(k))),
                  pl.BlockSpec((1, gw2), lambda i, k: (0, second(k))),
                  pl.BlockSpec((GMLP_GROUPS_PER_STEP, CHUNK, CHUNK), lambda i, k: (second(k), 0, 0)),
                  pl.BlockSpec((GMLP_GROUPS_PER_STEP, CHUNK, 1), lambda i, k: (second(k), 0, 0)),
                  pl.BlockSpec((None, gw2, d), lambda i, k: (layer, second(k), 0)),
                  vec_spec, vec_spec],
        out_specs=tuple(out_specs),
        scratch_shapes=[pltpu.VMEM((tm, d), BF16), pltpu.VMEM((ns, tm, gw2), F32),
                        pltpu.VMEM((tm, 1), F32), pltpu.VMEM((tm, 1), F32)],
        compiler_params=_cparams(("parallel", "arbitrary")),
        name="gmlp",
    )(x, mod, mod, mod, w_in, v_ln_g.reshape(1, gdim), v_ln_b.reshape(1, gdim), w_s,
      b_s.reshape(GMLP_GROUPS, CHUNK, 1), w_out, ln_g.reshape(1, d), ln_b.reshape(1, d))


def _compress_kernel(r_ref, pe_ref, w1_ref, w2_ref, o_ref, *, nseg):
    half = CMP_STRIDE * HEAD_DIM
    a = jnp.concatenate(
        [r_ref[pl.ds(r, nseg, stride=CMP_STRIDE), :].astype(BF16) for r in range(CMP_STRIDE)], axis=1)
    p0 = _dot(a, w1_ref[0, 0:half, :])
    p1 = _dot(a, w1_ref[0, half:2 * half, :])
    pe = jnp.broadcast_to(pe_ref[0], (8, 2 * half)).astype(BF16)
    peh = _dot(pe, w1_ref[0])[0:1]
    hid = peh + p0 + pltpu.roll(p1, nseg - 1, 0)
    o_ref[0, 0, 0] = _dot(jax.nn.gelu(hid).astype(BF16), w2_ref[0])


def _compress(rows2d, n_batch, t, col0, pe, w1, w2):
    nseg = t // CMP_STRIDE
    return pl.pallas_call(
        functools.partial(_compress_kernel, nseg=nseg),
        out_shape=jax.ShapeDtypeStruct((n_batch, 2, N_KV, nseg, HEAD_DIM), F32),
        grid=(n_batch, 2, N_KV),
        in_specs=[pl.BlockSpec((t, HEAD_DIM), lambda b, kv, g: (b, col0 + kv * N_KV + g)),
                  pl.BlockSpec((1, 1, CMP_LEN * HEAD_DIM), lambda b, kv, g: (kv, 0, 0)),
                  pl.BlockSpec((1, CMP_LEN * HEAD_DIM, w1.shape[2]), lambda b, kv, g: (kv, 0, 0)),
                  pl.BlockSpec((1, w2.shape[1], HEAD_DIM), lambda b, kv, g: (kv, 0, 0))],
        out_specs=pl.BlockSpec((1, 1, 1, nseg, HEAD_DIM), lambda b, kv, g: (b, kv, g, 0, 0)),
        compiler_params=_cparams(("parallel", "parallel", "parallel")),
        name="compress",
    )(rows2d, pe.reshape(2, 1, CMP_LEN * HEAD_DIM), w1, w2)


def _softmax_update_t(state, s, vt):
    m, l, acc = state
    m_new = jnp.maximum(m, jnp.max(s, axis=0, keepdims=True))
    alpha = jnp.exp(m - m_new)
    p = jnp.exp(s - m_new)
    l = alpha * l + jnp.sum(p, axis=0, keepdims=True)
    acc = alpha * acc + _dot(vt, p.astype(BF16))
    return m_new, l, acc


def _nsa_attn_kernel(q_ref, kc_ref, vc_ref, band_ref, imap_ref, ks_ref, vs_ref, kw_ref, vw_ref,
                     bsel_ref, bwin_ref, gate_ref, o_ref, ksa, vst, kwa, vwt, *, nslc, nsel):
    qb = pl.program_id(2)
    tq = Q_BLOCK
    cols = HPG * tq
    t = ks_ref.shape[0]
    nseg = kc_ref.shape[3]

    @pl.when(qb == 0)
    def _():
        lane_p = lax.broadcasted_iota(jnp.int32, (KEY_PAD, HEAD_DIM), 1)
        pad_mark = jnp.where(lane_p == HEAD_DIM - 1, 1.0, 0.0).astype(BF16)
        key_blk = lax.broadcasted_iota(jnp.int32, (t, HEAD_DIM), 0) // SLC_LEN
        lane = lax.broadcasted_iota(jnp.int32, (t, HEAD_DIM), 1)
        for ref in (ksa, kwa):
            ref[0:KEY_PAD, 0:HEAD_DIM] = jnp.zeros((KEY_PAD, HEAD_DIM), BF16)
            ref[0:KEY_PAD, HEAD_DIM:2 * HEAD_DIM] = pad_mark
        ksa[KEY_PAD:KEY_PAD + t, 0:HEAD_DIM] = ks_ref[...].astype(BF16)
        ksa[KEY_PAD:KEY_PAD + t, HEAD_DIM:2 * HEAD_DIM] = jnp.where(lane == key_blk, 1.0, 0.0).astype(BF16)
        kwa[KEY_PAD:KEY_PAD + t, 0:HEAD_DIM] = kw_ref[...].astype(BF16)
        kwa[KEY_PAD:KEY_PAD + t, HEAD_DIM:2 * HEAD_DIM] = jnp.zeros((t, HEAD_DIM), BF16)
        vst[:, 0:KEY_PAD] = jnp.zeros((HEAD_DIM, KEY_PAD), BF16)
        vwt[:, 0:KEY_PAD] = jnp.zeros((HEAD_DIM, KEY_PAD), BF16)
        for c in range(t // tq):
            vst[:, KEY_PAD + c * tq:KEY_PAD + (c + 1) * tq] = vs_ref[c * tq:(c + 1) * tq, :].T.astype(BF16)
            vwt[:, KEY_PAD + c * tq:KEY_PAD + (c + 1) * tq] = vw_ref[c * tq:(c + 1) * tq, :].T.astype(BF16)

    q_t = jnp.concatenate(
        [(q_ref[:, h * HEAD_DIM:(h + 1) * HEAD_DIM] * ATT_SCALE).T for h in range(HPG)], axis=1).astype(BF16)

    off = pl.multiple_of(nseg - (tq // CMP_STRIDE) * qb, 8)
    bias = jnp.concatenate([band_ref[h, pl.ds(off, nseg), :] for h in range(HPG)], axis=1)
    valid = bias > 0.5 * NEG_BIAS
    s = _dot(kc_ref[0, 0, 0].astype(BF16), q_t) + bias
    m = jnp.max(s, axis=0, keepdims=True)
    e = jnp.where(valid, jnp.exp(s - m), 0.0)
    den = jnp.sum(e, axis=0, keepdims=True)
    p = e / jnp.where(den > 0.0, den, 1.0)
    o_c = _dot(vc_ref[0, 0, 0].T.astype(BF16), p.astype(BF16))

    psum = p[:, 0:tq]
    for h in range(1, HPG):
        psum = psum + p[:, h * tq:(h + 1) * tq]
    p_hi = psum.astype(BF16)
    p_lo = (psum - p_hi.astype(F32)).astype(BF16)
    imap = imap_ref[...]
    imp = _dot(imap, p_hi) + _dot(imap, p_lo)
    jidx = lax.broadcasted_iota(jnp.int32, (nslc, tq), 0)
    qpos = qb * tq + lax.broadcasted_iota(jnp.int32, (nslc, tq), 1)
    cur = qpos // SLC_LEN
    valid_blk = jidx * SLC_LEN <= qpos
    forced = (jidx == 0) | (jidx == cur) | (jidx == cur - 1)
    score = jnp.where(valid_blk, imp + jnp.where(forced, FORCE_BONUS, 0.0), -jnp.inf)
    slab = 8
    slabs = [score[v:v + slab] for v in range(0, nslc, slab)]
    cnts = [jnp.zeros((slab, tq), jnp.int32) for _ in slabs]
    jloc = lax.broadcasted_iota(jnp.int32, (slab, tq), 0)
    for jp in range(nslc):
        row = score[jp:jp + 1, :]
        for v, blk in enumerate(slabs):
            if v * slab > jp:
                ahead = row >= blk
            elif (v + 1) * slab <= jp:
                ahead = row > blk
            else:
                ahead = (row > blk) | ((row == blk) & (jloc > jp - v * slab))
            cnts[v] = cnts[v] + jnp.where(ahead, 1, 0)
    cnt = jnp.concatenate(cnts, axis=0)
    unsel = jnp.where((cnt < nsel) & valid_blk, 0.0, -SEL_MASK)

    marker_row = lax.broadcasted_iota(jnp.int32, (8, cols), 0) == 7
    aug = jnp.concatenate([jnp.concatenate([unsel] * HPG, axis=1),
                           jnp.zeros((HEAD_DIM - nslc - 8, cols), F32),
                           jnp.where(marker_row, -SEL_MASK, 0.0)], axis=0)
    rhs = jnp.concatenate([q_t, aug.astype(BF16)], axis=0)

    last = pl.multiple_of((qb + 1) * tq, tq)
    s = _dot(ksa[pl.ds(last, SEL_CHUNK), :], rhs) + bsel_ref[0]
    m = jnp.max(s, axis=0, keepdims=True)
    p = jnp.exp(s - m)
    st = (m, jnp.sum(p, axis=0, keepdims=True), _dot(vst[:, pl.ds(last, SEL_CHUNK)], p.astype(BF16)))

    def far_start(c):
        return pl.multiple_of(jnp.maximum(last - (c + 1) * SEL_CHUNK, 0), tq)

    def far_logits(c):
        return _dot(ksa[pl.ds(far_start(c), SEL_CHUNK), :], rhs)

    def far_body(c, carry):
        state, s_c = carry
        s_next = far_logits(c + 1)
        return _softmax_update_t(state, s_c, vst[:, pl.ds(far_start(c), SEL_CHUNK)]), s_next

    (_, l_s, acc_s), _ = lax.fori_loop(0, qb // (SEL_CHUNK // tq), far_body, (st, far_logits(0)))
    o_s = acc_s / l_s

    first = pl.multiple_of(qb * tq, tq)
    s = _dot(kwa[pl.ds(first, WINDOW + tq), :], rhs) + bwin_ref[0]
    p = jnp.exp(s - jnp.max(s, axis=0, keepdims=True))
    o_w = _dot(vwt[:, pl.ds(first, WINDOW + tq)], p.astype(BF16)) / jnp.sum(p, axis=0, keepdims=True)

    gates_t = jax.nn.sigmoid(gate_ref[...]).T
    outs = []
    for h in range(HPG):
        hs = slice(h * tq, (h + 1) * tq)
        merged = (gates_t[h:h + 1, :] * o_c[:, hs] + gates_t[HPG + h:HPG + h + 1, :] * o_s[:, hs]
                  + gates_t[2 * HPG + h:2 * HPG + h + 1, :] * o_w[:, hs])
        outs.append(merged.T)
    o_ref[...] = jnp.concatenate(outs, axis=1).astype(o_ref.dtype)


def _nsa_attn(proj, kvc, band_t, imap_t, bias_sel, bias_win, n_batch, t):
    nqb = t // Q_BLOCK
    nseg = t // CMP_STRIDE
    nslc = t // SLC_LEN
    nsel = min(N_SEL, nslc)
    d_q = N_HEADS * HEAD_DIM
    gcols = HPG * HEAD_DIM
    kv0 = d_q // HEAD_DIM
    gate0 = kv0 + 6 * N_KV

    def kv_spec(branch, kv):
        return pl.BlockSpec((t, HEAD_DIM), lambda b, g, i: (b, kv0 + (branch * 2 + kv) * N_KV + g))

    return pl.pallas_call(
        functools.partial(_nsa_attn_kernel, nslc=nslc, nsel=nsel),
        out_shape=jax.ShapeDtypeStruct((n_batch * t, d_q), BF16),
        grid=(n_batch, N_KV, nqb),
        in_specs=[pl.BlockSpec((Q_BLOCK, gcols), lambda b, g, i: (b * nqb + i, g)),
                  pl.BlockSpec((1, 1, 1, nseg, HEAD_DIM), lambda b, g, i: (b, 0, g, 0, 0)),
                  pl.BlockSpec((1, 1, 1, nseg, HEAD_DIM), lambda b, g, i: (b, 1, g, 0, 0)),
                  pl.BlockSpec((HPG, 2 * nseg, Q_BLOCK), lambda b, g, i: (g, 0, 0)),
                  pl.BlockSpec((nslc, nseg), lambda b, g, i: (0, 0)),
                  kv_spec(1, 0), kv_spec(1, 1), kv_spec(2, 0), kv_spec(2, 1),
                  pl.BlockSpec((1, SEL_CHUNK, HPG * Q_BLOCK), lambda b, g, i: (g, 0, 0)),
                  pl.BlockSpec((1, WINDOW + Q_BLOCK, HPG * Q_BLOCK), lambda b, g, i: (g, 0, 0)),
                  pl.BlockSpec((Q_BLOCK, HEAD_DIM), lambda b, g, i: (b * nqb + i, gate0 + g))],
        out_specs=pl.BlockSpec((Q_BLOCK, gcols), lambda b, g, i: (b * nqb + i, g)),
        scratch_shapes=[pltpu.VMEM((KEY_PAD + t, 2 * HEAD_DIM), BF16), pltpu.VMEM((HEAD_DIM, KEY_PAD + t), BF16),
                        pltpu.VMEM((KEY_PAD + t, 2 * HEAD_DIM), BF16), pltpu.VMEM((HEAD_DIM, KEY_PAD + t), BF16)],
        compiler_params=_cparams(("parallel", "parallel", "arbitrary")),
        name="nsa_attn",
    )(proj, kvc, kvc, band_t, imap_t, proj, proj, proj, proj, bias_sel, bias_win, proj)


def _proj_res_kernel(a_ref, w_ref, x_ref, gt_ref, lg_ref, lb_ref, o_ref):
    y = _dot(a_ref[...], w_ref[...])
    z = DN_ALPHA * x_ref[...] + (1.0 + gt_ref[0]) * y
    o_ref[...] = _ln_rows(z, lg_ref[...], lb_ref[...])


def _proj_res(a, w, x, mod, mod_spec, gate_slot, ln_g, ln_b, *, layer, tm):
    m, d = x.shape
    k = a.shape[1]
    vec_spec = pl.BlockSpec((1, d), lambda i: (0, 0))
    return pl.pallas_call(
        _proj_res_kernel,
        out_shape=jax.ShapeDtypeStruct((m, d), F32),
        grid=(m // tm,),
        in_specs=[pl.BlockSpec((tm, k), lambda i: (i, 0)),
                  pl.BlockSpec((None, k, d), lambda i: (layer, 0, 0)),
                  pl.BlockSpec((tm, d), lambda i: (i, 0)),
                  mod_spec(gate_slot),
                  vec_spec, vec_spec],
        out_specs=pl.BlockSpec((tm, d), lambda i: (i, 0)),
        compiler_params=_cparams(("parallel",)),
        name="proj_res",
    )(a, w, x, mod, ln_g.reshape(1, d), ln_b.reshape(1, d))


def _rel_bucket_np(dist):
    dist = np.maximum(dist, 0)
    exact = REL_BUCKETS // 2
    logv = (np.log(np.maximum(dist, 1).astype(np.float32) / np.float32(exact))
            / np.float32(math.log(REL_MAX_DIST / exact))).astype(np.float32)
    large = exact + (logv * np.float32(REL_BUCKETS - exact)).astype(np.int32)
    return np.where(dist < exact, dist, np.minimum(large, REL_BUCKETS - 1)).astype(np.int32)


def _cmp_to_slc_np(nseg, nslc):
    i = np.arange(nseg)[:, None] * CMP_STRIDE
    j = np.arange(nslc)[None, :] * SLC_LEN
    ov = np.minimum(i + CMP_LEN, j + SLC_LEN) - np.maximum(i, j)
    return (np.maximum(ov, 0) / CMP_STRIDE).astype(np.float32)


def _bias_tables(rel_table, t):
    tbl_c = rel_table.T - rel_table[REL_BUCKETS - 1][:, None]
    nseg = t // CMP_STRIDE
    ql = np.arange(Q_BLOCK)[None, :]
    dist_c = ql - ((np.arange(2 * nseg)[:, None] - nseg) * CMP_STRIDE + CMP_LEN - 1)

    def lookup(dist):
        bucket = _rel_bucket_np(dist)
        out = jnp.broadcast_to(tbl_c[:, 0][:, None, None], (N_HEADS,) + dist.shape)
        for k in range(1, REL_BUCKETS):
            out = jnp.where(bucket == k, tbl_c[:, k][:, None, None], out)
        return out

    band_t = jnp.where(dist_c >= 0, lookup(dist_c), NEG_BIAS)

    def per_group(dist, ok):
        tab = jnp.where(ok, lookup(dist), NEG_MASK)
        n_keys = dist.shape[0]
        return tab.reshape(N_KV, HPG, n_keys, Q_BLOCK).transpose(0, 2, 1, 3).reshape(N_KV, n_keys, HPG * Q_BLOCK)

    dist_s = ql + (SEL_CHUNK - Q_BLOCK) - np.arange(SEL_CHUNK)[:, None]
    dist_w = ql + WINDOW - np.arange(WINDOW + Q_BLOCK)[:, None]
    return band_t, per_group(dist_s, dist_s >= 0), per_group(dist_w, (dist_w >= 0) & (dist_w < WINDOW))


CMP_PAGES_PER_STEP = 16


def _compress_paged_kernel(pt_ref, *refs, n_pg):
    pages = refs[:n_pg]
    pe_ref, w1a_ref, w1b_ref, w2_ref, o_ref, carry_ref = refs[n_pg:]
    n_c = 2 * N_KV
    seg_rows = CMP_STRIDE * n_c
    segs = PAGE_SIZE // CMP_STRIDE
    rows = n_pg * segs * n_c
    hid_w = w2_ref.shape[0]

    @pl.when(pl.program_id(1) == 0)
    def _():
        carry_ref[...] = jnp.zeros_like(carry_ref)

    a = jnp.concatenate(
        [jnp.concatenate(
            [jnp.concatenate([pg[0, 0, seg_rows * s + n_c * r:seg_rows * s + n_c * (r + 1), :]
                              for s in range(segs)], axis=0)
             for r in range(CMP_STRIDE)], axis=1)
         for pg in pages], axis=0).astype(BF16)
    is_k = lax.broadcasted_iota(jnp.int32, (rows, 1), 0) % n_c < N_KV
    is_k8 = lax.broadcasted_iota(jnp.int32, (n_c, 1), 0) < N_KV

    def pick(y, width, k_rows):
        return jnp.where(k_rows, y[:, 0:width], y[:, width:2 * width])

    p0 = pick(_dot(a, w1a_ref[...]), hid_w, is_k)
    p1 = pick(_dot(a, w1b_ref[...]), hid_w, is_k)
    half = CMP_STRIDE * HEAD_DIM
    pe = pe_ref[...].astype(BF16)
    peh = pick(_dot(pe[:, 0:half], w1a_ref[...]) + _dot(pe[:, half:2 * half], w1b_ref[...]), hid_w, is_k8)
    shifted = jnp.concatenate([carry_ref[...], p0[0:rows - n_c]], axis=0)
    carry_ref[...] = p0[rows - n_c:rows]
    hid = (shifted + p1).reshape(rows // n_c, n_c, hid_w) + peh[None]
    y = jax.nn.gelu(hid.reshape(rows, hid_w)).astype(BF16)
    o_ref[0] = pick(_dot(y, w2_ref[...]), HEAD_DIM, is_k)


def _compress_paged(cache_pages, page_table, page0, pe, w1, w2):
    n_s, n_pages = page_table.shape
    n_pg = min(CMP_PAGES_PER_STEP, n_pages)
    n_c = 2 * N_KV
    half = CMP_STRIDE * HEAD_DIM
    rows_pg = PAGE_SIZE * n_c
    out_rows = n_pg * (PAGE_SIZE // CMP_STRIDE) * n_c
    hid_w = w1.shape[2]
    w1a = jnp.concatenate([w1[0, :half], w1[1, :half]], axis=1)
    w1b = jnp.concatenate([w1[0, half:], w1[1, half:]], axis=1)
    w2c = jnp.concatenate([w2[0], w2[1]], axis=1)
    pe8 = jnp.repeat(pe.reshape(2, 2 * half), N_KV, axis=0)

    def page_spec(p):
        return pl.BlockSpec((1, 1, rows_pg, HEAD_DIM), lambda b, c, pt: (0, page0 + pt[b, c * n_pg + p], 0, 0))

    const = lambda b, c, pt: (0, 0)
    grid_spec = pltpu.PrefetchScalarGridSpec(
        num_scalar_prefetch=1,
        grid=(n_s, n_pages // n_pg),
        in_specs=[page_spec(p) for p in range(n_pg)] + [
            pl.BlockSpec((n_c, 2 * half), const),
            pl.BlockSpec((half, 2 * hid_w), const),
            pl.BlockSpec((half, 2 * hid_w), const),
            pl.BlockSpec((hid_w, 2 * HEAD_DIM), const)],
        out_specs=pl.BlockSpec((1, out_rows, HEAD_DIM), lambda b, c, pt: (b, c, 0)),
        scratch_shapes=[pltpu.VMEM((n_c, hid_w), F32)])
    return pl.pallas_call(
        functools.partial(_compress_paged_kernel, n_pg=n_pg),
        out_shape=jax.ShapeDtypeStruct((n_s, n_pages * (PAGE_SIZE // CMP_STRIDE) * n_c, HEAD_DIM), F32),
        grid_spec=grid_spec,
        compiler_params=_cparams(("parallel", "arbitrary")),
        name="compress_paged",
    )(page_table, *([cache_pages[None]] * n_pg), pe8, w1a, w1b, w2c)


def _sample_cmp_kernel(q_ref, kv_ref, bias_ref, im_ref, oc_ref, idx_ref, *, nslc, nsel, pos):
    n_c = 2 * N_KV
    q = (q_ref[0] * ATT_SCALE).astype(BF16)
    kv = kv_ref[0].astype(BF16)
    bias = bias_ref[...]
    valid = bias > 0.5 * NEG_BIAS
    s = _dot_nt(q, kv) + bias
    m = jnp.max(s, axis=-1, keepdims=True)
    e = jnp.where(valid, jnp.exp(s - m), 0.0)
    den = jnp.sum(e, axis=-1, keepdims=True)
    p = e / jnp.where(den > 0.0, den, 1.0)
    oc_ref[0] = _dot(pltpu.roll(p, N_KV, 1).astype(BF16), kv)

    p_hi = p.astype(BF16)
    p_lo = (p - p_hi.astype(F32)).astype(BF16)
    imp_h = _dot(p_hi, im_ref[...]) + _dot(p_lo, im_ref[...])
    lanes = im_ref.shape[1]
    head_grp = lax.broadcasted_iota(jnp.int32, (N_HEADS, lanes), 0) // HPG
    row8 = lax.broadcasted_iota(jnp.int32, (n_c, lanes), 0)
    imp = jnp.zeros((n_c, lanes), F32)
    for g in range(N_KV):
        imp_g = jnp.sum(jnp.where(head_grp == g, imp_h, 0.0), axis=0, keepdims=True)
        imp = imp + jnp.where(row8 == g, imp_g, 0.0)
    jidx = lax.broadcasted_iota(jnp.int32, (n_c, lanes), 1)
    cur = pos // SLC_LEN
    valid_blk = (jidx * SLC_LEN <= pos) & (jidx < nslc)
    forced = (jidx == 0) | (jidx == cur) | (jidx == cur - 1)
    score = jnp.where(valid_blk, imp + jnp.where(forced, FORCE_BONUS, 0.0), -jnp.inf)
    cnt = jnp.zeros((n_c, lanes), jnp.int32)
    for jp in range(nslc):
        col = score[:, jp:jp + 1]
        ahead = (col > score) | ((col == score) & (jidx > jp))
        cnt = cnt + jnp.where(ahead, 1, 0)
    out_lane = lax.broadcasted_iota(jnp.int32, (n_c, HEAD_DIM), 1)
    jf = jidx.astype(F32)
    out = jnp.zeros((n_c, HEAD_DIM), F32)
    for r in range(nsel):
        blk_r = jnp.sum(jnp.where((cnt == r) & valid_blk, jf, 0.0), axis=-1, keepdims=True)
        out = jnp.where(out_lane == r, blk_r, out)
    idx_ref[0] = out.astype(jnp.int32)


def _sample_cmp(q_heads, kvc, bias, imap, *, nslc, nsel, pos):
    n_s, rows, _ = kvc.shape
    lanes = imap.shape[1]
    return pl.pallas_call(
        functools.partial(_sample_cmp_kernel, nslc=nslc, nsel=nsel, pos=pos),
        out_shape=(jax.ShapeDtypeStruct((n_s, N_HEADS, HEAD_DIM), F32),
                   jax.ShapeDtypeStruct((n_s, 2 * N_KV, HEAD_DIM), jnp.int32)),
        grid=(n_s,),
        in_specs=[pl.BlockSpec((1, N_HEADS, HEAD_DIM), lambda b: (b, 0, 0)),
                  pl.BlockSpec((1, rows, HEAD_DIM), lambda b: (b, 0, 0)),
                  pl.BlockSpec((N_HEADS, rows), lambda b: (0, 0)),
                  pl.BlockSpec((rows, lanes), lambda b: (0, 0))],
        out_specs=(pl.BlockSpec((1, N_HEADS, HEAD_DIM), lambda b: (b, 0, 0)),
                   pl.BlockSpec((1, 2 * N_KV, HEAD_DIM), lambda b: (b, 0, 0))),
        compiler_params=_cparams(("parallel",)),
        name="sample_cmp",
    )(q_heads, kvc, bias, imap)


def _sample_sel_win_kernel(idx_ref, pt_ref, *refs, n_slot, pos, n_past_blk, buf_len, thresholds):
    slots = refs[:n_slot]
    q_ref, new_ref, win_ref, tb_ref, oc_ref, gate_ref, o_ref = refs[n_slot:]
    b = pl.program_id(0)
    g = pl.program_id(1)
    n_c = 2 * N_KV
    blk_lanes = SLC_LEN * n_c
    q = (q_ref[0, 0] * ATT_SCALE).astype(BF16)
    tb = tb_ref[0]

    def bias_of(dist):
        d = jnp.maximum(dist, 0)
        out = jnp.zeros(d.shape, F32) + tb[:, 0:1]
        for k in range(1, REL_BUCKETS):
            out = out + jnp.where(d >= thresholds[k], tb[:, k:k + 1] - tb[:, k - 1:k], 0.0)
        return out

    def softmax(s, ok):
        s = jnp.where(ok, s, NEG_BIAS)
        m = jnp.max(s, axis=-1, keepdims=True)
        e = jnp.where(ok, jnp.exp(s - m), 0.0)
        den = jnp.sum(e, axis=-1, keepdims=True)
        return e / jnp.where(den > 0.0, den, 1.0)

    new_rows = new_ref[0, 0]
    s_new = _dot_nt(q, new_rows.astype(BF16))
    lane_new = lax.broadcasted_iota(jnp.int32, s_new.shape, 1)
    bias_new = bias_of(jnp.zeros(s_new.shape, jnp.int32))

    lane = lax.broadcasted_iota(jnp.int32, (n_c, blk_lanes), 1)
    mine = lane % n_c == g
    r_in = lane // n_c
    base = (b * N_KV + g) * n_slot
    s_parts, d_parts, ok_parts = [], [], []
    n_new = 0
    for i in range(n_slot):
        j = idx_ref[base + i]
        dist = jnp.where(j < n_past_blk, pos - j * SLC_LEN, -1) - r_in
        s_parts.append(_dot_nt(q, slots[i][0, 0].astype(BF16)))
        d_parts.append(dist)
        ok_parts.append(mine & (dist >= 0))
        n_new = n_new + jnp.where(j >= n_past_blk, 1, 0)
    s_all = jnp.concatenate(s_parts + [s_new], axis=1)
    ok_all = jnp.concatenate(ok_parts + [lane_new == jnp.where(n_new > 0, 0, -1)], axis=1)
    b_all = jnp.concatenate([bias_of(jnp.concatenate(d_parts, axis=1)), bias_new], axis=1)
    p = softmax(s_all + b_all, ok_all)
    n_old = n_slot * blk_lanes
    p_v = pltpu.roll(p[:, 0:n_old], N_KV, 1).astype(BF16)
    o_s = p[:, n_old:n_old + 1] * new_rows[1:2, :]
    for i in range(n_slot):
        o_s = o_s + _dot(p_v[:, i * blk_lanes:(i + 1) * blk_lanes], slots[i][0, 0].astype(BF16))

    win = win_ref[0].astype(BF16)
    lane_w = lax.broadcasted_iota(jnp.int32, (n_c, buf_len * n_c), 1)
    dist_w = buf_len - lane_w // n_c
    ok_w = (lane_w % n_c == g) & (dist_w >= 0) & (dist_w < WINDOW)
    s_w = jnp.concatenate([_dot_nt(q, win) + bias_of(dist_w), s_new + bias_new], axis=1)
    p = softmax(s_w, jnp.concatenate([ok_w, lane_new == 2], axis=1))
    n_old = buf_len * n_c
    o_w = _dot(pltpu.roll(p[:, 0:n_old], N_KV, 1).astype(BF16), win) + p[:, n_old + 2:n_old + 3] * new_rows[3:4, :]

    gates = jax.nn.sigmoid(gate_ref[0, 0])
    o_ref[0, 0] = gates[:, 0:1] * oc_ref[0, 0] + gates[:, 1:2] * o_s + gates[:, 2:3] * o_w


def _sample_sel_win(idx, page_table, slc_halves, page0, q_g, new_rows, win_rows, tb, o_c, gates, *, pos, nsel):
    n_s, n_pages = page_table.shape
    n_c = 2 * N_KV
    per_page = PAGE_SIZE // SLC_LEN
    n_past_blk = n_pages * per_page
    buf_len = win_rows.shape[1] // n_c
    blk_rows = SLC_LEN * n_c
    thresholds = tuple(int(np.argmax(_rel_bucket_np(np.arange(4 * REL_MAX_DIST)) >= k)) for k in range(REL_BUCKETS))

    def slot_spec(i):
        def index_map(b, g, idx_ref, pt_ref):
            j = jnp.minimum(idx_ref[(b * N_KV + g) * nsel + i], n_past_blk - 1)
            return (page0 + pt_ref[b, j // per_page], j % per_page, 0, 0)
        return pl.BlockSpec((1, 1, blk_rows, HEAD_DIM), index_map)

    grp = lambda b, g, idx_ref, pt_ref: (b, g, 0, 0)
    grid_spec = pltpu.PrefetchScalarGridSpec(
        num_scalar_prefetch=2,
        grid=(n_s, N_KV),
        in_specs=[slot_spec(i) for i in range(nsel)] + [
            pl.BlockSpec((1, 1, n_c, HEAD_DIM), grp),
            pl.BlockSpec((1, 1, HEAD_DIM, HEAD_DIM), grp),
            pl.BlockSpec((1, buf_len * n_c, HEAD_DIM), lambda b, g, idx_ref, pt_ref: (b, 0, 0)),
            pl.BlockSpec((1, n_c, HEAD_DIM), lambda b, g, idx_ref, pt_ref: (g, 0, 0)),
            pl.BlockSpec((1, 1, n_c, HEAD_DIM), grp),
            pl.BlockSpec((1, 1, n_c, HEAD_DIM), grp)],
        out_specs=pl.BlockSpec((1, 1, n_c, HEAD_DIM), grp))
    return pl.pallas_call(
        functools.partial(_sample_sel_win_kernel, n_slot=nsel, pos=pos, n_past_blk=n_past_blk, buf_len=buf_len,
                          thresholds=thresholds),
        out_shape=jax.ShapeDtypeStruct((n_s, N_KV, n_c, HEAD_DIM), F32),
        grid_spec=grid_spec,
        compiler_params=_cparams(("parallel", "parallel")),
        name="sample_sel_win",
    )(idx, page_table, *([slc_halves] * nsel), q_g, new_rows, win_rows, tb, o_c, gates)


def _sample_nsa(proj_s, cmp_pages, slc_halves, page0, win_buf, page_table, pe, w1, w2, rel_table):
    n_s, n_pages = page_table.shape
    past = n_pages * PAGE_SIZE
    pos = past
    n_c = 2 * N_KV
    d_q = N_HEADS * HEAD_DIM
    kvw = n_c * HEAD_DIM
    nseg = past // CMP_STRIDE
    nslc = -(-(past + 1) // SLC_LEN)
    nsel = min(N_SEL, nslc)
    kvc = _compress_paged(cmp_pages, page_table, page0, pe, w1, w2)
    x = np.arange(nseg)[:, None]
    c = np.arange(n_c)[None, :]
    dist = pos - ((x - 1) * CMP_STRIDE + CMP_LEN - 1)
    head_grp = np.arange(N_HEADS)[:, None, None] // HPG
    ok = ((x >= 1) & (dist >= 0))[None] & (c[None] == head_grp)
    bias = jnp.where(ok, jnp.take(rel_table.T, _rel_bucket_np(np.broadcast_to(dist, (nseg, n_c))), axis=1),
                     NEG_BIAS).reshape(N_HEADS, nseg * n_c)
    lanes = -(-nslc // HEAD_DIM) * HEAD_DIM
    imap = np.zeros((nseg, n_c, lanes), np.float32)
    imap[1:, :, :nslc] = _cmp_to_slc_np(nseg - 1, nslc)[:, None, :]
    q_heads = proj_s[:, :d_q].reshape(n_s, N_HEADS, HEAD_DIM)
    o_c, idx = _sample_cmp(q_heads, kvc, bias, jnp.asarray(imap.reshape(nseg * n_c, lanes)).astype(BF16),
                           nslc=nslc, nsel=nsel, pos=pos)
    pad_heads = lambda a: jnp.pad(a.reshape(n_s, N_KV, HPG, -1), ((0, 0), (0, 0), (0, n_c - HPG), (0, 0)))
    kv_new = proj_s[:, d_q:d_q + 3 * kvw].reshape(n_s, 3, 2, N_KV, HEAD_DIM)
    new_rows = kv_new[:, 1:3].transpose(0, 3, 1, 2, 4).reshape(n_s, N_KV, 4, HEAD_DIM)
    new_rows = jnp.pad(new_rows, ((0, 0), (0, 0), (0, HEAD_DIM - 4), (0, 0)))
    gates = proj_s[:, d_q + 3 * kvw:].reshape(n_s, N_KV, HEAD_DIM)[:, :, :3 * HPG].reshape(n_s, N_KV, 3, HPG)
    gates = jnp.pad(gates.transpose(0, 1, 3, 2), ((0, 0), (0, 0), (0, n_c - HPG), (0, HEAD_DIM - 3)))
    tb = jnp.pad(rel_table.T.reshape(N_KV, HPG, REL_BUCKETS), ((0, 0), (0, n_c - HPG), (0, HEAD_DIM - REL_BUCKETS)))
    o = _sample_sel_win(idx[:, :N_KV, :nsel].reshape(-1), page_table, slc_halves, page0,
                        pad_heads(proj_s[:, :d_q]), new_rows, win_buf.reshape(n_s, -1, HEAD_DIM), tb,
                        pad_heads(o_c), gates, pos=pos, nsel=nsel)
    new_win = jnp.concatenate([win_buf, kv_new[:, 2][:, None]], axis=1)[:, -min(WINDOW, win_buf.shape[1] + 1):]
    return o[:, :, :HPG].reshape(n_s, d_q), kv_new, new_win


def _pad_rows(a, rows):
    return jnp.pad(a, ((0, rows - a.shape[0]),) + ((0, 0),) * (a.ndim - 1))


def _nsa_w_in_layout(w):
    n_l, d, _ = w.shape
    n_main = N_HEADS * HEAD_DIM + 6 * N_KV * HEAD_DIM
    wg = w[:, :, n_main:].reshape(n_l, d, 3, N_KV, HPG).transpose(0, 1, 3, 2, 4).reshape(n_l, d, N_KV, 3 * HPG)
    wg = jnp.pad(wg, ((0, 0), (0, 0), (0, 0), (0, HEAD_DIM - 3 * HPG))).reshape(n_l, d, N_KV * HEAD_DIM)
    return jnp.concatenate([w[:, :, :n_main], wg], axis=2).astype(BF16)


def kernel(x_prompt, x_sample, cache_cmp_kv, cache_slc_kv, state_win_kv, page_table, c_prompt, c_sample,
           ada_w, ada_b, ln_g, ln_b, ffn_pre_w_in, ffn_pre_w_out, ffn_post_w_in, ffn_post_w_out,
           gmlp_w_in, gmlp_ln_g, gmlp_ln_b, gmlp_w_s, gmlp_b_s, gmlp_w_out,
           nsa_w_in, nsa_cmp_pe, nsa_cmp_w1, nsa_cmp_w2, nsa_w_out, rel_table):
    n_b, t, d = x_prompt.shape
    n_s = x_sample.shape[0]
    n_phys = cache_cmp_kv.shape[1]
    d_q = N_HEADS * HEAD_DIM
    kvw = 2 * N_KV * HEAD_DIM
    tm_p = min(512, t)
    tm_f = min(512, t)
    tm_mm = min(1024, t)
    tm_g = min(512, t)

    xp = x_prompt.reshape(n_b * t, d)
    place = lambda a: jnp.pad(a, ((n_b, SAMPLE_ROWS - n_b - n_s), (0, 0)))
    xs = place(x_sample.reshape(n_s, d))
    c_all = _pad_rows(jnp.concatenate([c_prompt, c_sample], axis=0), SAMPLE_ROWS)
    mod = _ada(c_all, ada_w, ada_b)
    mod_rows = mod.reshape(DEPTH * SAMPLE_ROWS * N_ADA, 1, d)

    def prompt_mod(layer, tm):
        return lambda k: pl.BlockSpec(
            (1, 1, d), lambda i, *_: ((layer * SAMPLE_ROWS + i // (t // tm)) * N_ADA + k, 0, 0))

    def sample_mod(layer):
        return lambda k: pl.BlockSpec((1, SAMPLE_ROWS, d), lambda i, *_: (layer, 0, k))

    band_t, bias_sel, bias_win = _bias_tables(rel_table, t)
    imap_t = jnp.asarray(_cmp_to_slc_np(t // CMP_STRIDE, t // SLC_LEN).T).astype(BF16)
    cmp_pages = cache_cmp_kv.reshape(-1, PAGE_SIZE * 2 * N_KV, HEAD_DIM)
    slc_halves = cache_slc_kv.reshape(-1, PAGE_SIZE // SLC_LEN, SLC_LEN * 2 * N_KV, HEAD_DIM)

    pre_w_in, pre_w_out = ffn_pre_w_in.astype(BF16), ffn_pre_w_out.astype(BF16)
    post_w_in, post_w_out = ffn_post_w_in.astype(BF16), ffn_post_w_out.astype(BF16)
    gw_in, gw_out = gmlp_w_in.astype(BF16), gmlp_w_out.astype(BF16)
    nw_in, nw_out = _nsa_w_in_layout(nsa_w_in), nsa_w_out.astype(BF16)

    cmp_p, cmp_s, slc_p, slc_s, win_p, win_s, gv_s = [], [], [], [], [], [], []
    for i in range(DEPTH):
        ms = sample_mod(i)
        xp = _ffn(xp, mod_rows, prompt_mod(i, tm_f), (0, 1, 2), pre_w_in, pre_w_out, ln_g[i, 0], ln_b[i, 0],
                  layer=i, tm=tm_f)
        xs = _ffn(xs, mod, ms, (0, 1, 2), pre_w_in, pre_w_out, ln_g[i, 0], ln_b[i, 0], layer=i, tm=SAMPLE_ROWS)

        if i % 2 == 0:
            a = i // 2
            xp, = _gmlp(xp, mod_rows, prompt_mod(i, tm_g), (3, 4, 5), gw_in, gmlp_ln_g[a], gmlp_ln_b[a], gmlp_w_s[a],
                        gmlp_b_s[a], gw_out, ln_g[i, 1], ln_b[i, 1], layer=a, tm=tm_g, chunk=min(CHUNK, t),
                        emit_vn=False)
            xs, vn_s = _gmlp(xs, mod, ms, (3, 4, 5), gw_in, gmlp_ln_g[a], gmlp_ln_b[a], gmlp_w_s[a], gmlp_b_s[a],
                             gw_out, ln_g[i, 1], ln_b[i, 1], layer=a, tm=SAMPLE_ROWS, chunk=1, emit_vn=True)
            gv_s.append(vn_s[n_b:n_b + n_s].reshape(n_s, 1, -1))
        else:
            a = i // 2
            w1 = nsa_cmp_w1[a].astype(BF16)
            w2 = nsa_cmp_w2[a].astype(BF16)
            proj = _mod_mm(xp, mod_rows, prompt_mod(i, tm_mm), (3, 4), nw_in, layer=a, tm=tm_mm, tn=512)
            kvc = _compress(proj, n_b, t, d_q // HEAD_DIM, nsa_cmp_pe[a], w1, w2)
            o = _nsa_attn(proj, kvc, band_t, imap_t, bias_sel, bias_win, n_b, t)
            xp = _proj_res(o, nw_out, xp, mod_rows, prompt_mod(i, tm_p), 5, ln_g[i, 1], ln_b[i, 1], layer=a, tm=tm_p)
            kv_rows = lambda branch, rows: rows[..., d_q + branch * kvw:d_q + (branch + 1) * kvw].reshape(
                n_b, -1, 2, N_KV, HEAD_DIM)
            cmp_p.append(kv_rows(0, proj))
            slc_p.append(kv_rows(1, proj))
            win_p.append(kv_rows(2, proj.reshape(n_b, t, -1)[:, -min(WINDOW, t):]))
            proj_s = _mod_mm(xs, mod, ms, (3, 4), nw_in, layer=a, tm=SAMPLE_ROWS, tn=512)[n_b:n_b + n_s]
            o_samp, kv_new, new_win = _sample_nsa(proj_s, cmp_pages, slc_halves, a * n_phys, state_win_kv[a],
                                                  page_table, nsa_cmp_pe[a], w1, w2, rel_table)
            xs = _proj_res(place(o_samp).astype(BF16), nw_out, xs, mod, ms, 5, ln_g[i, 1], ln_b[i, 1],
                           layer=a, tm=SAMPLE_ROWS)
            cmp_s.append(kv_new[:, 0][:, None])
            slc_s.append(kv_new[:, 1][:, None])
            win_s.append(new_win)

        xp = _ffn(xp, mod_rows, prompt_mod(i, tm_f), (6, 7, 8), post_w_in, post_w_out, ln_g[i, 2], ln_b[i, 2],
                  layer=i, tm=tm_f)
        xs = _ffn(xs, mod, ms, (6, 7, 8), post_w_in, post_w_out, ln_g[i, 2], ln_b[i, 2], layer=i, tm=SAMPLE_ROWS)

    return (xp.reshape(n_b, t, d), xs[n_b:n_b + n_s].reshape(n_s, 1, d),
            jnp.stack(cmp_p), jnp.stack(cmp_s), jnp.stack(slc_p), jnp.stack(slc_s),
            jnp.stack(win_p), jnp.stack(win_s), jnp.stack(gv_s))
```

```python
import functools
import math

import numpy as np
import jax
import jax.numpy as jnp
from jax import lax
from jax.experimental import pallas as pl
from jax.experimental.pallas import tpu as pltpu

F32 = jnp.float32
BF16 = jnp.bfloat16

DEPTH = 4
N_ADA = 9
N_HEADS = 16
HEAD_DIM = 128
N_KV = 4
HPG = N_HEADS // N_KV
GMLP_GROUPS = 16
CHUNK = 128
PAGE_SIZE = 128
CMP_LEN = 32
CMP_STRIDE = 16
SLC_LEN = 64
N_SEL = 16
WINDOW = 512
Q_BLOCK = 128
REL_BUCKETS = 32
REL_MAX_DIST = 128
LN_EPS = 1e-5
DN_ALPHA = (2 * DEPTH) ** 0.25
FORCE_BONUS = 1e4
ATT_SCALE = HEAD_DIM ** -0.5

SAMPLE_ROWS = 16
NEG_MASK = -30000.0
NEG_BIAS = -1e30
SEL_MASK = 32768.0
KEY_PAD = WINDOW
SEL_CHUNK = 512
VAL_ROWS = HEAD_DIM + 16
VMEM_LIMIT = 56 * 1024 * 1024


def _cparams(sem):
    return pltpu.CompilerParams(dimension_semantics=sem, vmem_limit_bytes=VMEM_LIMIT)


def _dot(a, b):
    return jnp.dot(a, b, preferred_element_type=F32)


def _dot_nt(a, b):
    return lax.dot_general(a, b, (((1,), (1,)), ((), ())), preferred_element_type=F32)


def _ln_rows(z, g, b):
    mu = jnp.mean(z, axis=-1, keepdims=True)
    zc = z - mu
    var = jnp.mean(zc * zc, axis=-1, keepdims=True)
    return zc * lax.rsqrt(var + LN_EPS) * g + b


def _silu(x):
    return x * jax.nn.sigmoid(x)


def _ada_kernel(c_ref, w_ref, b_ref, o_ref):
    h = _silu(c_ref[...]).astype(BF16)
    o_ref[0] = _dot(h, w_ref[0].astype(BF16)) + b_ref[0]


def _ada(c_all, ada_w, ada_b, tn=1024):
    depth, d, n = ada_w.shape
    r = c_all.shape[0]
    return pl.pallas_call(
        _ada_kernel,
        out_shape=jax.ShapeDtypeStruct((depth, r, n), F32),
        grid=(depth, n // tn),
        in_specs=[pl.BlockSpec((r, d), lambda l, j: (0, 0)),
                  pl.BlockSpec((1, d, tn), lambda l, j: (l, 0, j)),
                  pl.BlockSpec((1, 1, tn), lambda l, j: (l, 0, j))],
        out_specs=pl.BlockSpec((1, r, tn), lambda l, j: (l, 0, j)),
        compiler_params=_cparams(("parallel", "parallel")),
        name="ada",
    )(c_all, ada_w, ada_b.reshape(depth, 1, n))


def _ffn_kernel(x_ref, sh_ref, sc_ref, gt_ref, wg_ref, wu_ref, wo_ref, lg_ref, lb_ref, o_ref, h_ref):
    j = pl.program_id(1)

    @pl.when(j == 0)
    def _():
        h_ref[...] = (x_ref[...] * (1.0 + sc_ref[0]) + sh_ref[0]).astype(BF16)
        o_ref[...] = jnp.zeros_like(o_ref)

    h = h_ref[...]
    g = _dot(h, wg_ref[...])
    u = _dot(h, wu_ref[...])
    a = (_silu(g) * u).astype(BF16)
    o_ref[...] += _dot(a, wo_ref[...])

    @pl.when(j == pl.num_programs(1) - 1)
    def _():
        z = DN_ALPHA * x_ref[...] + (1.0 + gt_ref[0]) * (0.5 * o_ref[...])
        o_ref[...] = _ln_rows(z, lg_ref[...], lb_ref[...])


def _ffn(x, mod, mod_spec, slots, w_in, w_out, ln_g, ln_b, *, layer, tm, tf=512):
    m, d = x.shape
    dff = w_out.shape[1]
    nf = dff // tf
    vec_spec = pl.BlockSpec((1, d), lambda i, j: (0, 0))
    return pl.pallas_call(
        _ffn_kernel,
        out_shape=jax.ShapeDtypeStruct((m, d), F32),
        grid=(m // tm, nf),
        in_specs=[pl.BlockSpec((tm, d), lambda i, j: (i, 0)),
                  mod_spec(slots[0]), mod_spec(slots[1]), mod_spec(slots[2]),
                  pl.BlockSpec((None, d, tf), lambda i, j: (layer, 0, j)),
                  pl.BlockSpec((None, d, tf), lambda i, j: (layer, 0, j + nf)),
                  pl.BlockSpec((None, tf, d), lambda i, j: (layer, j, 0)),
                  vec_spec, vec_spec],
        out_specs=pl.BlockSpec((tm, d), lambda i, j: (i, 0)),
        scratch_shapes=[pltpu.VMEM((tm, d), BF16)],
        compiler_params=_cparams(("parallel", "arbitrary")),
        name="ffn",
    )(x, mod, mod, mod, w_in, w_in, w_out, ln_g.reshape(1, d), ln_b.reshape(1, d))


def _mm_kernel(x_ref, sh_ref, sc_ref, w_ref, o_ref, h_ref):
    @pl.when(pl.program_id(1) == 0)
    def _():
        h_ref[...] = (x_ref[...] * (1.0 + sc_ref[0]) + sh_ref[0]).astype(BF16)

    o_ref[...] = _dot(h_ref[...], w_ref[...])


def _mod_mm(x, mod, mod_spec, slots, w, *, layer, tm, tn):
    m, d = x.shape
    n = w.shape[2]
    return pl.pallas_call(
        _mm_kernel,
        out_shape=jax.ShapeDtypeStruct((m, n), F32),
        grid=(m // tm, n // tn),
        in_specs=[pl.BlockSpec((tm, d), lambda i, j: (i, 0)),
                  mod_spec(slots[0]), mod_spec(slots[1]),
                  pl.BlockSpec((None, d, tn), lambda i, j: (layer, 0, j))],
        out_specs=pl.BlockSpec((tm, tn), lambda i, j: (i, j)),
        scratch_shapes=[pltpu.VMEM((tm, d), BF16)],
        compiler_params=_cparams(("parallel", "arbitrary")),
        name="mod_mm",
    )(x, mod, mod, w)


GMLP_GROUPS_PER_STEP = 2


def _gmlp_kernel(x_ref, sh_ref, sc_ref, gt_ref, win_ref, vlg_ref, vlb_ref, ws_ref, bs_ref,
                 wo_ref, lg_ref, lb_ref, o_ref, *rest, chunk, gw, emit_vn):
    if emit_vn:
        vn_ref, h_ref, v_ref, y_ref, mu_ref, rs_ref = rest
    else:
        h_ref, v_ref, y_ref, mu_ref, rs_ref = rest
    k = pl.program_id(1)
    ns = (pl.num_programs(1) - 1) // 2
    tm = x_ref.shape[0]
    gdim = ns * v_ref.shape[2]

    @pl.when(k == 0)
    def _():
        h_ref[...] = (x_ref[...] * (1.0 + sc_ref[0]) + sh_ref[0]).astype(BF16)
        o_ref[...] = jnp.zeros_like(o_ref)

    def in_proj():
        return jax.nn.gelu(_dot(h_ref[...], win_ref[...]))

    @pl.when(k < ns)
    def _():
        v_ref[k] = in_proj()

    @pl.when(k == ns)
    def _():
        tot = jnp.zeros((tm, 1), F32)
        for c in range(ns):
            tot = tot + jnp.sum(v_ref[c], axis=-1, keepdims=True)
        mu = tot / gdim
        sq = jnp.zeros((tm, 1), F32)
        for c in range(ns):
            dv = v_ref[c] - mu
            sq = sq + jnp.sum(dv * dv, axis=-1, keepdims=True)
        mu_ref[...] = mu
        rs_ref[...] = lax.rsqrt(sq / gdim + LN_EPS)
        y_ref[0] = in_proj()

    def gate(j):
        y = y_ref[j % 2]
        vn = (v_ref[j] - mu_ref[...]) * rs_ref[...] * vlg_ref[...] + vlb_ref[...]
        if emit_vn:
            vn_ref[...] = vn
        parts = []
        for gi in range(GMLP_GROUPS_PER_STEP):
            vn_g = vn[:, gi * gw:(gi + 1) * gw]
            if chunk == 1:
                s_g = ws_ref[gi, 0:1, 0:1] * vn_g + bs_ref[gi, 0:1, 0:1]
            else:
                row = lax.broadcasted_iota(jnp.int32, (chunk, chunk), 0)
                col = lax.broadcasted_iota(jnp.int32, (chunk, chunk), 1)
                w_tri = jnp.where(col <= row, ws_ref[gi], 0.0).astype(BF16)
                vb = vn_g.astype(BF16)
                s_g = jnp.concatenate(
                    [_dot(w_tri, vb[c * chunk:(c + 1) * chunk]) + bs_ref[gi]
                     for c in range(tm // chunk)], axis=0)
            parts.append(s_g)
        a = (y * jnp.concatenate(parts, axis=1)).astype(BF16)
        o_ref[...] += _dot(a, wo_ref[...])

    @pl.when((k > ns) & (k < 2 * ns))
    def _():
        j = k - ns - 1
        y_next = in_proj()
        gate(j)
        y_ref[(j + 1) % 2] = y_next

    @pl.when(k == 2 * ns)
    def _():
        gate(ns - 1)
        z = DN_ALPHA * x_ref[...] + (1.0 + gt_ref[0]) * o_ref[...]
        o_ref[...] = _ln_rows(z, lg_ref[...], lb_ref[...])


def _gmlp(x, mod, mod_spec, slots, w_in, v_ln_g, v_ln_b, w_s, b_s, w_out, ln_g, ln_b, *, layer, tm, chunk,
          emit_vn):
    m, d = x.shape
    gdim = w_out.shape[1]
    gw = gdim // GMLP_GROUPS
    gw2 = gw * GMLP_GROUPS_PER_STEP
    ns = GMLP_GROUPS // GMLP_GROUPS_PER_STEP
    vec_spec = pl.BlockSpec((1, d), lambda i, k: (0, 0))
    second = lambda k: jnp.clip(k - ns - 1, 0, ns - 1)
    out_shape = [jax.ShapeDtypeStruct((m, d), F32)]
    out_specs = [pl.BlockSpec((tm, d), lambda i, k: (i, 0))]
    if emit_vn:
        out_shape.append(jax.ShapeDtypeStruct((m, gdim), F32))
        out_specs.append(pl.BlockSpec((tm, gw2), lambda i, k: (i, second(k))))
    return pl.pallas_call(
        functools.partial(_gmlp_kernel, chunk=chunk, gw=gw, emit_vn=emit_vn),
        out_shape=tuple(out_shape),
        grid=(m // tm, 2 * ns + 1),
        in_specs=[pl.BlockSpec((tm, d), lambda i, k: (i, 0), pipeline_mode=pl.Buffered(1)),
                  mod_spec(slots[0]), mod_spec(slots[1]), mod_spec(slots[2]),
                  pl.BlockSpec((None, d, gw2),
                               lambda i, k: (layer, 0, jnp.where(k < ns, k + ns, jnp.minimum(k - ns, ns - 1)))),
                  pl.BlockSpec((1, gw2), lambda i, k: (0, second(k))),
                  pl.BlockSpec((1, gw2), lambda i, k: (0, second(k))),
                  pl.BlockSpec((GMLP_GROUPS_PER_STEP, CHUNK, CHUNK), lambda i, k: (second(k), 0, 0)),
                  pl.BlockSpec((GMLP_GROUPS_PER_STEP, CHUNK, 1), lambda i, k: (second(k), 0, 0)),
                  pl.BlockSpec((None, gw2, d), lambda i, k: (layer, second(k), 0)),
                  vec_spec, vec_spec],
        out_specs=tuple(out_specs),
        scratch_shapes=[pltpu.VMEM((tm, d), BF16), pltpu.VMEM((ns, tm, gw2), F32), pltpu.VMEM((2, tm, gw2), F32),
                        pltpu.VMEM((tm, 1), F32), pltpu.VMEM((tm, 1), F32)],
        compiler_params=_cparams(("parallel", "arbitrary")),
        name="gmlp",
    )(x, mod, mod, mod, w_in, v_ln_g.reshape(1, gdim), v_ln_b.reshape(1, gdim), w_s,
      b_s.reshape(GMLP_GROUPS, CHUNK, 1), w_out, ln_g.reshape(1, d), ln_b.reshape(1, d))


def _compress_kernel(r_ref, pe_ref, w1_ref, w2_ref, o_ref, *, nseg):
    half = CMP_STRIDE * HEAD_DIM
    a = jnp.concatenate(
        [r_ref[pl.ds(r, nseg, stride=CMP_STRIDE), :].astype(BF16) for r in range(CMP_STRIDE)], axis=1)
    p0 = _dot(a, w1_ref[0, 0:half, :])
    p1 = _dot(a, w1_ref[0, half:2 * half, :])
    pe = jnp.broadcast_to(pe_ref[0], (8, 2 * half)).astype(BF16)
    peh = _dot(pe, w1_ref[0])[0:1]
    hid = peh + p0 + pltpu.roll(p1, nseg - 1, 0)
    o_ref[0, 0, 0] = _dot(jax.nn.gelu(hid).astype(BF16), w2_ref[0])


def _compress(rows2d, n_batch, t, col0, pe, w1, w2):
    nseg = t // CMP_STRIDE
    return pl.pallas_call(
        functools.partial(_compress_kernel, nseg=nseg),
        out_shape=jax.ShapeDtypeStruct((n_batch, 2, N_KV, nseg, HEAD_DIM), F32),
        grid=(n_batch, 2, N_KV),
        in_specs=[pl.BlockSpec((t, HEAD_DIM), lambda b, kv, g: (b, col0 + kv * N_KV + g)),
                  pl.BlockSpec((1, 1, CMP_LEN * HEAD_DIM), lambda b, kv, g: (kv, 0, 0)),
                  pl.BlockSpec((1, CMP_LEN * HEAD_DIM, w1.shape[2]), lambda b, kv, g: (kv, 0, 0)),
                  pl.BlockSpec((1, w2.shape[1], HEAD_DIM), lambda b, kv, g: (kv, 0, 0))],
        out_specs=pl.BlockSpec((1, 1, 1, nseg, HEAD_DIM), lambda b, kv, g: (b, kv, g, 0, 0)),
        compiler_params=_cparams(("parallel", "parallel", "parallel")),
        name="compress",
    )(rows2d, pe.reshape(2, 1, CMP_LEN * HEAD_DIM), w1, w2)


def _softmax_update_t(state, s, vt):
    m, acc = state
    m_new = jnp.maximum(m, jnp.max(s, axis=0, keepdims=True))
    alpha = jnp.exp(m - m_new)
    p = jnp.exp(s - m_new)
    return m_new, alpha * acc + _dot(vt, p.astype(BF16))


def _nsa_attn_kernel(q_ref, kc_ref, vc_ref, band_ref, imap_ref, ks_ref, vs_ref, kw_ref, vw_ref,
                     bsel_ref, bwin_ref, gate_ref, o_ref, ksa, vst, kwa, vwt, *, nslc, nsel):
    qb = pl.program_id(2)
    tq = Q_BLOCK
    cols = HPG * tq
    t = ks_ref.shape[0]
    nseg = kc_ref.shape[3]

    @pl.when(qb == 0)
    def _():
        lane_p = lax.broadcasted_iota(jnp.int32, (KEY_PAD, HEAD_DIM), 1)
        pad_mark = jnp.where(lane_p == HEAD_DIM - 1, 1.0, 0.0).astype(BF16)
        key_blk = lax.broadcasted_iota(jnp.int32, (t, HEAD_DIM), 0) // SLC_LEN
        lane = lax.broadcasted_iota(jnp.int32, (t, HEAD_DIM), 1)
        for ref in (ksa, kwa):
            ref[0:KEY_PAD, 0:HEAD_DIM] = jnp.zeros((KEY_PAD, HEAD_DIM), BF16)
            ref[0:KEY_PAD, HEAD_DIM:2 * HEAD_DIM] = pad_mark
        ksa[KEY_PAD:KEY_PAD + t, 0:HEAD_DIM] = ks_ref[...].astype(BF16)
        ksa[KEY_PAD:KEY_PAD + t, HEAD_DIM:2 * HEAD_DIM] = jnp.where(lane == key_blk, 1.0, 0.0).astype(BF16)
        kwa[KEY_PAD:KEY_PAD + t, 0:HEAD_DIM] = kw_ref[...].astype(BF16)
        kwa[KEY_PAD:KEY_PAD + t, HEAD_DIM:2 * HEAD_DIM] = jnp.zeros((t, HEAD_DIM), BF16)
        ones_row = lax.broadcasted_iota(jnp.int32, (VAL_ROWS - HEAD_DIM, KEY_PAD + t), 0) == 0
        for ref in (vst, vwt):
            ref[0:HEAD_DIM, 0:KEY_PAD] = jnp.zeros((HEAD_DIM, KEY_PAD), BF16)
            ref[HEAD_DIM:VAL_ROWS, :] = jnp.where(ones_row, 1.0, 0.0).astype(BF16)
        for c in range(t // tq):
            cols_c = slice(KEY_PAD + c * tq, KEY_PAD + (c + 1) * tq)
            vst[0:HEAD_DIM, cols_c] = vs_ref[c * tq:(c + 1) * tq, :].T.astype(BF16)
            vwt[0:HEAD_DIM, cols_c] = vw_ref[c * tq:(c + 1) * tq, :].T.astype(BF16)

    q_t = jnp.concatenate(
        [(q_ref[:, h * HEAD_DIM:(h + 1) * HEAD_DIM] * ATT_SCALE).T for h in range(HPG)], axis=1).astype(BF16)

    off = pl.multiple_of(nseg - (tq // CMP_STRIDE) * qb, 8)
    bias = jnp.concatenate([band_ref[h, pl.ds(off, nseg), :] for h in range(HPG)], axis=1)
    valid = bias > 0.5 * NEG_BIAS
    s = _dot(kc_ref[0, 0, 0].astype(BF16), q_t) + bias
    m = jnp.max(s, axis=0, keepdims=True)
    e = jnp.where(valid, jnp.exp(s - m), 0.0)
    den = jnp.sum(e, axis=0, keepdims=True)
    p = e / jnp.where(den > 0.0, den, 1.0)
    o_c = _dot(vc_ref[0, 0, 0].T.astype(BF16), p.astype(BF16))

    psum = p[:, 0:tq]
    for h in range(1, HPG):
        psum = psum + p[:, h * tq:(h + 1) * tq]
    p_hi = psum.astype(BF16)
    p_lo = (psum - p_hi.astype(F32)).astype(BF16)
    imap = imap_ref[...]
    imp = _dot(imap, p_hi) + _dot(imap, p_lo)
    jidx = lax.broadcasted_iota(jnp.int32, (nslc, tq), 0)
    qpos = qb * tq + lax.broadcasted_iota(jnp.int32, (nslc, tq), 1)
    cur = qpos // SLC_LEN
    valid_blk = jidx * SLC_LEN <= qpos
    forced = (jidx == 0) | (jidx == cur) | (jidx == cur - 1)
    score = jnp.where(valid_blk, imp + jnp.where(forced, FORCE_BONUS, 0.0), -jnp.inf)
    slab = 8
    slabs = [score[v:v + slab] for v in range(0, nslc, slab)]
    cnts = [jnp.zeros((slab, tq), jnp.int32) for _ in slabs]
    jloc = lax.broadcasted_iota(jnp.int32, (slab, tq), 0)
    for jp in range(nslc):
        row = score[jp:jp + 1, :]
        for v, blk in enumerate(slabs):
            if v * slab > jp:
                ahead = row >= blk
            elif (v + 1) * slab <= jp:
                ahead = row > blk
            else:
                ahead = (row > blk) | ((row == blk) & (jloc > jp - v * slab))
            cnts[v] = cnts[v] + jnp.where(ahead, 1, 0)
    cnt = jnp.concatenate(cnts, axis=0)
    unsel = jnp.where((cnt < nsel) & valid_blk, 0.0, -SEL_MASK)

    marker_row = lax.broadcasted_iota(jnp.int32, (8, cols), 0) == 7
    aug = jnp.concatenate([jnp.concatenate([unsel] * HPG, axis=1),
                           jnp.zeros((HEAD_DIM - nslc - 8, cols), F32),
                           jnp.where(marker_row, -SEL_MASK, 0.0)], axis=0)
    rhs = jnp.concatenate([q_t, aug.astype(BF16)], axis=0)

    last = pl.multiple_of((qb + 1) * tq, tq)
    s = _dot(ksa[pl.ds(last, SEL_CHUNK), :], rhs) + bsel_ref[0]
    m = jnp.max(s, axis=0, keepdims=True)
    p = jnp.exp(s - m)
    st = (m, _dot(vst[:, pl.ds(last, SEL_CHUNK)], p.astype(BF16)))

    def far_start(c):
        return pl.multiple_of(jnp.maximum(last - (c + 1) * SEL_CHUNK, 0), tq)

    def far_logits(c):
        return _dot(ksa[pl.ds(far_start(c), SEL_CHUNK), :], rhs)

    def far_body(c, carry):
        state, s_c = carry
        s_next = far_logits(c + 1)
        return _softmax_update_t(state, s_c, vst[:, pl.ds(far_start(c), SEL_CHUNK)]), s_next

    (_, acc_s), _ = lax.fori_loop(0, qb // (SEL_CHUNK // tq), far_body, (st, far_logits(0)))
    o_s = acc_s[0:HEAD_DIM] / acc_s[HEAD_DIM:HEAD_DIM + 1]

    first = pl.multiple_of(qb * tq, tq)
    s = _dot(kwa[pl.ds(first, WINDOW + tq), :], rhs) + bwin_ref[0]
    p = jnp.exp(s - jnp.max(s, axis=0, keepdims=True))
    acc_w = _dot(vwt[:, pl.ds(first, WINDOW + tq)], p.astype(BF16))
    o_w = acc_w[0:HEAD_DIM] / acc_w[HEAD_DIM:HEAD_DIM + 1]

    gates_t = jax.nn.sigmoid(gate_ref[...]).T
    outs = []
    for h in range(HPG):
        hs = slice(h * tq, (h + 1) * tq)
        merged = (gates_t[h:h + 1, :] * o_c[:, hs] + gates_t[HPG + h:HPG + h + 1, :] * o_s[:, hs]
                  + gates_t[2 * HPG + h:2 * HPG + h + 1, :] * o_w[:, hs])
        outs.append(merged.T)
    o_ref[...] = jnp.concatenate(outs, axis=1).astype(o_ref.dtype)


def _nsa_attn(proj, kvc, band_t, imap_t, bias_sel, bias_win, n_batch, t):
    nqb = t // Q_BLOCK
    nseg = t // CMP_STRIDE
    nslc = t // SLC_LEN
    nsel = min(N_SEL, nslc)
    d_q = N_HEADS * HEAD_DIM
    gcols = HPG * HEAD_DIM
    kv0 = d_q // HEAD_DIM
    gate0 = kv0 + 6 * N_KV

    def kv_spec(branch, kv):
        return pl.BlockSpec((t, HEAD_DIM), lambda b, g, i: (b, kv0 + (branch * 2 + kv) * N_KV + g))

    return pl.pallas_call(
        functools.partial(_nsa_attn_kernel, nslc=nslc, nsel=nsel),
        out_shape=jax.ShapeDtypeStruct((n_batch * t, d_q), BF16),
        grid=(n_batch, N_KV, nqb),
        in_specs=[pl.BlockSpec((Q_BLOCK, gcols), lambda b, g, i: (b * nqb + i, g)),
                  pl.BlockSpec((1, 1, 1, nseg, HEAD_DIM), lambda b, g, i: (b, 0, g, 0, 0)),
                  pl.BlockSpec((1, 1, 1, nseg, HEAD_DIM), lambda b, g, i: (b, 1, g, 0, 0)),
                  pl.BlockSpec((HPG, 2 * nseg, Q_BLOCK), lambda b, g, i: (g, 0, 0)),
                  pl.BlockSpec((nslc, nseg), lambda b, g, i: (0, 0)),
                  kv_spec(1, 0), kv_spec(1, 1), kv_spec(2, 0), kv_spec(2, 1),
                  pl.BlockSpec((1, SEL_CHUNK, HPG * Q_BLOCK), lambda b, g, i: (g, 0, 0)),
                  pl.BlockSpec((1, WINDOW + Q_BLOCK, HPG * Q_BLOCK), lambda b, g, i: (g, 0, 0)),
                  pl.BlockSpec((Q_BLOCK, HEAD_DIM), lambda b, g, i: (b * nqb + i, gate0 + g))],
        out_specs=pl.BlockSpec((Q_BLOCK, gcols), lambda b, g, i: (b * nqb + i, g)),
        scratch_shapes=[pltpu.VMEM((KEY_PAD + t, 2 * HEAD_DIM), BF16), pltpu.VMEM((VAL_ROWS, KEY_PAD + t), BF16),
                        pltpu.VMEM((KEY_PAD + t, 2 * HEAD_DIM), BF16), pltpu.VMEM((VAL_ROWS, KEY_PAD + t), BF16)],
        compiler_params=_cparams(("parallel", "parallel", "arbitrary")),
        name="nsa_attn",
    )(proj, kvc, kvc, band_t, imap_t, proj, proj, proj, proj, bias_sel, bias_win, proj)


def _proj_res_kernel(a_ref, w_ref, x_ref, gt_ref, lg_ref, lb_ref, o_ref):
    y = _dot(a_ref[...], w_ref[...])
    z = DN_ALPHA * x_ref[...] + (1.0 + gt_ref[0]) * y
    o_ref[...] = _ln_rows(z, lg_ref[...], lb_ref[...])


def _proj_res(a, w, x, mod, mod_spec, gate_slot, ln_g, ln_b, *, layer, tm):
    m, d = x.shape
    k = a.shape[1]
    vec_spec = pl.BlockSpec((1, d), lambda i: (0, 0))
    return pl.pallas_call(
        _proj_res_kernel,
        out_shape=jax.ShapeDtypeStruct((m, d), F32),
        grid=(m // tm,),
        in_specs=[pl.BlockSpec((tm, k), lambda i: (i, 0)),
                  pl.BlockSpec((None, k, d), lambda i: (layer, 0, 0)),
                  pl.BlockSpec((tm, d), lambda i: (i, 0)),
                  mod_spec(gate_slot),
                  vec_spec, vec_spec],
        out_specs=pl.BlockSpec((tm, d), lambda i: (i, 0)),
        compiler_params=_cparams(("parallel",)),
        name="proj_res",
    )(a, w, x, mod, ln_g.reshape(1, d), ln_b.reshape(1, d))


def _rel_bucket_np(dist):
    dist = np.maximum(dist, 0)
    exact = REL_BUCKETS // 2
    logv = (np.log(np.maximum(dist, 1).astype(np.float32) / np.float32(exact))
            / np.float32(math.log(REL_MAX_DIST / exact))).astype(np.float32)
    large = exact + (logv * np.float32(REL_BUCKETS - exact)).astype(np.int32)
    return np.where(dist < exact, dist, np.minimum(large, REL_BUCKETS - 1)).astype(np.int32)


def _cmp_to_slc_np(nseg, nslc):
    i = np.arange(nseg)[:, None] * CMP_STRIDE
    j = np.arange(nslc)[None, :] * SLC_LEN
    ov = np.minimum(i + CMP_LEN, j + SLC_LEN) - np.maximum(i, j)
    return (np.maximum(ov, 0) / CMP_STRIDE).astype(np.float32)


def _bias_tables(rel_table, t):
    tbl_c = rel_table.T - rel_table[REL_BUCKETS - 1][:, None]
    nseg = t // CMP_STRIDE
    ql = np.arange(Q_BLOCK)[None, :]
    dist_c = ql - ((np.arange(2 * nseg)[:, None] - nseg) * CMP_STRIDE + CMP_LEN - 1)

    def lookup(dist):
        bucket = _rel_bucket_np(dist)
        out = jnp.broadcast_to(tbl_c[:, 0][:, None, None], (N_HEADS,) + dist.shape)
        for k in range(1, REL_BUCKETS):
            out = jnp.where(bucket == k, tbl_c[:, k][:, None, None], out)
        return out

    band_t = jnp.where(dist_c >= 0, lookup(dist_c), NEG_BIAS)

    def per_group(dist, ok):
        tab = jnp.where(ok, lookup(dist), NEG_MASK)
        n_keys = dist.shape[0]
        return tab.reshape(N_KV, HPG, n_keys, Q_BLOCK).transpose(0, 2, 1, 3).reshape(N_KV, n_keys, HPG * Q_BLOCK)

    dist_s = ql + (SEL_CHUNK - Q_BLOCK) - np.arange(SEL_CHUNK)[:, None]
    dist_w = ql + WINDOW - np.arange(WINDOW + Q_BLOCK)[:, None]
    return band_t, per_group(dist_s, dist_s >= 0), per_group(dist_w, (dist_w >= 0) & (dist_w < WINDOW))


CMP_PAGES_PER_STEP = 16


def _compress_paged_kernel(pt_ref, *refs, n_pg):
    pages = refs[:n_pg]
    pe_ref, w1a_ref, w1b_ref, w2_ref, o_ref, carry_ref = refs[n_pg:]
    n_c = 2 * N_KV
    seg_rows = CMP_STRIDE * n_c
    segs = PAGE_SIZE // CMP_STRIDE
    rows = n_pg * segs * n_c
    hid_w = w2_ref.shape[0]

    @pl.when(pl.program_id(1) == 0)
    def _():
        carry_ref[...] = jnp.zeros_like(carry_ref)

    a = jnp.concatenate(
        [jnp.concatenate(
            [jnp.concatenate([pg[0, 0, seg_rows * s + n_c * r:seg_rows * s + n_c * (r + 1), :]
                              for s in range(segs)], axis=0)
             for r in range(CMP_STRIDE)], axis=1)
         for pg in pages], axis=0).astype(BF16)
    is_k = lax.broadcasted_iota(jnp.int32, (rows, 1), 0) % n_c < N_KV
    is_k8 = lax.broadcasted_iota(jnp.int32, (n_c, 1), 0) < N_KV

    def pick(y, width, k_rows):
        return jnp.where(k_rows, y[:, 0:width], y[:, width:2 * width])

    p0 = pick(_dot(a, w1a_ref[...]), hid_w, is_k)
    p1 = pick(_dot(a, w1b_ref[...]), hid_w, is_k)
    half = CMP_STRIDE * HEAD_DIM
    pe = pe_ref[...].astype(BF16)
    peh = pick(_dot(pe[:, 0:half], w1a_ref[...]) + _dot(pe[:, half:2 * half], w1b_ref[...]), hid_w, is_k8)
    shifted = jnp.concatenate([carry_ref[...], p0[0:rows - n_c]], axis=0)
    carry_ref[...] = p0[rows - n_c:rows]
    hid = (shifted + p1).reshape(rows // n_c, n_c, hid_w) + peh[None]
    y = jax.nn.gelu(hid.reshape(rows, hid_w)).astype(BF16)
    o_ref[0] = pick(_dot(y, w2_ref[...]), HEAD_DIM, is_k)


def _compress_paged(cache_pages, page_table, page0, pe, w1, w2):
    n_s, n_pages = page_table.shape
    n_pg = min(CMP_PAGES_PER_STEP, n_pages)
    n_c = 2 * N_KV
    half = CMP_STRIDE * HEAD_DIM
    rows_pg = PAGE_SIZE * n_c
    out_rows = n_pg * (PAGE_SIZE // CMP_STRIDE) * n_c
    hid_w = w1.shape[2]
    w1a = jnp.concatenate([w1[0, :half], w1[1, :half]], axis=1)
    w1b = jnp.concatenate([w1[0, half:], w1[1, half:]], axis=1)
    w2c = jnp.concatenate([w2[0], w2[1]], axis=1)
    pe8 = jnp.repeat(pe.reshape(2, 2 * half), N_KV, axis=0)

    def page_spec(p):
        return pl.BlockSpec((1, 1, rows_pg, HEAD_DIM), lambda b, c, pt: (0, page0 + pt[b, c * n_pg + p], 0, 0))

    const = lambda b, c, pt: (0, 0)
    grid_spec = pltpu.PrefetchScalarGridSpec(
        num_scalar_prefetch=1,
        grid=(n_s, n_pages // n_pg),
        in_specs=[page_spec(p) for p in range(n_pg)] + [
            pl.BlockSpec((n_c, 2 * half), const),
            pl.BlockSpec((half, 2 * hid_w), const),
            pl.BlockSpec((half, 2 * hid_w), const),
            pl.BlockSpec((hid_w, 2 * HEAD_DIM), const)],
        out_specs=pl.BlockSpec((1, out_rows, HEAD_DIM), lambda b, c, pt: (b, c, 0)),
        scratch_shapes=[pltpu.VMEM((n_c, hid_w), F32)])
    return pl.pallas_call(
        functools.partial(_compress_paged_kernel, n_pg=n_pg),
        out_shape=jax.ShapeDtypeStruct((n_s, n_pages * (PAGE_SIZE // CMP_STRIDE) * n_c, HEAD_DIM), F32),
        grid_spec=grid_spec,
        compiler_params=_cparams(("parallel", "arbitrary")),
        name="compress_paged",
    )(page_table, *([cache_pages[None]] * n_pg), pe8, w1a, w1b, w2c)


def _sample_cmp_kernel(q_ref, kv_ref, bias_ref, im_ref, oc_ref, idx_ref, *, nslc, nsel, pos):
    n_c = 2 * N_KV
    q = (q_ref[0] * ATT_SCALE).astype(BF16)
    kv = kv_ref[0].astype(BF16)
    bias = bias_ref[...]
    valid = bias > 0.5 * NEG_BIAS
    s = _dot_nt(q, kv) + bias
    m = jnp.max(s, axis=-1, keepdims=True)
    e = jnp.where(valid, jnp.exp(s - m), 0.0)
    den = jnp.sum(e, axis=-1, keepdims=True)
    p = e / jnp.where(den > 0.0, den, 1.0)
    oc_ref[0] = _dot(pltpu.roll(p, N_KV, 1).astype(BF16), kv)

    p_hi = p.astype(BF16)
    p_lo = (p - p_hi.astype(F32)).astype(BF16)
    imp_h = _dot(p_hi, im_ref[...]) + _dot(p_lo, im_ref[...])
    lanes = im_ref.shape[1]
    head_grp = lax.broadcasted_iota(jnp.int32, (N_HEADS, lanes), 0) // HPG
    row8 = lax.broadcasted_iota(jnp.int32, (n_c, lanes), 0)
    imp = jnp.zeros((n_c, lanes), F32)
    for g in range(N_KV):
        imp_g = jnp.sum(jnp.where(head_grp == g, imp_h, 0.0), axis=0, keepdims=True)
        imp = imp + jnp.where(row8 == g, imp_g, 0.0)
    jidx = lax.broadcasted_iota(jnp.int32, (n_c, lanes), 1)
    cur = pos // SLC_LEN
    valid_blk = (jidx * SLC_LEN <= pos) & (jidx < nslc)
    forced = (jidx == 0) | (jidx == cur) | (jidx == cur - 1)
    score = jnp.where(valid_blk, imp + jnp.where(forced, FORCE_BONUS, 0.0), -jnp.inf)
    cnt = jnp.zeros((n_c, lanes), jnp.int32)
    for jp in range(nslc):
        col = score[:, jp:jp + 1]
        ahead = (col > score) | ((col == score) & (jidx > jp))
        cnt = cnt + jnp.where(ahead, 1, 0)
    out_lane = lax.broadcasted_iota(jnp.int32, (n_c, HEAD_DIM), 1)
    jf = jidx.astype(F32)
    out = jnp.zeros((n_c, HEAD_DIM), F32)
    for r in range(nsel):
        blk_r = jnp.sum(jnp.where((cnt == r) & valid_blk, jf, 0.0), axis=-1, keepdims=True)
        out = jnp.where(out_lane == r, blk_r, out)
    idx_ref[0] = out.astype(jnp.int32)


def _sample_cmp(q_heads, kvc, bias, imap, *, nslc, nsel, pos):
    n_s, rows, _ = kvc.shape
    lanes = imap.shape[1]
    return pl.pallas_call(
        functools.partial(_sample_cmp_kernel, nslc=nslc, nsel=nsel, pos=pos),
        out_shape=(jax.ShapeDtypeStruct((n_s, N_HEADS, HEAD_DIM), F32),
                   jax.ShapeDtypeStruct((n_s, 2 * N_KV, HEAD_DIM), jnp.int32)),
        grid=(n_s,),
        in_specs=[pl.BlockSpec((1, N_HEADS, HEAD_DIM), lambda b: (b, 0, 0)),
                  pl.BlockSpec((1, rows, HEAD_DIM), lambda b: (b, 0, 0)),
                  pl.BlockSpec((N_HEADS, rows), lambda b: (0, 0)),
                  pl.BlockSpec((rows, lanes), lambda b: (0, 0))],
        out_specs=(pl.BlockSpec((1, N_HEADS, HEAD_DIM), lambda b: (b, 0, 0)),
                   pl.BlockSpec((1, 2 * N_KV, HEAD_DIM), lambda b: (b, 0, 0))),
        compiler_params=_cparams(("parallel",)),
        name="sample_cmp",
    )(q_heads, kvc, bias, imap)


def _sample_sel_win_kernel(idx_ref, pt_ref, *refs, n_slot, pos, n_past_blk, buf_len, thresholds):
    slots = refs[:n_slot]
    q_ref, new_ref, win_ref, tb_ref, oc_ref, gate_ref, o_ref = refs[n_slot:]
    b = pl.program_id(0)
    g = pl.program_id(1)
    n_c = 2 * N_KV
    blk_lanes = SLC_LEN * n_c
    q = (q_ref[0, 0] * ATT_SCALE).astype(BF16)
    tb = tb_ref[0]

    def bias_of(dist):
        d = jnp.maximum(dist, 0)
        out = jnp.zeros(d.shape, F32) + tb[:, 0:1]
        for k in range(1, REL_BUCKETS):
            out = out + jnp.where(d >= thresholds[k], tb[:, k:k + 1] - tb[:, k - 1:k], 0.0)
        return out

    def softmax(s, ok):
        s = jnp.where(ok, s, NEG_BIAS)
        m = jnp.max(s, axis=-1, keepdims=True)
        e = jnp.where(ok, jnp.exp(s - m), 0.0)
        den = jnp.sum(e, axis=-1, keepdims=True)
        return e / jnp.where(den > 0.0, den, 1.0)

    new_rows = new_ref[0, 0]
    s_new = _dot_nt(q, new_rows.astype(BF16))
    lane_new = lax.broadcasted_iota(jnp.int32, s_new.shape, 1)
    bias_new = bias_of(jnp.zeros(s_new.shape, jnp.int32))

    lane = lax.broadcasted_iota(jnp.int32, (n_c, blk_lanes), 1)
    mine = lane % n_c == g
    r_in = lane // n_c
    base = (b * N_KV + g) * n_slot
    s_parts, d_parts, ok_parts = [], [], []
    n_new = 0
    for i in range(n_slot):
        j = idx_ref[base + i]
        dist = jnp.where(j < n_past_blk, pos - j * SLC_LEN, -1) - r_in
        s_parts.append(_dot_nt(q, slots[i][0, 0].astype(BF16)))
        d_parts.append(dist)
        ok_parts.append(mine & (dist >= 0))
        n_new = n_new + jnp.where(j >= n_past_blk, 1, 0)
    s_all = jnp.concatenate(s_parts + [s_new], axis=1)
    ok_all = jnp.concatenate(ok_parts + [lane_new == jnp.where(n_new > 0, 0, -1)], axis=1)
    b_all = jnp.concatenate([bias_of(jnp.concatenate(d_parts, axis=1)), bias_new], axis=1)
    p = softmax(s_all + b_all, ok_all)
    n_old = n_slot * blk_lanes
    p_v = pltpu.roll(p[:, 0:n_old], N_KV, 1).astype(BF16)
    o_s = p[:, n_old:n_old + 1] * new_rows[1:2, :]
    for i in range(n_slot):
        o_s = o_s + _dot(p_v[:, i * blk_lanes:(i + 1) * blk_lanes], slots[i][0, 0].astype(BF16))

    win = win_ref[0].astype(BF16)
    lane_w = lax.broadcasted_iota(jnp.int32, (n_c, buf_len * n_c), 1)
    dist_w = buf_len - lane_w // n_c
    ok_w = (lane_w % n_c == g) & (dist_w >= 0) & (dist_w < WINDOW)
    s_w = jnp.concatenate([_dot_nt(q, win) + bias_of(dist_w), s_new + bias_new], axis=1)
    p = softmax(s_w, jnp.concatenate([ok_w, lane_new == 2], axis=1))
    n_old = buf_len * n_c
    o_w = _dot(pltpu.roll(p[:, 0:n_old], N_KV, 1).astype(BF16), win) + p[:, n_old + 2:n_old + 3] * new_rows[3:4, :]

    gates = jax.nn.sigmoid(gate_ref[0, 0])
    o_ref[0, 0] = gates[:, 0:1] * oc_ref[0, 0] + gates[:, 1:2] * o_s + gates[:, 2:3] * o_w


def _sample_sel_win(idx, page_table, slc_halves, page0, q_g, new_rows, win_rows, tb, o_c, gates, *, pos, nsel):
    n_s, n_pages = page_table.shape
    n_c = 2 * N_KV
    per_page = PAGE_SIZE // SLC_LEN
    n_past_blk = n_pages * per_page
    buf_len = win_rows.shape[1] // n_c
    blk_rows = SLC_LEN * n_c
    thresholds = tuple(int(np.argmax(_rel_bucket_np(np.arange(4 * REL_MAX_DIST)) >= k)) for k in range(REL_BUCKETS))

    def slot_spec(i):
        def index_map(b, g, idx_ref, pt_ref):
            j = jnp.minimum(idx_ref[(b * N_KV + g) * nsel + i], n_past_blk - 1)
            return (page0 + pt_ref[b, j // per_page], j % per_page, 0, 0)
        return pl.BlockSpec((1, 1, blk_rows, HEAD_DIM), index_map)

    grp = lambda b, g, idx_ref, pt_ref: (b, g, 0, 0)
    grid_spec = pltpu.PrefetchScalarGridSpec(
        num_scalar_prefetch=2,
        grid=(n_s, N_KV),
        in_specs=[slot_spec(i) for i in range(nsel)] + [
            pl.BlockSpec((1, 1, n_c, HEAD_DIM), grp),
            pl.BlockSpec((1, 1, HEAD_DIM, HEAD_DIM), grp),
            pl.BlockSpec((1, buf_len * n_c, HEAD_DIM), lambda b, g, idx_ref, pt_ref: (b, 0, 0)),
            pl.BlockSpec((1, n_c, HEAD_DIM), lambda b, g, idx_ref, pt_ref: (g, 0, 0)),
            pl.BlockSpec((1, 1, n_c, HEAD_DIM), grp),
            pl.BlockSpec((1, 1, n_c, HEAD_DIM), grp)],
        out_specs=pl.BlockSpec((1, 1, n_c, HEAD_DIM), grp))
    return pl.pallas_call(
        functools.partial(_sample_sel_win_kernel, n_slot=nsel, pos=pos, n_past_blk=n_past_blk, buf_len=buf_len,
                          thresholds=thresholds),
        out_shape=jax.ShapeDtypeStruct((n_s, N_KV, n_c, HEAD_DIM), F32),
        grid_spec=grid_spec,
        compiler_params=_cparams(("parallel", "parallel")),
        name="sample_sel_win",
    )(idx, page_table, *([slc_halves] * nsel), q_g, new_rows, win_rows, tb, o_c, gates)


def _sample_nsa(proj_s, cmp_pages, slc_halves, page0, win_buf, page_table, pe, w1, w2, rel_table):
    n_s, n_pages = page_table.shape
    past = n_pages * PAGE_SIZE
    pos = past
    n_c = 2 * N_KV
    d_q = N_HEADS * HEAD_DIM
    kvw = n_c * HEAD_DIM
    nseg = past // CMP_STRIDE
    nslc = -(-(past + 1) // SLC_LEN)
    nsel = min(N_SEL, nslc)
    kvc = _compress_paged(cmp_pages, page_table, page0, pe, w1, w2)
    x = np.arange(nseg)[:, None]
    c = np.arange(n_c)[None, :]
    dist = pos - ((x - 1) * CMP_STRIDE + CMP_LEN - 1)
    head_grp = np.arange(N_HEADS)[:, None, None] // HPG
    ok = ((x >= 1) & (dist >= 0))[None] & (c[None] == head_grp)
    bias = jnp.where(ok, jnp.take(rel_table.T, _rel_bucket_np(np.broadcast_to(dist, (nseg, n_c))), axis=1),
                     NEG_BIAS).reshape(N_HEADS, nseg * n_c)
    lanes = -(-nslc // HEAD_DIM) * HEAD_DIM
    imap = np.zeros((nseg, n_c, lanes), np.float32)
    imap[1:, :, :nslc] = _cmp_to_slc_np(nseg - 1, nslc)[:, None, :]
    q_heads = proj_s[:, :d_q].reshape(n_s, N_HEADS, HEAD_DIM)
    o_c, idx = _sample_cmp(q_heads, kvc, bias, jnp.asarray(imap.reshape(nseg * n_c, lanes)).astype(BF16),
                           nslc=nslc, nsel=nsel, pos=pos)
    pad_heads = lambda a: jnp.pad(a.reshape(n_s, N_KV, HPG, -1), ((0, 0), (0, 0), (0, n_c - HPG), (0, 0)))
    kv_new = proj_s[:, d_q:d_q + 3 * kvw].reshape(n_s, 3, 2, N_KV, HEAD_DIM)
    new_rows = kv_new[:, 1:3].transpose(0, 3, 1, 2, 4).reshape(n_s, N_KV, 4, HEAD_DIM)
    new_rows = jnp.pad(new_rows, ((0, 0), (0, 0), (0, HEAD_DIM - 4), (0, 0)))
    gates = proj_s[:, d_q + 3 * kvw:].reshape(n_s, N_KV, HEAD_DIM)[:, :, :3 * HPG].reshape(n_s, N_KV, 3, HPG)
    gates = jnp.pad(gates.transpose(0, 1, 3, 2), ((0, 0), (0, 0), (0, n_c - HPG), (0, HEAD_DIM - 3)))
    tb = jnp.pad(rel_table.T.reshape(N_KV, HPG, REL_BUCKETS), ((0, 0), (0, n_c - HPG), (0, HEAD_DIM - REL_BUCKETS)))
    o = _sample_sel_win(idx[:, :N_KV, :nsel].reshape(-1), page_table, slc_halves, page0,
                        pad_heads(proj_s[:, :d_q]), new_rows, win_buf.reshape(n_s, -1, HEAD_DIM), tb,
                        pad_heads(o_c), gates, pos=pos, nsel=nsel)
    new_win = jnp.concatenate([win_buf, kv_new[:, 2][:, None]], axis=1)[:, -min(WINDOW, win_buf.shape[1] + 1):]
    return o[:, :, :HPG].reshape(n_s, d_q), kv_new, new_win


def _pad_rows(a, rows):
    return jnp.pad(a, ((0, rows - a.shape[0]),) + ((0, 0),) * (a.ndim - 1))


def _nsa_w_in_layout(w):
    n_l, d, _ = w.shape
    n_main = N_HEADS * HEAD_DIM + 6 * N_KV * HEAD_DIM
    wg = w[:, :, n_main:].reshape(n_l, d, 3, N_KV, HPG).transpose(0, 1, 3, 2, 4).reshape(n_l, d, N_KV, 3 * HPG)
    wg = jnp.pad(wg, ((0, 0), (0, 0), (0, 0), (0, HEAD_DIM - 3 * HPG))).reshape(n_l, d, N_KV * HEAD_DIM)
    return jnp.concatenate([w[:, :, :n_main], wg], axis=2).astype(BF16)


def kernel(x_prompt, x_sample, cache_cmp_kv, cache_slc_kv, state_win_kv, page_table, c_prompt, c_sample,
           ada_w, ada_b, ln_g, ln_b, ffn_pre_w_in, ffn_pre_w_out, ffn_post_w_in, ffn_post_w_out,
           gmlp_w_in, gmlp_ln_g, gmlp_ln_b, gmlp_w_s, gmlp_b_s, gmlp_w_out,
           nsa_w_in, nsa_cmp_pe, nsa_cmp_w1, nsa_cmp_w2, nsa_w_out, rel_table):
    n_b, t, d = x_prompt.shape
    n_s = x_sample.shape[0]
    n_phys = cache_cmp_kv.shape[1]
    d_q = N_HEADS * HEAD_DIM
    kvw = 2 * N_KV * HEAD_DIM
    tm_p = min(512, t)
    tm_f = min(512, t)
    tm_mm = min(1024, t)
    tm_g = min(512, t)

    xp = x_prompt.reshape(n_b * t, d)
    place = lambda a: jnp.pad(a, ((n_b, SAMPLE_ROWS - n_b - n_s), (0, 0)))
    xs = place(x_sample.reshape(n_s, d))
    c_all = _pad_rows(jnp.concatenate([c_prompt, c_sample], axis=0), SAMPLE_ROWS)
    mod = _ada(c_all, ada_w, ada_b)
    mod_rows = mod.reshape(DEPTH * SAMPLE_ROWS * N_ADA, 1, d)

    def prompt_mod(layer, tm):
        return lambda k: pl.BlockSpec(
            (1, 1, d), lambda i, *_: ((layer * SAMPLE_ROWS + i // (t // tm)) * N_ADA + k, 0, 0))

    def sample_mod(layer):
        return lambda k: pl.BlockSpec((1, SAMPLE_ROWS, d), lambda i, *_: (layer, 0, k))

    band_t, bias_sel, bias_win = _bias_tables(rel_table, t)
    imap_t = jnp.asarray(_cmp_to_slc_np(t // CMP_STRIDE, t // SLC_LEN).T).astype(BF16)
    cmp_pages = cache_cmp_kv.reshape(-1, PAGE_SIZE * 2 * N_KV, HEAD_DIM)
    slc_halves = cache_slc_kv.reshape(-1, PAGE_SIZE // SLC_LEN, SLC_LEN * 2 * N_KV, HEAD_DIM)

    pre_w_in, pre_w_out = ffn_pre_w_in.astype(BF16), ffn_pre_w_out.astype(BF16)
    post_w_in, post_w_out = ffn_post_w_in.astype(BF16), ffn_post_w_out.astype(BF16)
    gw_in, gw_out = gmlp_w_in.astype(BF16), gmlp_w_out.astype(BF16)
    nw_in, nw_out = _nsa_w_in_layout(nsa_w_in), nsa_w_out.astype(BF16)

    cmp_p, cmp_s, slc_p, slc_s, win_p, win_s, gv_s = [], [], [], [], [], [], []
    for i in range(DEPTH):
        ms = sample_mod(i)
        xp = _ffn(xp, mod_rows, prompt_mod(i, tm_f), (0, 1, 2), pre_w_in, pre_w_out, ln_g[i, 0], ln_b[i, 0],
                  layer=i, tm=tm_f)
        xs = _ffn(xs, mod, ms, (0, 1, 2), pre_w_in, pre_w_out, ln_g[i, 0], ln_b[i, 0], layer=i, tm=SAMPLE_ROWS)

        if i % 2 == 0:
            a = i // 2
            xp, = _gmlp(xp, mod_rows, prompt_mod(i, tm_g), (3, 4, 5), gw_in, gmlp_ln_g[a], gmlp_ln_b[a], gmlp_w_s[a],
                        gmlp_b_s[a], gw_out, ln_g[i, 1], ln_b[i, 1], layer=a, tm=tm_g, chunk=min(CHUNK, t),
                        emit_vn=False)
            xs, vn_s = _gmlp(xs, mod, ms, (3, 4, 5), gw_in, gmlp_ln_g[a], gmlp_ln_b[a], gmlp_w_s[a], gmlp_b_s[a],
                             gw_out, ln_g[i, 1], ln_b[i, 1], layer=a, tm=SAMPLE_ROWS, chunk=1, emit_vn=True)
            gv_s.append(vn_s[n_b:n_b + n_s].reshape(n_s, 1, -1))
        else:
            a = i // 2
            w1 = nsa_cmp_w1[a].astype(BF16)
            w2 = nsa_cmp_w2[a].astype(BF16)
            proj = _mod_mm(xp, mod_rows, prompt_mod(i, tm_mm), (3, 4), nw_in, layer=a, tm=tm_mm, tn=512)
            kvc = _compress(proj, n_b, t, d_q // HEAD_DIM, nsa_cmp_pe[a], w1, w2)
            o = _nsa_attn(proj, kvc, band_t, imap_t, bias_sel, bias_win, n_b, t)
            xp = _proj_res(o, nw_out, xp, mod_rows, prompt_mod(i, tm_p), 5, ln_g[i, 1], ln_b[i, 1], layer=a, tm=tm_p)
            kv_rows = lambda branch, rows: rows[..., d_q + branch * kvw:d_q + (branch + 1) * kvw].reshape(
                n_b, -1, 2, N_KV, HEAD_DIM)
            cmp_p.append(kv_rows(0, proj))
            slc_p.append(kv_rows(1, proj))
            win_p.append(kv_rows(2, proj.reshape(n_b, t, -1)[:, -min(WINDOW, t):]))
            proj_s = _mod_mm(xs, mod, ms, (3, 4), nw_in, layer=a, tm=SAMPLE_ROWS, tn=512)[n_b:n_b + n_s]
            o_samp, kv_new, new_win = _sample_nsa(proj_s, cmp_pages, slc_halves, a * n_phys, state_win_kv[a],
                                                  page_table, nsa_cmp_pe[a], w1, w2, rel_table)
            xs = _proj_res(place(o_samp).astype(BF16), nw_out, xs, mod, ms, 5, ln_g[i, 1], ln_b[i, 1],
                           layer=a, tm=SAMPLE_ROWS)
            cmp_s.append(kv_new[:, 0][:, None])
            slc_s.append(kv_new[:, 1][:, None])
            win_s.append(new_win)

        xp = _ffn(xp, mod_rows, prompt_mod(i, tm_f), (6, 7, 8), post_w_in, post_w_out, ln_g[i, 2], ln_b[i, 2],
                  layer=i, tm=tm_f)
        xs = _ffn(xs, mod, ms, (6, 7, 8), post_w_in, post_w_out, ln_g[i, 2], ln_b[i, 2], layer=i, tm=SAMPLE_ROWS)

    return (xp.reshape(n_b, t, d), xs[n_b:n_b + n_s].reshape(n_s, 1, d),
            jnp.stack(cmp_p), jnp.stack(cmp_s), jnp.stack(slc_p), jnp.stack(slc_s),
            jnp.stack(win_p), jnp.stack(win_s), jnp.stack(gv_s))
```

```python
import functools
import math

import numpy as np
import jax
import jax.numpy as jnp
from jax import lax
from jax.experimental import pallas as pl
from jax.experimental.pallas import tpu as pltpu

F32 = jnp.float32
BF16 = jnp.bfloat16

DEPTH = 4
N_ADA = 9
N_HEADS = 16
HEAD_DIM = 128
N_KV = 4
HPG = N_HEADS // N_KV
GMLP_GROUPS = 16
CHUNK = 128
PAGE_SIZE = 128
CMP_LEN = 32
CMP_STRIDE = 16
SLC_LEN = 64
N_SEL = 16
WINDOW = 512
Q_BLOCK = 128
REL_BUCKETS = 32
REL_MAX_DIST = 128
LN_EPS = 1e-5
DN_ALPHA = (2 * DEPTH) ** 0.25
FORCE_BONUS = 1e4
ATT_SCALE = HEAD_DIM ** -0.5

SAMPLE_ROWS = 16
NEG_MASK = -30000.0
NEG_BIAS = -1e30
SEL_MASK = 32768.0
KEY_PAD = WINDOW
SEL_CHUNK = 512
VAL_ROWS = HEAD_DIM + 16
VMEM_LIMIT = 56 * 1024 * 1024


def _cparams(sem):
    return pltpu.CompilerParams(dimension_semantics=sem, vmem_limit_bytes=VMEM_LIMIT)


def _dot(a, b):
    return jnp.dot(a, b, preferred_element_type=F32)


def _dot_nt(a, b):
    return lax.dot_general(a, b, (((1,), (1,)), ((), ())), preferred_element_type=F32)


def _ln_rows(z, g, b):
    mu = jnp.mean(z, axis=-1, keepdims=True)
    zc = z - mu
    var = jnp.mean(zc * zc, axis=-1, keepdims=True)
    return zc * lax.rsqrt(var + LN_EPS) * g + b


def _silu(x):
    return x * jax.nn.sigmoid(x)


def _ada_kernel(c_ref, w_ref, b_ref, o_ref):
    h = _silu(c_ref[...]).astype(BF16)
    o_ref[0] = _dot(h, w_ref[0].astype(BF16)) + b_ref[0]


def _ada(c_all, ada_w, ada_b, tn=1024):
    depth, d, n = ada_w.shape
    r = c_all.shape[0]
    return pl.pallas_call(
        _ada_kernel,
        out_shape=jax.ShapeDtypeStruct((depth, r, n), F32),
        grid=(depth, n // tn),
        in_specs=[pl.BlockSpec((r, d), lambda l, j: (0, 0)),
                  pl.BlockSpec((1, d, tn), lambda l, j: (l, 0, j)),
                  pl.BlockSpec((1, 1, tn), lambda l, j: (l, 0, j))],
        out_specs=pl.BlockSpec((1, r, tn), lambda l, j: (l, 0, j)),
        compiler_params=_cparams(("parallel", "parallel")),
        name="ada",
    )(c_all, ada_w, ada_b.reshape(depth, 1, n))


def _ffn_kernel(x_ref, sh_ref, sc_ref, gt_ref, wg_ref, wu_ref, wo_ref, lg_ref, lb_ref, o_ref, h_ref):
    j = pl.program_id(1)

    @pl.when(j == 0)
    def _():
        h_ref[...] = (x_ref[...] * (1.0 + sc_ref[0]) + sh_ref[0]).astype(BF16)
        o_ref[...] = jnp.zeros_like(o_ref)

    h = h_ref[...]
    g = _dot(h, wg_ref[...])
    u = _dot(h, wu_ref[...])
    a = (_silu(g) * u).astype(BF16)
    o_ref[...] += _dot(a, wo_ref[...])

    @pl.when(j == pl.num_programs(1) - 1)
    def _():
        z = DN_ALPHA * x_ref[...] + (1.0 + gt_ref[0]) * (0.5 * o_ref[...])
        o_ref[...] = _ln_rows(z, lg_ref[...], lb_ref[...])


def _ffn(x, mod, mod_spec, slots, w_in, w_out, ln_g, ln_b, *, layer, tm, tf=512):
    m, d = x.shape
    dff = w_out.shape[1]
    nf = dff // tf
    vec_spec = pl.BlockSpec((1, d), lambda i, j: (0, 0))
    return pl.pallas_call(
        _ffn_kernel,
        out_shape=jax.ShapeDtypeStruct((m, d), F32),
        grid=(m // tm, nf),
        in_specs=[pl.BlockSpec((tm, d), lambda i, j: (i, 0)),
                  mod_spec(slots[0]), mod_spec(slots[1]), mod_spec(slots[2]),
                  pl.BlockSpec((None, d, tf), lambda i, j: (layer, 0, j)),
                  pl.BlockSpec((None, d, tf), lambda i, j: (layer, 0, j + nf)),
                  pl.BlockSpec((None, tf, d), lambda i, j: (layer, j, 0)),
                  vec_spec, vec_spec],
        out_specs=pl.BlockSpec((tm, d), lambda i, j: (i, 0)),
        scratch_shapes=[pltpu.VMEM((tm, d), BF16)],
        compiler_params=_cparams(("parallel", "arbitrary")),
        name="ffn",
    )(x, mod, mod, mod, w_in, w_in, w_out, ln_g.reshape(1, d), ln_b.reshape(1, d))


def _mm_kernel(x_ref, sh_ref, sc_ref, w_ref, o_ref, h_ref):
    @pl.when(pl.program_id(1) == 0)
    def _():
        h_ref[...] = (x_ref[...] * (1.0 + sc_ref[0]) + sh_ref[0]).astype(BF16)

    o_ref[...] = _dot(h_ref[...], w_ref[...])


def _mod_mm(x, mod, mod_spec, slots, w, *, layer, tm, tn):
    m, d = x.shape
    n = w.shape[2]
    return pl.pallas_call(
        _mm_kernel,
        out_shape=jax.ShapeDtypeStruct((m, n), F32),
        grid=(m // tm, n // tn),
        in_specs=[pl.BlockSpec((tm, d), lambda i, j: (i, 0)),
                  mod_spec(slots[0]), mod_spec(slots[1]),
                  pl.BlockSpec((None, d, tn), lambda i, j: (layer, 0, j))],
        out_specs=pl.BlockSpec((tm, tn), lambda i, j: (i, j)),
        scratch_shapes=[pltpu.VMEM((tm, d), BF16)],
        compiler_params=_cparams(("parallel", "arbitrary")),
        name="mod_mm",
    )(x, mod, mod, w)


GMLP_GROUPS_PER_STEP = 2


def _gmlp_kernel(x_ref, sh_ref, sc_ref, gt_ref, win_ref, vlg_ref, vlb_ref, ws_ref, bs_ref,
                 wo_ref, lg_ref, lb_ref, o_ref, *rest, chunk, gw, emit_vn):
    if emit_vn:
        vn_ref, h_ref, v_ref, mu_ref, rs_ref = rest
    else:
        h_ref, v_ref, mu_ref, rs_ref = rest
    k = pl.program_id(1)
    ns = pl.num_programs(1) // 2
    tm = x_ref.shape[0]
    gdim = ns * v_ref.shape[2]

    @pl.when(k == 0)
    def _():
        h_ref[...] = (x_ref[...] * (1.0 + sc_ref[0]) + sh_ref[0]).astype(BF16)
        o_ref[...] = jnp.zeros_like(o_ref)

    y = jax.nn.gelu(_dot(h_ref[...], win_ref[...]))

    @pl.when(k < ns)
    def _():
        v_ref[k] = y

    @pl.when(k == ns)
    def _():
        tot = jnp.zeros((tm, 1), F32)
        for c in range(ns):
            tot = tot + jnp.sum(v_ref[c], axis=-1, keepdims=True)
        mu = tot / gdim
        sq = jnp.zeros((tm, 1), F32)
        for c in range(ns):
            dv = v_ref[c] - mu
            sq = sq + jnp.sum(dv * dv, axis=-1, keepdims=True)
        mu_ref[...] = mu
        rs_ref[...] = lax.rsqrt(sq / gdim + LN_EPS)

    @pl.when(k >= ns)
    def _():
        vn = (v_ref[k - ns] - mu_ref[...]) * rs_ref[...] * vlg_ref[...] + vlb_ref[...]
        if emit_vn:
            vn_ref[...] = vn
        parts = []
        for gi in range(GMLP_GROUPS_PER_STEP):
            vn_g = vn[:, gi * gw:(gi + 1) * gw]
            if chunk == 1:
                s_g = ws_ref[gi, 0:1, 0:1] * vn_g + bs_ref[gi, 0:1, 0:1]
            else:
                row = lax.broadcasted_iota(jnp.int32, (chunk, chunk), 0)
                col = lax.broadcasted_iota(jnp.int32, (chunk, chunk), 1)
                w_tri = jnp.where(col <= row, ws_ref[gi], 0.0).astype(BF16)
                vb = vn_g.astype(BF16)
                s_g = jnp.concatenate(
                    [_dot(w_tri, vb[c * chunk:(c + 1) * chunk]) + bs_ref[gi]
                     for c in range(tm // chunk)], axis=0)
            parts.append(s_g)
        a = (y * jnp.concatenate(parts, axis=1)).astype(BF16)
        o_ref[...] += _dot(a, wo_ref[...])

    @pl.when(k == 2 * ns - 1)
    def _():
        z = DN_ALPHA * x_ref[...] + (1.0 + gt_ref[0]) * o_ref[...]
        o_ref[...] = _ln_rows(z, lg_ref[...], lb_ref[...])


def _gmlp(x, mod, mod_spec, slots, w_in, v_ln_g, v_ln_b, w_s, b_s, w_out, ln_g, ln_b, *, layer, tm, chunk,
          emit_vn):
    m, d = x.shape
    gdim = w_out.shape[1]
    gw = gdim // GMLP_GROUPS
    gw2 = gw * GMLP_GROUPS_PER_STEP
    ns = GMLP_GROUPS // GMLP_GROUPS_PER_STEP
    vec_spec = pl.BlockSpec((1, d), lambda i, k: (0, 0))
    second = lambda k: jnp.maximum(k - ns, 0)
    out_shape = [jax.ShapeDtypeStruct((m, d), F32)]
    out_specs = [pl.BlockSpec((tm, d), lambda i, k: (i, 0))]
    if emit_vn:
        out_shape.append(jax.ShapeDtypeStruct((m, gdim), F32))
        out_specs.append(pl.BlockSpec((tm, gw2), lambda i, k: (i, second(k))))
    return pl.pallas_call(
        functools.partial(_gmlp_kernel, chunk=chunk, gw=gw, emit_vn=emit_vn),
        out_shape=tuple(out_shape),
        grid=(m // tm, 2 * ns),
        in_specs=[pl.BlockSpec((tm, d), lambda i, k: (i, 0)),
                  mod_spec(slots[0]), mod_spec(slots[1]), mod_spec(slots[2]),
                  pl.BlockSpec((None, d, gw2), lambda i, k: (layer, 0, jnp.where(k < ns, k + ns, k - ns))),
                  pl.BlockSpec((1, gw2), lambda i, k: (0, second(k))),
                  pl.BlockSpec((1, gw2), lambda i, k: (0, second(k))),
                  pl.BlockSpec((GMLP_GROUPS_PER_STEP, CHUNK, CHUNK), lambda i, k: (second(k), 0, 0)),
                  pl.BlockSpec((GMLP_GROUPS_PER_STEP, CHUNK, 1), lambda i, k: (second(k), 0, 0)),
                  pl.BlockSpec((None, gw2, d), lambda i, k: (layer, second(k), 0)),
                  vec_spec, vec_spec],
        out_specs=tuple(out_specs),
        scratch_shapes=[pltpu.VMEM((tm, d), BF16), pltpu.VMEM((ns, tm, gw2), F32),
                        pltpu.VMEM((tm, 1), F32), pltpu.VMEM((tm, 1), F32)],
        compiler_params=_cparams(("parallel", "arbitrary")),
        name="gmlp",
    )(x, mod, mod, mod, w_in, v_ln_g.reshape(1, gdim), v_ln_b.reshape(1, gdim), w_s,
      b_s.reshape(GMLP_GROUPS, CHUNK, 1), w_out, ln_g.reshape(1, d), ln_b.reshape(1, d))


def _compress_kernel(r_ref, pe_ref, w1_ref, w2_ref, o_ref, *, nseg):
    half = CMP_STRIDE * HEAD_DIM
    a = jnp.concatenate(
        [r_ref[pl.ds(r, nseg, stride=CMP_STRIDE), :].astype(BF16) for r in range(CMP_STRIDE)], axis=1)
    p0 = _dot(a, w1_ref[0, 0:half, :])
    p1 = _dot(a, w1_ref[0, half:2 * half, :])
    pe = jnp.broadcast_to(pe_ref[0], (8, 2 * half)).astype(BF16)
    peh = _dot(pe, w1_ref[0])[0:1]
    hid = peh + p0 + pltpu.roll(p1, nseg - 1, 0)
    o_ref[0, 0, 0] = _dot(jax.nn.gelu(hid).astype(BF16), w2_ref[0])


def _compress(rows2d, n_batch, t, col0, pe, w1, w2):
    nseg = t // CMP_STRIDE
    return pl.pallas_call(
        functools.partial(_compress_kernel, nseg=nseg),
        out_shape=jax.ShapeDtypeStruct((n_batch, 2, N_KV, nseg, HEAD_DIM), F32),
        grid=(n_batch, 2, N_KV),
        in_specs=[pl.BlockSpec((t, HEAD_DIM), lambda b, kv, g: (b, col0 + kv * N_KV + g)),
                  pl.BlockSpec((1, 1, CMP_LEN * HEAD_DIM), lambda b, kv, g: (kv, 0, 0)),
                  pl.BlockSpec((1, CMP_LEN * HEAD_DIM, w1.shape[2]), lambda b, kv, g: (kv, 0, 0)),
                  pl.BlockSpec((1, w2.shape[1], HEAD_DIM), lambda b, kv, g: (kv, 0, 0))],
        out_specs=pl.BlockSpec((1, 1, 1, nseg, HEAD_DIM), lambda b, kv, g: (b, kv, g, 0, 0)),
        compiler_params=_cparams(("parallel", "parallel", "parallel")),
        name="compress",
    )(rows2d, pe.reshape(2, 1, CMP_LEN * HEAD_DIM), w1, w2)


def _softmax_update_t(state, s, vt):
    m, acc = state
    m_new = jnp.maximum(m, jnp.max(s, axis=0, keepdims=True))
    alpha = jnp.exp(m - m_new)
    p = jnp.exp(s - m_new)
    return m_new, alpha * acc + _dot(vt, p.astype(BF16))


def _nsa_attn_kernel(q_ref, kc_ref, vc_ref, band_ref, imap_ref, ks_ref, vs_ref, kw_ref, vw_ref,
                     bsel_ref, bwin_ref, gate_ref, o_ref, ksa, vst, kwa, vwt, *, nslc, nsel):
    qb = pl.program_id(2)
    tq = Q_BLOCK
    cols = HPG * tq
    t = ks_ref.shape[0]
    nseg = kc_ref.shape[3]

    @pl.when(qb == 0)
    def _():
        lane_p = lax.broadcasted_iota(jnp.int32, (KEY_PAD, HEAD_DIM), 1)
        pad_mark = jnp.where(lane_p == HEAD_DIM - 1, 1.0, 0.0).astype(BF16)
        key_blk = lax.broadcasted_iota(jnp.int32, (t, HEAD_DIM), 0) // SLC_LEN
        lane = lax.broadcasted_iota(jnp.int32, (t, HEAD_DIM), 1)
        for ref in (ksa, kwa):
            ref[0:KEY_PAD, 0:HEAD_DIM] = jnp.zeros((KEY_PAD, HEAD_DIM), BF16)
            ref[0:KEY_PAD, HEAD_DIM:2 * HEAD_DIM] = pad_mark
        ksa[KEY_PAD:KEY_PAD + t, 0:HEAD_DIM] = ks_ref[...].astype(BF16)
        ksa[KEY_PAD:KEY_PAD + t, HEAD_DIM:2 * HEAD_DIM] = jnp.where(lane == key_blk, 1.0, 0.0).astype(BF16)
        kwa[KEY_PAD:KEY_PAD + t, 0:HEAD_DIM] = kw_ref[...].astype(BF16)
        kwa[KEY_PAD:KEY_PAD + t, HEAD_DIM:2 * HEAD_DIM] = jnp.zeros((t, HEAD_DIM), BF16)
        ones_row = lax.broadcasted_iota(jnp.int32, (VAL_ROWS - HEAD_DIM, KEY_PAD + t), 0) == 0
        for ref in (vst, vwt):
            ref[0:HEAD_DIM, 0:KEY_PAD] = jnp.zeros((HEAD_DIM, KEY_PAD), BF16)
            ref[HEAD_DIM:VAL_ROWS, :] = jnp.where(ones_row, 1.0, 0.0).astype(BF16)
        for c in range(t // tq):
            cols_c = slice(KEY_PAD + c * tq, KEY_PAD + (c + 1) * tq)
            vst[0:HEAD_DIM, cols_c] = vs_ref[c * tq:(c + 1) * tq, :].T.astype(BF16)
            vwt[0:HEAD_DIM, cols_c] = vw_ref[c * tq:(c + 1) * tq, :].T.astype(BF16)

    q_t = jnp.concatenate(
        [(q_ref[:, h * HEAD_DIM:(h + 1) * HEAD_DIM] * ATT_SCALE).T for h in range(HPG)], axis=1).astype(BF16)

    off = pl.multiple_of(nseg - (tq // CMP_STRIDE) * qb, 8)
    bias = jnp.concatenate([band_ref[h, pl.ds(off, nseg), :] for h in range(HPG)], axis=1)
    valid = bias > 0.5 * NEG_BIAS
    s = _dot(kc_ref[0, 0, 0].astype(BF16), q_t) + bias
    m = jnp.max(s, axis=0, keepdims=True)
    e = jnp.where(valid, jnp.exp(s - m), 0.0)
    den = jnp.sum(e, axis=0, keepdims=True)
    p = e / jnp.where(den > 0.0, den, 1.0)
    o_c = _dot(vc_ref[0, 0, 0].T.astype(BF16), p.astype(BF16))

    psum = p[:, 0:tq]
    for h in range(1, HPG):
        psum = psum + p[:, h * tq:(h + 1) * tq]
    p_hi = psum.astype(BF16)
    p_lo = (psum - p_hi.astype(F32)).astype(BF16)
    imap = imap_ref[...]
    imp = _dot(imap, p_hi) + _dot(imap, p_lo)
    jidx = lax.broadcasted_iota(jnp.int32, (nslc, tq), 0)
    qpos = qb * tq + lax.broadcasted_iota(jnp.int32, (nslc, tq), 1)
    cur = qpos // SLC_LEN
    valid_blk = jidx * SLC_LEN <= qpos
    forced = (jidx == 0) | (jidx == cur) | (jidx == cur - 1)
    score = jnp.where(valid_blk, imp + jnp.where(forced, FORCE_BONUS, 0.0), -jnp.inf)
    slab = 8
    slabs = [score[v:v + slab] for v in range(0, nslc, slab)]
    cnts = [jnp.zeros((slab, tq), jnp.int32) for _ in slabs]
    jloc = lax.broadcasted_iota(jnp.int32, (slab, tq), 0)
    for jp in range(nslc):
        row = score[jp:jp + 1, :]
        for v, blk in enumerate(slabs):
            if v * slab > jp:
                ahead = row >= blk
            elif (v + 1) * slab <= jp:
                ahead = row > blk
            else:
                ahead = (row > blk) | ((row == blk) & (jloc > jp - v * slab))
            cnts[v] = cnts[v] + jnp.where(ahead, 1, 0)
    cnt = jnp.concatenate(cnts, axis=0)
    unsel = jnp.where((cnt < nsel) & valid_blk, 0.0, -SEL_MASK)

    marker_row = lax.broadcasted_iota(jnp.int32, (8, cols), 0) == 7
    aug = jnp.concatenate([jnp.concatenate([unsel] * HPG, axis=1),
                           jnp.zeros((HEAD_DIM - nslc - 8, cols), F32),
                           jnp.where(marker_row, -SEL_MASK, 0.0)], axis=0)
    rhs = jnp.concatenate([q_t, aug.astype(BF16)], axis=0)

    last = pl.multiple_of((qb + 1) * tq, tq)
    s = _dot(ksa[pl.ds(last, SEL_CHUNK), :], rhs) + bsel_ref[0]
    m = jnp.max(s, axis=0, keepdims=True)
    p = jnp.exp(s - m)
    st = (m, _dot(vst[:, pl.ds(last, SEL_CHUNK)], p.astype(BF16)))

    def far_start(c):
        return pl.multiple_of(jnp.maximum(last - (c + 1) * SEL_CHUNK, 0), tq)

    def far_logits(c):
        return _dot(ksa[pl.ds(far_start(c), SEL_CHUNK), :], rhs)

    def far_body(c, carry):
        state, s_c = carry
        s_next = far_logits(c + 1)
        return _softmax_update_t(state, s_c, vst[:, pl.ds(far_start(c), SEL_CHUNK)]), s_next

    (_, acc_s), _ = lax.fori_loop(0, qb // (SEL_CHUNK // tq), far_body, (st, far_logits(0)))
    o_s = acc_s[0:HEAD_DIM] / acc_s[HEAD_DIM:HEAD_DIM + 1]

    first = pl.multiple_of(qb * tq, tq)
    s = _dot(kwa[pl.ds(first, WINDOW + tq), :], rhs) + bwin_ref[0]
    p = jnp.exp(s - jnp.max(s, axis=0, keepdims=True))
    acc_w = _dot(vwt[:, pl.ds(first, WINDOW + tq)], p.astype(BF16))
    o_w = acc_w[0:HEAD_DIM] / acc_w[HEAD_DIM:HEAD_DIM + 1]

    gates_t = jax.nn.sigmoid(gate_ref[...]).T
    outs = []
    for h in range(HPG):
        hs = slice(h * tq, (h + 1) * tq)
        merged = (gates_t[h:h + 1, :] * o_c[:, hs] + gates_t[HPG + h:HPG + h + 1, :] * o_s[:, hs]
                  + gates_t[2 * HPG + h:2 * HPG + h + 1, :] * o_w[:, hs])
        outs.append(merged.T)
    o_ref[...] = jnp.concatenate(outs, axis=1).astype(o_ref.dtype)


def _nsa_attn(proj, kvc, band_t, imap_t, bias_sel, bias_win, n_batch, t):
    nqb = t // Q_BLOCK
    nseg = t // CMP_STRIDE
    nslc = t // SLC_LEN
    nsel = min(N_SEL, nslc)
    d_q = N_HEADS * HEAD_DIM
    gcols = HPG * HEAD_DIM
    kv0 = d_q // HEAD_DIM
    gate0 = kv0 + 6 * N_KV

    def kv_spec(branch, kv):
        return pl.BlockSpec((t, HEAD_DIM), lambda b, g, i: (b, kv0 + (branch * 2 + kv) * N_KV + g))

    return pl.pallas_call(
        functools.partial(_nsa_attn_kernel, nslc=nslc, nsel=nsel),
        out_shape=jax.ShapeDtypeStruct((n_batch * t, d_q), BF16),
        grid=(n_batch, N_KV, nqb),
        in_specs=[pl.BlockSpec((Q_BLOCK, gcols), lambda b, g, i: (b * nqb + i, g)),
                  pl.BlockSpec((1, 1, 1, nseg, HEAD_DIM), lambda b, g, i: (b, 0, g, 0, 0)),
                  pl.BlockSpec((1, 1, 1, nseg, HEAD_DIM), lambda b, g, i: (b, 1, g, 0, 0)),
                  pl.BlockSpec((HPG, 2 * nseg, Q_BLOCK), lambda b, g, i: (g, 0, 0)),
                  pl.BlockSpec((nslc, nseg), lambda b, g, i: (0, 0)),
                  kv_spec(1, 0), kv_spec(1, 1), kv_spec(2, 0), kv_spec(2, 1),
                  pl.BlockSpec((1, SEL_CHUNK, HPG * Q_BLOCK), lambda b, g, i: (g, 0, 0)),
                  pl.BlockSpec((1, WINDOW + Q_BLOCK, HPG * Q_BLOCK), lambda b, g, i: (g, 0, 0)),
                  pl.BlockSpec((Q_BLOCK, HEAD_DIM), lambda b, g, i: (b * nqb + i, gate0 + g))],
        out_specs=pl.BlockSpec((Q_BLOCK, gcols), lambda b, g, i: (b * nqb + i, g)),
        scratch_shapes=[pltpu.VMEM((KEY_PAD + t, 2 * HEAD_DIM), BF16), pltpu.VMEM((VAL_ROWS, KEY_PAD + t), BF16),
                        pltpu.VMEM((KEY_PAD + t, 2 * HEAD_DIM), BF16), pltpu.VMEM((VAL_ROWS, KEY_PAD + t), BF16)],
        compiler_params=_cparams(("parallel", "parallel", "arbitrary")),
        name="nsa_attn",
    )(proj, kvc, kvc, band_t, imap_t, proj, proj, proj, proj, bias_sel, bias_win, proj)


def _proj_res_kernel(a_ref, w_ref, x_ref, gt_ref, lg_ref, lb_ref, o_ref):
    y = _dot(a_ref[...], w_ref[...])
    z = DN_ALPHA * x_ref[...] + (1.0 + gt_ref[0]) * y
    o_ref[...] = _ln_rows(z, lg_ref[...], lb_ref[...])


def _proj_res(a, w, x, mod, mod_spec, gate_slot, ln_g, ln_b, *, layer, tm):
    m, d = x.shape
    k = a.shape[1]
    vec_spec = pl.BlockSpec((1, d), lambda i: (0, 0))
    return pl.pallas_call(
        _proj_res_kernel,
        out_shape=jax.ShapeDtypeStruct((m, d), F32),
        grid=(m // tm,),
        in_specs=[pl.BlockSpec((tm, k), lambda i: (i, 0)),
                  pl.BlockSpec((None, k, d), lambda i: (layer, 0, 0)),
                  pl.BlockSpec((tm, d), lambda i: (i, 0)),
                  mod_spec(gate_slot),
                  vec_spec, vec_spec],
        out_specs=pl.BlockSpec((tm, d), lambda i: (i, 0)),
        compiler_params=_cparams(("parallel",)),
        name="proj_res",
    )(a, w, x, mod, ln_g.reshape(1, d), ln_b.reshape(1, d))


def _rel_bucket_np(dist):
    dist = np.maximum(dist, 0)
    exact = REL_BUCKETS // 2
    logv = (np.log(np.maximum(dist, 1).astype(np.float32) / np.float32(exact))
            / np.float32(math.log(REL_MAX_DIST / exact))).astype(np.float32)
    large = exact + (logv * np.float32(REL_BUCKETS - exact)).astype(np.int32)
    return np.where(dist < exact, dist, np.minimum(large, REL_BUCKETS - 1)).astype(np.int32)


def _cmp_to_slc_np(nseg, nslc):
    i = np.arange(nseg)[:, None] * CMP_STRIDE
    j = np.arange(nslc)[None, :] * SLC_LEN
    ov = np.minimum(i + CMP_LEN, j + SLC_LEN) - np.maximum(i, j)
    return (np.maximum(ov, 0) / CMP_STRIDE).astype(np.float32)


def _bias_tables(rel_table, t):
    tbl_c = rel_table.T - rel_table[REL_BUCKETS - 1][:, None]
    nseg = t // CMP_STRIDE
    ql = np.arange(Q_BLOCK)[None, :]
    dist_c = ql - ((np.arange(2 * nseg)[:, None] - nseg) * CMP_STRIDE + CMP_LEN - 1)

    def lookup(dist):
        bucket = _rel_bucket_np(dist)
        out = jnp.broadcast_to(tbl_c[:, 0][:, None, None], (N_HEADS,) + dist.shape)
        for k in range(1, REL_BUCKETS):
            out = jnp.where(bucket == k, tbl_c[:, k][:, None, None], out)
        return out

    band_t = jnp.where(dist_c >= 0, lookup(dist_c), NEG_BIAS)

    def per_group(dist, ok):
        tab = jnp.where(ok, lookup(dist), NEG_MASK)
        n_keys = dist.shape[0]
        return tab.reshape(N_KV, HPG, n_keys, Q_BLOCK).transpose(0, 2, 1, 3).reshape(N_KV, n_keys, HPG * Q_BLOCK)

    dist_s = ql + (SEL_CHUNK - Q_BLOCK) - np.arange(SEL_CHUNK)[:, None]
    dist_w = ql + WINDOW - np.arange(WINDOW + Q_BLOCK)[:, None]
    return band_t, per_group(dist_s, dist_s >= 0), per_group(dist_w, (dist_w >= 0) & (dist_w < WINDOW))


CMP_PAGES_PER_STEP = 16


def _compress_paged_kernel(pt_ref, *refs, n_pg):
    pages = refs[:n_pg]
    pe_ref, w1a_ref, w1b_ref, w2_ref, o_ref, carry_ref = refs[n_pg:]
    n_c = 2 * N_KV
    seg_rows = CMP_STRIDE * n_c
    segs = PAGE_SIZE // CMP_STRIDE
    rows = n_pg * segs * n_c
    hid_w = w2_ref.shape[0]

    @pl.when(pl.program_id(1) == 0)
    def _():
        carry_ref[...] = jnp.zeros_like(carry_ref)

    a = jnp.concatenate(
        [jnp.concatenate(
            [jnp.concatenate([pg[0, 0, seg_rows * s + n_c * r:seg_rows * s + n_c * (r + 1), :]
                              for s in range(segs)], axis=0)
             for r in range(CMP_STRIDE)], axis=1)
         for pg in pages], axis=0).astype(BF16)
    is_k = lax.broadcasted_iota(jnp.int32, (rows, 1), 0) % n_c < N_KV
    is_k8 = lax.broadcasted_iota(jnp.int32, (n_c, 1), 0) < N_KV

    def pick(y, width, k_rows):
        return jnp.where(k_rows, y[:, 0:width], y[:, width:2 * width])

    p0 = pick(_dot(a, w1a_ref[...]), hid_w, is_k)
    p1 = pick(_dot(a, w1b_ref[...]), hid_w, is_k)
    half = CMP_STRIDE * HEAD_DIM
    pe = pe_ref[...].astype(BF16)
    peh = pick(_dot(pe[:, 0:half], w1a_ref[...]) + _dot(pe[:, half:2 * half], w1b_ref[...]), hid_w, is_k8)
    shifted = jnp.concatenate([carry_ref[...], p0[0:rows - n_c]], axis=0)
    carry_ref[...] = p0[rows - n_c:rows]
    hid = (shifted + p1).reshape(rows // n_c, n_c, hid_w) + peh[None]
    y = jax.nn.gelu(hid.reshape(rows, hid_w)).astype(BF16)
    o_ref[0] = pick(_dot(y, w2_ref[...]), HEAD_DIM, is_k)


def _compress_paged(cache_pages, page_table, page0, pe, w1, w2):
    n_s, n_pages = page_table.shape
    n_pg = min(CMP_PAGES_PER_STEP, n_pages)
    n_c = 2 * N_KV
    half = CMP_STRIDE * HEAD_DIM
    rows_pg = PAGE_SIZE * n_c
    out_rows = n_pg * (PAGE_SIZE // CMP_STRIDE) * n_c
    hid_w = w1.shape[2]
    w1a = jnp.concatenate([w1[0, :half], w1[1, :half]], axis=1)
    w1b = jnp.concatenate([w1[0, half:], w1[1, half:]], axis=1)
    w2c = jnp.concatenate([w2[0], w2[1]], axis=1)
    pe8 = jnp.repeat(pe.reshape(2, 2 * half), N_KV, axis=0)

    def page_spec(p):
        return pl.BlockSpec((1, 1, rows_pg, HEAD_DIM), lambda b, c, pt: (0, page0 + pt[b, c * n_pg + p], 0, 0))

    const = lambda b, c, pt: (0, 0)
    grid_spec = pltpu.PrefetchScalarGridSpec(
        num_scalar_prefetch=1,
        grid=(n_s, n_pages // n_pg),
        in_specs=[page_spec(p) for p in range(n_pg)] + [
            pl.BlockSpec((n_c, 2 * half), const),
            pl.BlockSpec((half, 2 * hid_w), const),
            pl.BlockSpec((half, 2 * hid_w), const),
            pl.BlockSpec((hid_w, 2 * HEAD_DIM), const)],
        out_specs=pl.BlockSpec((1, out_rows, HEAD_DIM), lambda b, c, pt: (b, c, 0)),
        scratch_shapes=[pltpu.VMEM((n_c, hid_w), F32)])
    return pl.pallas_call(
        functools.partial(_compress_paged_kernel, n_pg=n_pg),
        out_shape=jax.ShapeDtypeStruct((n_s, n_pages * (PAGE_SIZE // CMP_STRIDE) * n_c, HEAD_DIM), F32),
        grid_spec=grid_spec,
        compiler_params=_cparams(("parallel", "arbitrary")),
        name="compress_paged",
    )(page_table, *([cache_pages[None]] * n_pg), pe8, w1a, w1b, w2c)


def _sample_cmp_kernel(q_ref, kv_ref, bias_ref, im_ref, oc_ref, idx_ref, *, nslc, nsel, pos):
    n_c = 2 * N_KV
    q = (q_ref[0] * ATT_SCALE).astype(BF16)
    kv = kv_ref[0].astype(BF16)
    bias = bias_ref[...]
    valid = bias > 0.5 * NEG_BIAS
    s = _dot_nt(q, kv) + bias
    m = jnp.max(s, axis=-1, keepdims=True)
    e = jnp.where(valid, jnp.exp(s - m), 0.0)
    den = jnp.sum(e, axis=-1, keepdims=True)
    p = e / jnp.where(den > 0.0, den, 1.0)
    oc_ref[0] = _dot(pltpu.roll(p, N_KV, 1).astype(BF16), kv)

    p_hi = p.astype(BF16)
    p_lo = (p - p_hi.astype(F32)).astype(BF16)
    imp_h = _dot(p_hi, im_ref[...]) + _dot(p_lo, im_ref[...])
    lanes = im_ref.shape[1]
    head_grp = lax.broadcasted_iota(jnp.int32, (N_HEADS, lanes), 0) // HPG
    row8 = lax.broadcasted_iota(jnp.int32, (n_c, lanes), 0)
    imp = jnp.zeros((n_c, lanes), F32)
    for g in range(N_KV):
        imp_g = jnp.sum(jnp.where(head_grp == g, imp_h, 0.0), axis=0, keepdims=True)
        imp = imp + jnp.where(row8 == g, imp_g, 0.0)
    jidx = lax.broadcasted_iota(jnp.int32, (n_c, lanes), 1)
    cur = pos // SLC_LEN
    valid_blk = (jidx * SLC_LEN <= pos) & (jidx < nslc)
    forced = (jidx == 0) | (jidx == cur) | (jidx == cur - 1)
    score = jnp.where(valid_blk, imp + jnp.where(forced, FORCE_BONUS, 0.0), -jnp.inf)
    cnt = jnp.zeros((n_c, lanes), jnp.int32)
    for jp in range(nslc):
        col = score[:, jp:jp + 1]
        ahead = (col > score) | ((col == score) & (jidx > jp))
        cnt = cnt + jnp.where(ahead, 1, 0)
    out_lane = lax.broadcasted_iota(jnp.int32, (n_c, HEAD_DIM), 1)
    jf = jidx.astype(F32)
    out = jnp.zeros((n_c, HEAD_DIM), F32)
    for r in range(nsel):
        blk_r = jnp.sum(jnp.where((cnt == r) & valid_blk, jf, 0.0), axis=-1, keepdims=True)
        out = jnp.where(out_lane == r, blk_r, out)
    idx_ref[0] = out.astype(jnp.int32)


def _sample_cmp(q_heads, kvc, bias, imap, *, nslc, nsel, pos):
    n_s, rows, _ = kvc.shape
    lanes = imap.shape[1]
    return pl.pallas_call(
        functools.partial(_sample_cmp_kernel, nslc=nslc, nsel=nsel, pos=pos),
        out_shape=(jax.ShapeDtypeStruct((n_s, N_HEADS, HEAD_DIM), F32),
                   jax.ShapeDtypeStruct((n_s, 2 * N_KV, HEAD_DIM), jnp.int32)),
        grid=(n_s,),
        in_specs=[pl.BlockSpec((1, N_HEADS, HEAD_DIM), lambda b: (b, 0, 0)),
                  pl.BlockSpec((1, rows, HEAD_DIM), lambda b: (b, 0, 0)),
                  pl.BlockSpec((N_HEADS, rows), lambda b: (0, 0)),
                  pl.BlockSpec((rows, lanes), lambda b: (0, 0))],
        out_specs=(pl.BlockSpec((1, N_HEADS, HEAD_DIM), lambda b: (b, 0, 0)),
                   pl.BlockSpec((1, 2 * N_KV, HEAD_DIM), lambda b: (b, 0, 0))),
        compiler_params=_cparams(("parallel",)),
        name="sample_cmp",
    )(q_heads, kvc, bias, imap)


def _sample_sel_win_kernel(idx_ref, pt_ref, *refs, n_slot, pos, n_past_blk, buf_len, thresholds):
    slots = refs[:n_slot]
    q_ref, new_ref, win_ref, tb_ref, oc_ref, gate_ref, o_ref = refs[n_slot:]
    b = pl.program_id(0)
    g = pl.program_id(1)
    n_c = 2 * N_KV
    blk_lanes = SLC_LEN * n_c
    q = (q_ref[0, 0] * ATT_SCALE).astype(BF16)
    tb = tb_ref[0]

    def bias_of(dist):
        d = jnp.maximum(dist, 0)
        out = jnp.zeros(d.shape, F32) + tb[:, 0:1]
        for k in range(1, REL_BUCKETS):
            out = out + jnp.where(d >= thresholds[k], tb[:, k:k + 1] - tb[:, k - 1:k], 0.0)
        return out

    def softmax(s, ok):
        s = jnp.where(ok, s, NEG_BIAS)
        m = jnp.max(s, axis=-1, keepdims=True)
        e = jnp.where(ok, jnp.exp(s - m), 0.0)
        den = jnp.sum(e, axis=-1, keepdims=True)
        return e / jnp.where(den > 0.0, den, 1.0)

    new_rows = new_ref[0, 0]
    s_new = _dot_nt(q, new_rows.astype(BF16))
    lane_new = lax.broadcasted_iota(jnp.int32, s_new.shape, 1)
    bias_new = bias_of(jnp.zeros(s_new.shape, jnp.int32))

    lane = lax.broadcasted_iota(jnp.int32, (n_c, blk_lanes), 1)
    mine = lane % n_c == g
    r_in = lane // n_c
    base = (b * N_KV + g) * n_slot
    s_parts, d_parts, ok_parts = [], [], []
    n_new = 0
    for i in range(n_slot):
        j = idx_ref[base + i]
        dist = jnp.where(j < n_past_blk, pos - j * SLC_LEN, -1) - r_in
        s_parts.append(_dot_nt(q, slots[i][0, 0].astype(BF16)))
        d_parts.append(dist)
        ok_parts.append(mine & (dist >= 0))
        n_new = n_new + jnp.where(j >= n_past_blk, 1, 0)
    s_all = jnp.concatenate(s_parts + [s_new], axis=1)
    ok_all = jnp.concatenate(ok_parts + [lane_new == jnp.where(n_new > 0, 0, -1)], axis=1)
    b_all = jnp.concatenate([bias_of(jnp.concatenate(d_parts, axis=1)), bias_new], axis=1)
    p = softmax(s_all + b_all, ok_all)
    n_old = n_slot * blk_lanes
    p_v = pltpu.roll(p[:, 0:n_old], N_KV, 1).astype(BF16)
    o_s = p[:, n_old:n_old + 1] * new_rows[1:2, :]
    for i in range(n_slot):
        o_s = o_s + _dot(p_v[:, i * blk_lanes:(i + 1) * blk_lanes], slots[i][0, 0].astype(BF16))

    win = win_ref[0].astype(BF16)
    lane_w = lax.broadcasted_iota(jnp.int32, (n_c, buf_len * n_c), 1)
    dist_w = buf_len - lane_w // n_c
    ok_w = (lane_w % n_c == g) & (dist_w >= 0) & (dist_w < WINDOW)
    s_w = jnp.concatenate([_dot_nt(q, win) + bias_of(dist_w), s_new + bias_new], axis=1)
    p = softmax(s_w, jnp.concatenate([ok_w, lane_new == 2], axis=1))
    n_old = buf_len * n_c
    o_w = _dot(pltpu.roll(p[:, 0:n_old], N_KV, 1).astype(BF16), win) + p[:, n_old + 2:n_old + 3] * new_rows[3:4, :]

    gates = jax.nn.sigmoid(gate_ref[0, 0])
    o_ref[0, 0] = gates[:, 0:1] * oc_ref[0, 0] + gates[:, 1:2] * o_s + gates[:, 2:3] * o_w


def _sample_sel_win(idx, page_table, slc_halves, page0, q_g, new_rows, win_rows, tb, o_c, gates, *, pos, nsel):
    n_s, n_pages = page_table.shape
    n_c = 2 * N_KV
    per_page = PAGE_SIZE // SLC_LEN
    n_past_blk = n_pages * per_page
    buf_len = win_rows.shape[1] // n_c
    blk_rows = SLC_LEN * n_c
    thresholds = tuple(int(np.argmax(_rel_bucket_np(np.arange(4 * REL_MAX_DIST)) >= k)) for k in range(REL_BUCKETS))

    def slot_spec(i):
        def index_map(b, g, idx_ref, pt_ref):
            j = jnp.minimum(idx_ref[(b * N_KV + g) * nsel + i], n_past_blk - 1)
            return (page0 + pt_ref[b, j // per_page], j % per_page, 0, 0)
        return pl.BlockSpec((1, 1, blk_rows, HEAD_DIM), index_map)

    grp = lambda b, g, idx_ref, pt_ref: (b, g, 0, 0)
    grid_spec = pltpu.PrefetchScalarGridSpec(
        num_scalar_prefetch=2,
        grid=(n_s, N_KV),
        in_specs=[slot_spec(i) for i in range(nsel)] + [
            pl.BlockSpec((1, 1, n_c, HEAD_DIM), grp),
            pl.BlockSpec((1, 1, HEAD_DIM, HEAD_DIM), grp),
            pl.BlockSpec((1, buf_len * n_c, HEAD_DIM), lambda b, g, idx_ref, pt_ref: (b, 0, 0)),
            pl.BlockSpec((1, n_c, HEAD_DIM), lambda b, g, idx_ref, pt_ref: (g, 0, 0)),
            pl.BlockSpec((1, 1, n_c, HEAD_DIM), grp),
            pl.BlockSpec((1, 1, n_c, HEAD_DIM), grp)],
        out_specs=pl.BlockSpec((1, 1, n_c, HEAD_DIM), grp))
    return pl.pallas_call(
        functools.partial(_sample_sel_win_kernel, n_slot=nsel, pos=pos, n_past_blk=n_past_blk, buf_len=buf_len,
                          thresholds=thresholds),
        out_shape=jax.ShapeDtypeStruct((n_s, N_KV, n_c, HEAD_DIM), F32),
        grid_spec=grid_spec,
        compiler_params=_cparams(("parallel", "parallel")),
        name="sample_sel_win",
    )(idx, page_table, *([slc_halves] * nsel), q_g, new_rows, win_rows, tb, o_c, gates)


def _sample_nsa(proj_s, cmp_pages, slc_halves, page0, win_buf, page_table, pe, w1, w2, rel_table):
    n_s, n_pages = page_table.shape
    past = n_pages * PAGE_SIZE
    pos = past
    n_c = 2 * N_KV
    d_q = N_HEADS * HEAD_DIM
    kvw = n_c * HEAD_DIM
    nseg = past // CMP_STRIDE
    nslc = -(-(past + 1) // SLC_LEN)
    nsel = min(N_SEL, nslc)
    kvc = _compress_paged(cmp_pages, page_table, page0, pe, w1, w2)
    x = np.arange(nseg)[:, None]
    c = np.arange(n_c)[None, :]
    dist = pos - ((x - 1) * CMP_STRIDE + CMP_LEN - 1)
    head_grp = np.arange(N_HEADS)[:, None, None] // HPG
    ok = ((x >= 1) & (dist >= 0))[None] & (c[None] == head_grp)
    bias = jnp.where(ok, jnp.take(rel_table.T, _rel_bucket_np(np.broadcast_to(dist, (nseg, n_c))), axis=1),
                     NEG_BIAS).reshape(N_HEADS, nseg * n_c)
    lanes = -(-nslc // HEAD_DIM) * HEAD_DIM
    imap = np.zeros((nseg, n_c, lanes), np.float32)
    imap[1:, :, :nslc] = _cmp_to_slc_np(nseg - 1, nslc)[:, None, :]
    q_heads = proj_s[:, :d_q].reshape(n_s, N_HEADS, HEAD_DIM)
    o_c, idx = _sample_cmp(q_heads, kvc, bias, jnp.asarray(imap.reshape(nseg * n_c, lanes)).astype(BF16),
                           nslc=nslc, nsel=nsel, pos=pos)
    pad_heads = lambda a: jnp.pad(a.reshape(n_s, N_KV, HPG, -1), ((0, 0), (0, 0), (0, n_c - HPG), (0, 0)))
    kv_new = proj_s[:, d_q:d_q + 3 * kvw].reshape(n_s, 3, 2, N_KV, HEAD_DIM)
    new_rows = kv_new[:, 1:3].transpose(0, 3, 1, 2, 4).reshape(n_s, N_KV, 4, HEAD_DIM)
    new_rows = jnp.pad(new_rows, ((0, 0), (0, 0), (0, HEAD_DIM - 4), (0, 0)))
    gates = proj_s[:, d_q + 3 * kvw:].reshape(n_s, N_KV, HEAD_DIM)[:, :, :3 * HPG].reshape(n_s, N_KV, 3, HPG)
    gates = jnp.pad(gates.transpose(0, 1, 3, 2), ((0, 0), (0, 0), (0, n_c - HPG), (0, HEAD_DIM - 3)))
    tb = jnp.pad(rel_table.T.reshape(N_KV, HPG, REL_BUCKETS), ((0, 0), (0, n_c - HPG), (0, HEAD_DIM - REL_BUCKETS)))
    o = _sample_sel_win(idx[:, :N_KV, :nsel].reshape(-1), page_table, slc_halves, page0,
                        pad_heads(proj_s[:, :d_q]), new_rows, win_buf.reshape(n_s, -1, HEAD_DIM), tb,
                        pad_heads(o_c), gates, pos=pos, nsel=nsel)
    new_win = jnp.concatenate([win_buf, kv_new[:, 2][:, None]], axis=1)[:, -min(WINDOW, win_buf.shape[1] + 1):]
    return o[:, :, :HPG].reshape(n_s, d_q), kv_new, new_win


def _pad_rows(a, rows):
    return jnp.pad(a, ((0, rows - a.shape[0]),) + ((0, 0),) * (a.ndim - 1))


def _nsa_w_in_layout(w):
    n_l, d, _ = w.shape
    n_main = N_HEADS * HEAD_DIM + 6 * N_KV * HEAD_DIM
    wg = w[:, :, n_main:].reshape(n_l, d, 3, N_KV, HPG).transpose(0, 1, 3, 2, 4).reshape(n_l, d, N_KV, 3 * HPG)
    wg = jnp.pad(wg, ((0, 0), (0, 0), (0, 0), (0, HEAD_DIM - 3 * HPG))).reshape(n_l, d, N_KV * HEAD_DIM)
    return jnp.concatenate([w[:, :, :n_main], wg], axis=2).astype(BF16)


def kernel(x_prompt, x_sample, cache_cmp_kv, cache_slc_kv, state_win_kv, page_table, c_prompt, c_sample,
           ada_w, ada_b, ln_g, ln_b, ffn_pre_w_in, ffn_pre_w_out, ffn_post_w_in, ffn_post_w_out,
           gmlp_w_in, gmlp_ln_g, gmlp_ln_b, gmlp_w_s, gmlp_b_s, gmlp_w_out,
           nsa_w_in, nsa_cmp_pe, nsa_cmp_w1, nsa_cmp_w2, nsa_w_out, rel_table):
    n_b, t, d = x_prompt.shape
    n_s = x_sample.shape[0]
    n_phys = cache_cmp_kv.shape[1]
    d_q = N_HEADS * HEAD_DIM
    kvw = 2 * N_KV * HEAD_DIM
    tm_p = min(512, t)
    tm_f = min(512, t)
    tm_mm = min(1024, t)
    tm_g = min(512, t)

    xp = x_prompt.reshape(n_b * t, d)
    place = lambda a: jnp.pad(a, ((n_b, SAMPLE_ROWS - n_b - n_s), (0, 0)))
    xs = place(x_sample.reshape(n_s, d))
    c_all = _pad_rows(jnp.concatenate([c_prompt, c_sample], axis=0), SAMPLE_ROWS)
    mod = _ada(c_all, ada_w, ada_b)
    mod_rows = mod.reshape(DEPTH * SAMPLE_ROWS * N_ADA, 1, d)

    def prompt_mod(layer, tm):
        return lambda k: pl.BlockSpec(
            (1, 1, d), lambda i, *_: ((layer * SAMPLE_ROWS + i // (t // tm)) * N_ADA + k, 0, 0))

    def sample_mod(layer):
        return lambda k: pl.BlockSpec((1, SAMPLE_ROWS, d), lambda i, *_: (layer, 0, k))

    band_t, bias_sel, bias_win = _bias_tables(rel_table, t)
    imap_t = jnp.asarray(_cmp_to_slc_np(t // CMP_STRIDE, t // SLC_LEN).T).astype(BF16)
    cmp_pages = cache_cmp_kv.reshape(-1, PAGE_SIZE * 2 * N_KV, HEAD_DIM)
    slc_halves = cache_slc_kv.reshape(-1, PAGE_SIZE // SLC_LEN, SLC_LEN * 2 * N_KV, HEAD_DIM)

    pre_w_in, pre_w_out = ffn_pre_w_in.astype(BF16), ffn_pre_w_out.astype(BF16)
    post_w_in, post_w_out = ffn_post_w_in.astype(BF16), ffn_post_w_out.astype(BF16)
    gw_in, gw_out = gmlp_w_in.astype(BF16), gmlp_w_out.astype(BF16)
    nw_in, nw_out = _nsa_w_in_layout(nsa_w_in), nsa_w_out.astype(BF16)

    cmp_p, cmp_s, slc_p, slc_s, win_p, win_s, gv_s = [], [], [], [], [], [], []
    for i in range(DEPTH):
        ms = sample_mod(i)
        xp = _ffn(xp, mod_rows, prompt_mod(i, tm_f), (0, 1, 2), pre_w_in, pre_w_out, ln_g[i, 0], ln_b[i, 0],
                  layer=i, tm=tm_f)
        xs = _ffn(xs, mod, ms, (0, 1, 2), pre_w_in, pre_w_out, ln_g[i, 0], ln_b[i, 0], layer=i, tm=SAMPLE_ROWS)

        if i % 2 == 0:
            a = i // 2
            xp, = _gmlp(xp, mod_rows, prompt_mod(i, tm_g), (3, 4, 5), gw_in, gmlp_ln_g[a], gmlp_ln_b[a], gmlp_w_s[a],
                        gmlp_b_s[a], gw_out, ln_g[i, 1], ln_b[i, 1], layer=a, tm=tm_g, chunk=min(CHUNK, t),
                        emit_vn=False)
            xs, vn_s = _gmlp(xs, mod, ms, (3, 4, 5), gw_in, gmlp_ln_g[a], gmlp_ln_b[a], gmlp_w_s[a], gmlp_b_s[a],
                             gw_out, ln_g[i, 1], ln_b[i, 1], layer=a, tm=SAMPLE_ROWS, chunk=1, emit_vn=True)
            gv_s.append(vn_s[n_b:n_b + n_s].reshape(n_s, 1, -1))
        else:
            a = i // 2
            w1 = nsa_cmp_w1[a].astype(BF16)
            w2 = nsa_cmp_w2[a].astype(BF16)
            proj = _mod_mm(xp, mod_rows, prompt_mod(i, tm_mm), (3, 4), nw_in, layer=a, tm=tm_mm, tn=512)
            kvc = _compress(proj, n_b, t, d_q // HEAD_DIM, nsa_cmp_pe[a], w1, w2)
            o = _nsa_attn(proj, kvc, band_t, imap_t, bias_sel, bias_win, n_b, t)
            xp = _proj_res(o, nw_out, xp, mod_rows, prompt_mod(i, tm_p), 5, ln_g[i, 1], ln_b[i, 1], layer=a, tm=tm_p)
            kv_rows = lambda branch, rows: rows[..., d_q + branch * kvw:d_q + (branch + 1) * kvw].reshape(
                n_b, -1, 2, N_KV, HEAD_DIM)
            cmp_p.append(kv_rows(0, proj))
            slc_p.append(kv_rows(1, proj))
            win_p.append(kv_rows(2, proj.reshape(n_b, t, -1)[:, -min(WINDOW, t):]))
            proj_s = _mod_mm(xs, mod, ms, (3, 4), nw_in, layer=a, tm=SAMPLE_ROWS, tn=512)[n_b:n_b + n_s]
            o_samp, kv_new, new_win = _sample_nsa(proj_s, cmp_pages, slc_halves, a * n_phys, state_win_kv[a],
                                                  page_table, nsa_cmp_pe[a], w1, w2, rel_table)
            xs = _proj_res(place(o_samp).astype(BF16), nw_out, xs, mod, ms, 5, ln_g[i, 1], ln_b[i, 1],
                           layer=a, tm=SAMPLE_ROWS)
            cmp_s.append(kv_new[:, 0][:, None])
            slc_s.append(kv_new[:, 1][:, None])
            win_s.append(new_win)

        xp = _ffn(xp, mod_rows, prompt_mod(i, tm_f), (6, 7, 8), post_w_in, post_w_out, ln_g[i, 2], ln_b[i, 2],
                  layer=i, tm=tm_f)
        xs = _ffn(xs, mod, ms, (6, 7, 8), post_w_in, post_w_out, ln_g[i, 2], ln_b[i, 2], layer=i, tm=SAMPLE_ROWS)

    return (xp.reshape(n_b, t, d), xs[n_b:n_b + n_s].reshape(n_s, 1, d),
            jnp.stack(cmp_p), jnp.stack(cmp_s), jnp.stack(slc_p), jnp.stack(slc_s),
            jnp.stack(win_p), jnp.stack(win_s), jnp.stack(gv_s))
```
